```python
import jax, jax.numpy as jnp
from jax import lax
import numpy as np

D_MODEL = 1024
BATCH = 16
SEQ = 2048
DEPTH = 2

EPS = 1e-6
HEAD_DIM = 64
MIX_WIDTH = D_MODEL
N_GROUPS = 4
GROUP_WIDTH = MIX_WIDTH // N_GROUPS
POOL_WINDOWS = (2, 4, 8, 16)
POOL_GROUP = GROUP_WIDTH // len(POOL_WINDOWS)
SWA_Q_HEADS = GROUP_WIDTH // HEAD_DIM
SWA_KV_HEADS = 2
SWA_GQA = SWA_Q_HEADS // SWA_KV_HEADS
SWA_WINDOW = 128
SWA_BLOCK = SWA_WINDOW
CONV_WIDTH = 3
DIFF_HEADS = GROUP_WIDTH // HEAD_DIM
DIFF_QK_DIM = HEAD_DIM // 2
DIFF_V_DIM = HEAD_DIM
Q_BLOCK = 128
SPLITS = (GROUP_WIDTH,
          SWA_Q_HEADS * HEAD_DIM, SWA_KV_HEADS * HEAD_DIM, SWA_KV_HEADS * HEAD_DIM,
          GROUP_WIDTH, GROUP_WIDTH, GROUP_WIDTH,
          DIFF_HEADS * 2 * DIFF_QK_DIM, DIFF_HEADS * 2 * DIFF_QK_DIM, DIFF_HEADS * DIFF_V_DIM)
IN_PROJ_WIDTH = sum(SPLITS)
PEER_HEADS = 8
PEER_KEYS = 128
PEER_EXPERTS = PEER_KEYS * PEER_KEYS
PEER_QUERY_DIM = 256
PEER_HALF = PEER_QUERY_DIM // 2
PEER_TOPK = 16
PEER_TOKEN_BLOCK = 128

kernel_name = "hymba_pool_swa_conv_diff_peer"


def rms_norm(x, g):
    xf = x.astype(jnp.float32)
    y = xf * lax.rsqrt(jnp.mean(xf * xf, axis=-1, keepdims=True) + EPS)
    return (y * g.astype(jnp.float32)).astype(x.dtype)


def head_norm(y, g):
    b, s, w = y.shape
    nh = w // HEAD_DIM
    return rms_norm(y.reshape(b, s, nh, HEAD_DIM), g.reshape(nh, HEAD_DIM)).reshape(b, s, w)


def split_cols(z, sizes):
    out, off = [], 0
    for n in sizes:
        out.append(z[..., off:off + n])
        off += n
    return out


def alibi_slopes(n):
    return jnp.exp2(-8.0 * jnp.arange(1, n + 1, dtype=jnp.float32) / n)


def pool_mixer(p, w_pool):
    b, s, _ = p.shape
    pf = p.astype(jnp.float32).reshape(b, s, len(POOL_WINDOWS), POOL_GROUP)
    cs = jnp.cumsum(pf, axis=1)
    t = jnp.arange(s)
    pooled = []
    for gi, w in enumerate(POOL_WINDOWS):
        c = cs[:, :, gi]
        lagged = jnp.pad(c, ((0, 0), (w, 0), (0, 0)))[:, :s]
        cnt = jnp.minimum(t + 1, w).astype(jnp.float32)[None, :, None]
        pooled.append((c - lagged) / cnt)
    d = (jnp.stack(pooled, axis=2) - pf).astype(p.dtype)
    return jnp.einsum('bsgc,gcd->bsgd', d, w_pool).reshape(b, s, GROUP_WIDTH)


def swa_sink_attention(q, k, v, qn_g, kn_g, sinks, slopes):
    b, s, _ = q.shape
    nb = s // SWA_BLOCK
    q = rms_norm(q.reshape(b, s, SWA_KV_HEADS, SWA_GQA, HEAD_DIM), qn_g)
    k = rms_norm(k.reshape(b, s, SWA_KV_HEADS, HEAD_DIM), kn_g)
    v = v.reshape(b, s, SWA_KV_HEADS, HEAD_DIM)
    qb = q.reshape(b, nb, SWA_BLOCK, SWA_KV_HEADS, SWA_GQA, HEAD_DIM)

    def banded(a):
        ap = jnp.pad(a, ((0, 0), (SWA_BLOCK, 0), (0, 0), (0, 0)))
        blk = ap.reshape(b, nb + 1, SWA_BLOCK, SWA_KV_HEADS, HEAD_DIM)
        return jnp.concatenate([blk[:, :-1], blk[:, 1:]], axis=2)

    kb, vb = banded(k), banded(v)
    sc = jnp.einsum('bnqhgd,bnkhd->bhgnqk', qb, kb).astype(jnp.float32) * (HEAD_DIM ** -0.5)
    i = jnp.arange(SWA_BLOCK)[:, None]
    j = jnp.arange(2 * SWA_BLOCK)[None, :]
    dist = i + SWA_BLOCK - j
    kpos = jnp.arange(nb)[:, None, None] * SWA_BLOCK - SWA_BLOCK + j
    valid = (dist >= 0) & (dist < SWA_WINDOW) & (kpos >= 0)
    m_h = slopes.reshape(SWA_KV_HEADS, SWA_GQA)[None, :, :, None, None, None]
    sc = sc - m_h * dist.astype(jnp.float32)
    sc = jnp.where(valid[None, None, None], sc, -jnp.inf)
    sink = sinks.astype(jnp.float32).reshape(SWA_KV_HEADS, SWA_GQA)[None, :, :, None, None, None]
    mx = jnp.maximum(jnp.max(sc, axis=-1, keepdims=True), sink)
    e = jnp.exp(sc - mx)
    pr = e / (jnp.sum(e, axis=-1, keepdims=True) + jnp.exp(sink - mx))
    o = jnp.einsum('bhgnqk,bnkhd->bnqhgd', pr.astype(v.dtype), vb)
    return o.reshape(b, s, SWA_Q_HEADS * HEAD_DIM)


def short_conv_mixer(h, gate_b, gate_c, conv_w):
    z = gate_c * h
    y = lax.conv_general_dilated(z, conv_w[:, None, :].astype(z.dtype), window_strides=(1,),
                                 padding=[(CONV_WIDTH - 1, 0)],
                                 dimension_numbers=('NWC', 'WIO', 'NWC'),
                                 feature_group_count=GROUP_WIDTH)
    return gate_b * y


def diff_attention(q, k, v, qn_g, kn_g, lam, slopes):
    b, s, _ = q.shape
    nb = s // Q_BLOCK
    q = rms_norm(q.reshape(b, s, DIFF_HEADS, 2, DIFF_QK_DIM), qn_g)
    k = rms_norm(k.reshape(b, s, DIFF_HEADS, 2, DIFF_QK_DIM), kn_g)
    v = v.reshape(b, s, DIFF_HEADS, DIFF_V_DIM)
    qb = jnp.moveaxis(q.reshape(b, nb, Q_BLOCK, DIFF_HEADS, 2, DIFF_QK_DIM), 1, 0)
    kpos = jnp.arange(s)
    m_h = slopes[None, :, None, None, None]

    def block(args):
        qi, n = args
        qpos = n * Q_BLOCK + jnp.arange(Q_BLOCK)
        sc = jnp.einsum('bqhmd,bkhmd->bhmqk', qi, k).astype(jnp.float32) * (DIFF_QK_DIM ** -0.5)
        dist = qpos[:, None] - kpos[None, :]
        sc = sc - m_h * dist.astype(jnp.float32)
        sc = jnp.where(dist >= 0, sc, -jnp.inf)
        a = jax.nn.softmax(sc, axis=-1)
        w = a[:, :, 0] - lam * a[:, :, 1]
        return jnp.einsum('bhqk,bkhd->bqhd', w.astype(v.dtype), v)

    o = lax.map(block, (qb, jnp.arange(nb)))
    return jnp.moveaxis(o, 0, 1).reshape(b, s, DIFF_HEADS * DIFF_V_DIM)


def peer_ffn(xn, w_query, sub_keys, peer_u, peer_v):
    b, s, d = xn.shape
    t = b * s
    xt = xn.reshape(t, d)
    q = (xt @ w_query).reshape(t, PEER_HEADS, 2, PEER_HALF)
    sk = jnp.einsum('thcd,ckd->thck', q, sub_keys).astype(jnp.float32)
    top_s, top_i = lax.top_k(sk, PEER_TOPK)
    cand_s = top_s[:, :, 0, :, None] + top_s[:, :, 1, None, :]
    cand_i = top_i[:, :, 0, :, None] * PEER_KEYS + top_i[:, :, 1, None, :]
    cand_s = cand_s.reshape(t, PEER_HEADS, PEER_TOPK * PEER_TOPK)
    cand_i = cand_i.reshape(t, PEER_HEADS, PEER_TOPK * PEER_TOPK)
    best_s, best_pos = lax.top_k(cand_s, PEER_TOPK)
    idx = jnp.take_along_axis(cand_i, best_pos, axis=-1)
    gate = jax.nn.softmax(best_s, axis=-1)
    nblk = t // PEER_TOKEN_BLOCK

    def block(args):
        xb, ib, gb = args
        h = jnp.einsum('thkd,td->thk', peer_u[ib], xb).astype(jnp.float32)
        a = (gb * jax.nn.gelu(h, approximate=False)).astype(xb.dtype)
        return jnp.einsum('thk,thkd->td', a, peer_v[ib])

    out = lax.map(block, (xt.reshape(nblk, PEER_TOKEN_BLOCK, d),
                          idx.reshape(nblk, PEER_TOKEN_BLOCK, PEER_HEADS, PEER_TOPK),
                          gate.reshape(nblk, PEER_TOKEN_BLOCK, PEER_HEADS, PEER_TOPK)))
    return out.reshape(b, s, d).astype(xn.dtype)


def hybrid_layer(x, layer_idx, norm1_g, w_in, w_pool, swa_q_norm, swa_k_norm, swa_sinks, conv_w,
                 diff_q_norm, diff_k_norm, lam_q1, lam_k1, lam_q2, lam_k2, out_norm_g, w_out,
                 norm2_g, w_query, sub_keys, peer_u, peer_v):
    slopes = alibi_slopes(SWA_Q_HEADS + DIFF_HEADS)
    lam_init = 0.8 - 0.6 * float(np.exp(-0.3 * layer_idx))
    lam = (jnp.exp(jnp.sum(lam_q1.astype(jnp.float32) * lam_k1.astype(jnp.float32)))
           - jnp.exp(jnp.sum(lam_q2.astype(jnp.float32) * lam_k2.astype(jnp.float32))) + lam_init)

    xn = rms_norm(x, norm1_g)
    z = xn @ w_in
    p_a, q_b, k_b, v_b, h_c, b_c, c_c, q_d, k_d, v_d = split_cols(z, SPLITS)

    y_a = pool_mixer(p_a, w_pool)
    y_b = swa_sink_attention(q_b, k_b, v_b, swa_q_norm, swa_k_norm, swa_sinks, slopes[:SWA_Q_HEADS])
    y_c = short_conv_mixer(h_c, b_c, c_c, conv_w)
    y_d = diff_attention(q_d, k_d, v_d, diff_q_norm, diff_k_norm, lam, slopes[SWA_Q_HEADS:])

    g_a, g_b, g_c, g_d = split_cols(out_norm_g, (GROUP_WIDTH,) * N_GROUPS)
    y = jnp.concatenate([head_norm(y_a, g_a), head_norm(y_b, g_b), head_norm(y_c, g_c),
                         head_norm(y_d, g_d) * (1.0 - lam_init)], axis=-1)
    x = x + y @ w_out
    x = x + peer_ffn(rms_norm(x, norm2_g), w_query, sub_keys, peer_u, peer_v)
    return x


def setup_inputs(seed: int = 0) -> dict:
    key = jax.random.key(seed)
    ks = jax.random.split(key, 24)
    nrm = lambda k, shape, scale: jax.random.normal(k, shape, jnp.float32) * scale
    gain = lambda k, shape: 1.0 + 0.02 * jax.random.normal(k, shape, jnp.float32)
    L = DEPTH
    return {
        "x": nrm(ks[0], (BATCH, SEQ, D_MODEL), 1.0),
        "norm1_g": gain(ks[1], (L, D_MODEL)),
        "w_in": nrm(ks[2], (L, D_MODEL, IN_PROJ_WIDTH), D_MODEL ** -0.5),
        "w_pool": nrm(ks[3], (L, len(POOL_WINDOWS), POOL_GROUP, POOL_GROUP), POOL_GROUP ** -0.5),
        "swa_q_norm": gain(ks[4], (L, HEAD_DIM)),
        "swa_k_norm": gain(ks[5], (L, HEAD_DIM)),
        "swa_sinks": nrm(ks[6], (L, SWA_Q_HEADS), 1.0),
        "conv_w": nrm(ks[7], (L, CONV_WIDTH, GROUP_WIDTH), CONV_WIDTH ** -0.5),
        "diff_q_norm": gain(ks[8], (L, DIFF_QK_DIM)),
        "diff_k_norm": gain(ks[9], (L, DIFF_QK_DIM)),
        "lam_q1": nrm(ks[10], (L, DIFF_QK_DIM), 0.1),
        "lam_k1": nrm(ks[11], (L, DIFF_QK_DIM), 0.1),
        "lam_q2": nrm(ks[12], (L, DIFF_QK_DIM), 0.1),
        "lam_k2": nrm(ks[13], (L, DIFF_QK_DIM), 0.1),
        "out_norm_g": gain(ks[14], (L, MIX_WIDTH)),
        "w_out": nrm(ks[15], (L, MIX_WIDTH, D_MODEL), MIX_WIDTH ** -0.5),
        "norm2_g": gain(ks[16], (L, D_MODEL)),
        "w_query": nrm(ks[17], (L, D_MODEL, PEER_HEADS * PEER_QUERY_DIM), D_MODEL ** -0.5),
        "sub_keys": nrm(ks[18], (L, 2, PEER_KEYS, PEER_HALF), PEER_HALF ** -0.5),
        "peer_u": nrm(ks[19], (L, PEER_EXPERTS, D_MODEL), D_MODEL ** -0.5),
        "peer_v": nrm(ks[20], (L, PEER_EXPERTS, D_MODEL), PEER_HEADS ** -0.5),
    }


def reference(x, norm1_g, w_in, w_pool, swa_q_norm, swa_k_norm, swa_sinks, conv_w, diff_q_norm,
              diff_k_norm, lam_q1, lam_k1, lam_q2, lam_k2, out_norm_g, w_out, norm2_g, w_query,
              sub_keys, peer_u, peer_v):
    for l in range(DEPTH):
        x = hybrid_layer(x, l, norm1_g[l], w_in[l], w_pool[l], swa_q_norm[l], swa_k_norm[l],
                         swa_sinks[l], conv_w[l], diff_q_norm[l], diff_k_norm[l], lam_q1[l],
                         lam_k1[l], lam_q2[l], lam_k2[l], out_norm_g[l], w_out[l], norm2_g[l],
                         w_query[l], sub_keys[l], peer_u[l], peer_v[l])
    return x
```

```python
import functools

import jax
import jax.numpy as jnp
import numpy as np
from jax import lax
from jax.experimental import pallas as pl
from jax.experimental.pallas import tpu as pltpu

F32 = jnp.float32
BF16 = jnp.bfloat16

D_MODEL = 1024
EPS = 1e-6
HEAD_DIM = 64
GROUP_WIDTH = 256
POOL_WINDOWS = (2, 4, 8, 16)
SWA_Q_HEADS = 4
SWA_GQA = 2
SWA_WINDOW = 128
DIFF_HEADS = 4
DIFF_QK_DIM = 32
IN_PROJ_WIDTH = 2304
PEER_HEADS = 8
PEER_KEYS = 128
PEER_EXPERTS = PEER_KEYS * PEER_KEYS
PEER_HALF = 128
PEER_TOPK = 16
ALIBI_SLOPES = tuple(2.0 ** (-(i + 1)) for i in range(SWA_Q_HEADS + DIFF_HEADS))

NEG = -1e30
HALO = 16
ROW_BLOCK = 128
DIFF_Q_BLOCK = 256
IN_PROJ_ROWS = 512
ROUTER_ROWS = 256
PEER_TOKENS = 512
PEER_EXPERT_CHUNK = 1024
VMEM_LIMIT = 56 * 1024 * 1024

NT_DIMS = (((1,), (1,)), ((), ()))


def _group_sumsq(x, bd):
    x2 = x * x
    hi = x2.astype(BF16)
    lo = (x2 - hi.astype(F32)).astype(BF16)
    return (jnp.dot(hi, bd, preferred_element_type=F32)
            + jnp.dot(lo, bd, preferred_element_type=F32))


def _group_rms_norm(x, bd, group, gain):
    return x * lax.rsqrt(_group_sumsq(x, bd) * (1.0 / group) + EPS) * gain


def _gelu(h):
    return 0.5 * h * (1.0 + lax.erf(h * np.float32(np.sqrt(0.5))))


def _in_proj_kernel(x_ref, g_ref, w_ref, z_ref):
    x = x_ref[...]
    xn = x * lax.rsqrt(jnp.mean(x * x, axis=-1, keepdims=True) + EPS) * g_ref[...]
    z = jnp.dot(xn.astype(BF16), w_ref[...], preferred_element_type=F32)
    z_ref[...] = z.astype(z_ref.dtype)


def _in_proj(x2d, g, w_bf16):
    t = x2d.shape[0]
    tm = min(IN_PROJ_ROWS, t)
    return pl.pallas_call(
        _in_proj_kernel,
        grid=(t // tm,),
        in_specs=[pl.BlockSpec((tm, D_MODEL), lambda i: (i, 0)),
                  pl.BlockSpec((1, D_MODEL), lambda i: (0, 0)),
                  pl.BlockSpec((D_MODEL, IN_PROJ_WIDTH), lambda i: (0, 0))],
        out_specs=pl.BlockSpec((tm, IN_PROJ_WIDTH), lambda i: (i, 0)),
        out_shape=jax.ShapeDtypeStruct((t, IN_PROJ_WIDTH), BF16),
        compiler_params=pltpu.CompilerParams(dimension_semantics=("parallel",),
                                             vmem_limit_bytes=VMEM_LIMIT),
        name="in_proj",
    )(x2d, g, w_bf16)


def _local_mix_kernel(sinks_ref, a_ref, a_halo_ref, bq_ref, bkv_ref, bkv_prev_ref, ch_ref, ch_halo_ref,
                      cb_ref, cc_ref, cc_halo_ref, dq_ref, dk_ref,
                      wpool_ref, bd64_ref, bd32_ref, gq_swa_ref, gk_swa_ref, conv_w_ref, g_out_ref,
                      gq_diff_ref, gk_diff_ref,
                      y_ref, qdn_ref, kdn_ref):
    n = pl.program_id(1)
    not_first = (n > 0).astype(F32)
    bd64 = bd64_ref[...]
    rows = lax.broadcasted_iota(jnp.int32, (ROW_BLOCK, GROUP_WIDTH), 0)
    cols = lax.broadcasted_iota(jnp.int32, (ROW_BLOCK, GROUP_WIDTH), 1)

    p = a_ref[...].astype(F32)
    p_ext = jnp.concatenate([a_halo_ref[...].astype(F32) * not_first, p], axis=0)
    s2 = p_ext + pltpu.roll(p_ext, 1, axis=0)
    s4 = s2 + pltpu.roll(s2, 2, axis=0)
    s8 = s4 + pltpu.roll(s4, 4, axis=0)
    s16 = s8 + pltpu.roll(s8, 8, axis=0)
    wsel = jnp.where(cols < 64, 2, jnp.where(cols < 128, 4, jnp.where(cols < 192, 8, 16)))
    ssel = jnp.where(cols < 64, s2[HALO:], jnp.where(cols < 128, s4[HALO:],
                                                      jnp.where(cols < 192, s8[HALO:], s16[HALO:])))
    cnt = jnp.minimum(n * ROW_BLOCK + rows + 1, wsel).astype(F32)
    d = ssel / cnt - p
    y_a = jnp.dot(d.astype(BF16), wpool_ref[...], preferred_element_type=F32)

    q = _group_rms_norm(bq_ref[...].astype(F32), bd64, HEAD_DIM, gq_swa_ref[...]) * (HEAD_DIM ** -0.5)
    qn = q.astype(BF16)
    bd64k = bd64[:128, :128]
    kv_cur = bkv_ref[...]
    kv_prev = bkv_prev_ref[...]
    k_cur = _group_rms_norm(kv_cur[:, :128].astype(F32), bd64k, HEAD_DIM, gk_swa_ref[...])
    k_prev = _group_rms_norm(kv_prev[:, :128].astype(F32), bd64k, HEAD_DIM, gk_swa_ref[...])
    kn = jnp.concatenate([k_prev, k_cur], axis=0).astype(BF16)
    v = jnp.concatenate([kv_prev[:, 128:], kv_cur[:, 128:]], axis=0)
    qi = lax.broadcasted_iota(jnp.int32, (ROW_BLOCK, 2 * ROW_BLOCK), 0)
    kj = lax.broadcasted_iota(jnp.int32, (ROW_BLOCK, 2 * ROW_BLOCK), 1)
    dist = qi + ROW_BLOCK - kj
    valid = (dist >= 0) & (dist < SWA_WINDOW) & ((n - 1) * ROW_BLOCK + kj >= 0)
    distf = dist.astype(F32)
    heads = []
    for h in range(SWA_Q_HEADS):
        hk = h // SWA_GQA
        sc = lax.dot_general(qn[:, h * 64:(h + 1) * 64], kn[:, hk * 64:(hk + 1) * 64], NT_DIMS,
                             preferred_element_type=F32)
        sc = jnp.where(valid, sc - ALIBI_SLOPES[h] * distf, NEG)
        sink = sinks_ref[h]
        mx = jnp.maximum(jnp.max(sc, axis=-1, keepdims=True), sink)
        e = jnp.exp(sc - mx)
        den = jnp.sum(e, axis=-1, keepdims=True) + jnp.exp(sink - mx)
        o = jnp.dot(e.astype(BF16), v[:, hk * 64:(hk + 1) * 64], preferred_element_type=F32)
        heads.append(o / den)
    y_b = jnp.concatenate(heads, axis=-1)

    zc = cc_ref[...].astype(F32) * ch_ref[...].astype(F32)
    zc_halo = cc_halo_ref[...].astype(F32) * ch_halo_ref[...].astype(F32) * not_first
    zc_ext = jnp.concatenate([zc_halo, zc], axis=0)
    cw = conv_w_ref[...]
    conv = (cw[0:1] * pltpu.roll(zc_ext, 2, axis=0)[HALO:] + cw[1:2] * pltpu.roll(zc_ext, 1, axis=0)[HALO:]
            + cw[2:3] * zc)
    y_c = cb_ref[...].astype(F32) * conv

    g_out = g_out_ref[...]
    for gi, y in enumerate((y_a, y_b, y_c)):
        lo, hi = gi * GROUP_WIDTH, (gi + 1) * GROUP_WIDTH
        y_ref[:, lo:hi] = _group_rms_norm(y, bd64, HEAD_DIM, g_out[:, lo:hi]).astype(y_ref.dtype)

    bd32 = bd32_ref[...]
    qd = _group_rms_norm(dq_ref[...].astype(F32), bd32, DIFF_QK_DIM, gq_diff_ref[...]) * (DIFF_QK_DIM ** -0.5)
    kd = _group_rms_norm(dk_ref[...].astype(F32), bd32, DIFF_QK_DIM, gk_diff_ref[...])
    qdn_ref[...] = qd.astype(qdn_ref.dtype)
    kdn_ref[...] = kd.astype(kdn_ref.dtype)


def _local_mix(z, batch, seq, sinks, wpool_bd, bd64, bd32, gq_swa, gk_swa, conv_w, g_out, gq_diff, gk_diff):
    t = batch * seq
    nb = seq // ROW_BLOCK
    halo_per_block = ROW_BLOCK // HALO

    def cur(col):
        return pl.BlockSpec((ROW_BLOCK, GROUP_WIDTH), lambda b, n: (b * nb + n, col))

    def prev(col):
        return pl.BlockSpec((ROW_BLOCK, GROUP_WIDTH), lambda b, n: (b * nb + jnp.maximum(n - 1, 0), col))

    def halo(col):
        return pl.BlockSpec(
            (HALO, GROUP_WIDTH),
            lambda b, n: (jnp.maximum((b * nb + n) * halo_per_block - 1, 0), col))

    def whole(arr):
        return pl.BlockSpec(arr.shape, lambda b, n: (0,) * arr.ndim)

    params = (wpool_bd, bd64, bd32, gq_swa, gk_swa, conv_w, g_out, gq_diff, gk_diff)
    in_specs = ([pl.BlockSpec(memory_space=pltpu.SMEM),
                 cur(0), halo(0), cur(1), cur(2), prev(2), cur(3), halo(3), cur(4), cur(5), halo(5),
                 cur(6), cur(7)] + [whole(a) for a in params])
    out_specs = [pl.BlockSpec((ROW_BLOCK, 3 * GROUP_WIDTH), lambda b, n: (b * nb + n, 0)),
                 pl.BlockSpec((ROW_BLOCK, GROUP_WIDTH), lambda b, n: (b * nb + n, 0)),
                 pl.BlockSpec((ROW_BLOCK, GROUP_WIDTH), lambda b, n: (b * nb + n, 0))]
    out_shape = [jax.ShapeDtypeStruct((t, 3 * GROUP_WIDTH), BF16),
                 jax.ShapeDtypeStruct((t, GROUP_WIDTH), BF16),
                 jax.ShapeDtypeStruct((t, GROUP_WIDTH), BF16)]
    return pl.pallas_call(
        _local_mix_kernel,
        grid=(batch, nb),
        in_specs=in_specs,
        out_specs=out_specs,
        out_shape=out_shape,
        compiler_params=pltpu.CompilerParams(dimension_semantics=("parallel", "parallel"),
                                             vmem_limit_bytes=VMEM_LIMIT),
        name="local_mix",
    )(sinks, *([z] * 12), *params)


def _diff_attn_kernel(q_ref, k_ref, v_ref, lamv_ref, bd64_ref, g_ref, y_ref, *, one_minus_lam_init, lam_init):
    qi = pl.program_id(1)
    qb = DIFF_Q_BLOCK
    lamv = lamv_ref[...]
    lam = (jnp.exp(jnp.sum(lamv[0:1] * lamv[1:2], axis=-1, keepdims=True))
           - jnp.exp(jnp.sum(lamv[2:3] * lamv[3:4], axis=-1, keepdims=True)) + lam_init)
    q = q_ref[...]
    lane_group = lax.broadcasted_iota(jnp.int32, (qb, GROUP_WIDTH), 1) // DIFF_QK_DIM
    rel = (lax.broadcasted_iota(jnp.int32, (qb, qb), 0) - lax.broadcasted_iota(jnp.int32, (qb, qb), 1))
    relf = rel.astype(F32)
    outs = []
    for h in range(DIFF_HEADS):
        slope = ALIBI_SLOPES[SWA_Q_HEADS + h]
        maps = []
        for m in range(2):
            qm = jnp.where(lane_group == 2 * h + m, q, jnp.zeros_like(q))

            def kv_step(c, carry, qm=qm, slope=slope, h=h):
                mx, den, acc = carry
                start = pl.multiple_of(c * qb, qb)
                k = k_ref[pl.ds(start, qb), :]
                sc = lax.dot_general(qm, k, NT_DIMS, preferred_element_type=F32)
                off = (qi - c) * qb
                sc = sc - slope * (relf + off.astype(F32))
                sc = jnp.where(rel + off >= 0, sc, NEG)
                mx_new = jnp.maximum(mx, jnp.max(sc, axis=-1, keepdims=True))
                alpha = jnp.exp(mx - mx_new)
                e = jnp.exp(sc - mx_new)
                den = alpha * den + jnp.sum(e, axis=-1, keepdims=True)
                vh = v_ref[pl.ds(start, qb), h * HEAD_DIM:(h + 1) * HEAD_DIM]
                acc = alpha * acc + jnp.dot(e.astype(BF16), vh, preferred_element_type=F32)
                return mx_new, den, acc

            init = (jnp.full((qb, 1), NEG, F32), jnp.zeros((qb, 1), F32), jnp.zeros((qb, HEAD_DIM), F32))
            _, den, acc = lax.fori_loop(0, qi + 1, kv_step, init)
            maps.append(acc / den)
        outs.append(maps[0] - lam * maps[1])
    o = jnp.concatenate(outs, axis=-1)
    y = _group_rms_norm(o, bd64_ref[...], HEAD_DIM, g_ref[...]) * one_minus_lam_init
    y_ref[...] = y.astype(y_ref.dtype)


def _diff_attn(qdn, kdn, z, batch, seq, lamv, bd64, g_d, lam_init):
    t = batch * seq
    nq = seq // DIFF_Q_BLOCK
    v_col = 8
    kernel = functools.partial(_diff_attn_kernel, one_minus_lam_init=1.0 - lam_init, lam_init=lam_init)
    return pl.pallas_call(
        kernel,
        grid=(batch, nq),
        in_specs=[pl.BlockSpec((DIFF_Q_BLOCK, GROUP_WIDTH), lambda b, i: (b * nq + i, 0)),
                  pl.BlockSpec((seq, GROUP_WIDTH), lambda b, i: (b, 0)),
                  pl.BlockSpec((seq, GROUP_WIDTH), lambda b, i: (b, v_col)),
                  pl.BlockSpec(lamv.shape, lambda b, i: (0, 0)),
                  pl.BlockSpec(bd64.shape, lambda b, i: (0, 0)),
                  pl.BlockSpec(g_d.shape, lambda b, i: (0, 0))],
        out_specs=pl.BlockSpec((DIFF_Q_BLOCK, GROUP_WIDTH), lambda b, i: (b * nq + i, 0)),
        out_shape=jax.ShapeDtypeStruct((t, GROUP_WIDTH), BF16),
        compiler_params=pltpu.CompilerParams(dimension_semantics=("parallel", "parallel"),
                                             vmem_limit_bytes=VMEM_LIMIT),
        name="diff_attn",
    )(qdn, kdn, z, lamv, bd64, g_d)


def _top_values(s, count):
    vals = []
    cur = s
    for r in range(count):
        m = jnp.max(cur, axis=0, keepdims=True)
        vals.append(m)
        if r + 1 < count:
            cur = jnp.where(cur >= m, NEG, cur)
    return vals


def _stack_rows(rows_list):
    n = len(rows_list)
    lanes = rows_list[0].shape[-1]
    ri = lax.broadcasted_iota(jnp.int32, (n, lanes), 0)
    out = jnp.zeros((n, lanes), F32)
    for r, row in enumerate(rows_list):
        out = jnp.where(ri == r, row, out)
    return out


def _out_router_kernel(yabc_ref, yd_ref, x_ref, wo1_ref, wo2_ref, g2_ref, wq_ref, keys_ref,
                       x1_ref, xn2_ref, s1_ref, s2_ref, w1_ref, w2_ref, tau_ref):
    acc = (jnp.dot(yabc_ref[...], wo1_ref[...], preferred_element_type=F32)
           + jnp.dot(yd_ref[...], wo2_ref[...], preferred_element_type=F32))
    x1 = x_ref[...] + acc
    x1_ref[...] = x1
    xn2 = (x1 * lax.rsqrt(jnp.mean(x1 * x1, axis=-1, keepdims=True) + EPS) * g2_ref[...]).astype(BF16)
    xn2_ref[...] = xn2
    q = jnp.dot(xn2, wq_ref[...], preferred_element_type=F32)
    tm = q.shape[0]
    row8 = lax.broadcasted_iota(jnp.int32, (8, tm), 0)
    for h in range(PEER_HEADS):
        scores = []
        tops = []
        for c in range(2):
            lo = (2 * h + c) * PEER_HALF
            qhc = q[:, lo:lo + PEER_HALF].astype(BF16)
            s = lax.dot_general(keys_ref[c], qhc, NT_DIMS, preferred_element_type=F32)
            scores.append(s)
            tops.append(_top_values(s, PEER_TOPK))
        t1, t2 = tops
        v1 = _stack_rows(t1)
        v2_hi = _stack_rows(t2[8:])
        groups = [v1 + t2[0]]
        for b in range(1, 8):
            groups.append(jnp.where(row8 < PEER_TOPK // (b + 1), v1[:8] + t2[b], NEG))
        groups.append(t1[0] + v2_hi)
        best = _top_values(jnp.concatenate(groups, axis=0), PEER_TOPK)
        tau = best[PEER_TOPK - 1]
        zsum = jnp.sum(jnp.exp(_stack_rows(best) - best[0]), axis=0, keepdims=True)
        s1_ref[h] = scores[0]
        s2_ref[h] = scores[1]
        w1_ref[h] = jnp.exp(scores[0] - t1[0]) / zsum
        w2_ref[h] = jnp.exp(scores[1] - t2[0])
        tau_ref[h:h + 1, :] = tau


def _out_router(yabc, yd, x2d, wo1, wo2, g2, wq, keys):
    t = x2d.shape[0]
    tm = min(ROUTER_ROWS, t)
    n_q = PEER_HEADS * 2 * PEER_HALF

    def whole(arr):
        return pl.BlockSpec(arr.shape, lambda i: (0,) * arr.ndim)

    score_spec = pl.BlockSpec((PEER_HEADS, PEER_KEYS, tm), lambda i: (0, 0, i))
    score_shape = jax.ShapeDtypeStruct((PEER_HEADS, PEER_KEYS, t), F32)
    return pl.pallas_call(
        _out_router_kernel,
        grid=(t // tm,),
        in_specs=[pl.BlockSpec((tm, 3 * GROUP_WIDTH), lambda i: (i, 0)),
                  pl.BlockSpec((tm, GROUP_WIDTH), lambda i: (i, 0)),
                  pl.BlockSpec((tm, D_MODEL), lambda i: (i, 0)),
                  whole(wo1), whole(wo2), whole(g2), whole(wq), whole(keys)],
        out_specs=[pl.BlockSpec((tm, D_MODEL), lambda i: (i, 0)),
                   pl.BlockSpec((tm, D_MODEL), lambda i: (i, 0)),
                   score_spec, score_spec, score_spec, score_spec,
                   pl.BlockSpec((PEER_HEADS, tm), lambda i: (0, i))],
        out_shape=[jax.ShapeDtypeStruct((t, D_MODEL), F32),
                   jax.ShapeDtypeStruct((t, D_MODEL), BF16),
                   score_shape, score_shape, score_shape, score_shape,
                   jax.ShapeDtypeStruct((PEER_HEADS, t), F32)],
        compiler_params=pltpu.CompilerParams(dimension_semantics=("parallel",),
                                             vmem_limit_bytes=VMEM_LIMIT),
        name="out_router",
    )(yabc, yd, x2d, wo1, wo2, g2, wq, keys)


def _peer_kernel(x_ref, xn_ref, u_ref, vt_ref, s1_ref, s2_ref, w1_ref, w2_ref, tau_ref, o_ref,
                 acc_ref, h_ref, a_ref, *, rows_per_step):
    j = pl.program_id(1)
    tb = xn_ref.shape[0]

    @pl.when(j == 0)
    def _():
        acc_ref[...] = jnp.zeros_like(acc_ref)

    h_ref[...] = lax.dot_general(u_ref[...], xn_ref[...], NT_DIMS, preferred_element_type=F32)

    first_row = pl.multiple_of(j * rows_per_step, rows_per_step)
    for lc in range(tb // 128):
        ls = slice(lc * 128, (lc + 1) * 128)
        s1_rows = [s1_ref[hd, pl.ds(first_row, rows_per_step), ls] for hd in range(PEER_HEADS)]
        w1_rows = [w1_ref[hd, pl.ds(first_row, rows_per_step), ls] for hd in range(PEER_HEADS)]
        for r in range(rows_per_step):
            w = jnp.zeros((PEER_KEYS, 128), F32)
            for hd in range(PEER_HEADS):
                selected = (s2_ref[hd, :, ls] + s1_rows[hd][r:r + 1]) >= tau_ref[hd:hd + 1, ls]
                w = w + jnp.where(selected, w2_ref[hd, :, ls] * w1_rows[hd][r:r + 1], 0.0)
            rs = slice(r * PEER_KEYS, (r + 1) * PEER_KEYS)
            a_ref[rs, ls] = (w * _gelu(h_ref[rs, ls])).astype(a_ref.dtype)
    acc_ref[...] += jnp.dot(vt_ref[...], a_ref[...], preferred_element_type=F32)

    @pl.when(j == pl.num_programs(1) - 1)
    def _():
        o_ref[...] = x_ref[...] + acc_ref[...].T


def _peer(x1, xn2, u_bf16, vt_bf16, s1, s2, w1, w2, tau):
    t = x1.shape[0]
    tb = min(PEER_TOKENS, t)
    ec = PEER_EXPERT_CHUNK
    rows_per_step = ec // PEER_KEYS
    assert rows_per_step % 8 == 0 and tb % 128 == 0 and t % tb == 0
    score_spec = pl.BlockSpec((PEER_HEADS, PEER_KEYS, tb), lambda i, j: (0, 0, i))
    kernel = functools.partial(_peer_kernel, rows_per_step=rows_per_step)
    return pl.pallas_call(
        kernel,
        grid=(t // tb, PEER_EXPERTS // ec),
        in_specs=[pl.BlockSpec((tb, D_MODEL), lambda i, j: (i, 0)),
                  pl.BlockSpec((tb, D_MODEL), lambda i, j: (i, 0)),
                  pl.BlockSpec((ec, D_MODEL), lambda i, j: (j, 0)),
                  pl.BlockSpec((D_MODEL, ec), lambda i, j: (0, j)),
                  score_spec, score_spec, score_spec, score_spec,
                  pl.BlockSpec((PEER_HEADS, tb), lambda i, j: (0, i))],
        out_specs=pl.BlockSpec((tb, D_MODEL), lambda i, j: (i, 0)),
        out_shape=jax.ShapeDtypeStruct((t, D_MODEL), F32),
        scratch_shapes=[pltpu.VMEM((D_MODEL, tb), F32),
                        pltpu.VMEM((ec, tb), F32),
                        pltpu.VMEM((ec, tb), BF16)],
        compiler_params=pltpu.CompilerParams(dimension_semantics=("parallel", "arbitrary"),
                                             vmem_limit_bytes=VMEM_LIMIT),
        name="peer",
    )(x1, xn2, u_bf16, vt_bf16, s1, s2, w1, w2, tau)


def _block_diag_ones(width, group):
    idx = np.arange(width) // group
    return jnp.asarray((idx[:, None] == idx[None, :]).astype(np.float32), BF16)


def _layer(x2d, batch, seq, layer_idx, norm1_g, w_in, w_pool, swa_q_norm, swa_k_norm, swa_sinks, conv_w,
           diff_q_norm, diff_k_norm, lam_q1, lam_k1, lam_q2, lam_k2, out_norm_g, w_out, norm2_g, w_query,
           sub_keys, peer_u, peer_v):
    lam_init = 0.8 - 0.6 * float(np.exp(-0.3 * layer_idx))
    bd64 = _block_diag_ones(GROUP_WIDTH, HEAD_DIM)
    bd32 = _block_diag_ones(GROUP_WIDTH, DIFF_QK_DIM)
    wpool_bd = jax.scipy.linalg.block_diag(*[w_pool[g] for g in range(len(POOL_WINDOWS))]).astype(BF16)

    z = _in_proj(x2d, norm1_g.reshape(1, D_MODEL), w_in.astype(BF16))
    yabc, qdn, kdn = _local_mix(
        z, batch, seq, swa_sinks.astype(F32), wpool_bd, bd64, bd32,
        jnp.tile(swa_q_norm, SWA_Q_HEADS).reshape(1, GROUP_WIDTH),
        jnp.tile(swa_k_norm, 2).reshape(1, 128),
        conv_w, out_norm_g[:3 * GROUP_WIDTH].reshape(1, 3 * GROUP_WIDTH),
        jnp.tile(diff_q_norm, 2 * DIFF_HEADS).reshape(1, GROUP_WIDTH),
        jnp.tile(diff_k_norm, 2 * DIFF_HEADS).reshape(1, GROUP_WIDTH))
    lamv = jnp.stack([lam_q1, lam_k1, lam_q2, lam_k2]).astype(F32)
    yd = _diff_attn(qdn, kdn, z, batch, seq, lamv, bd64,
                    out_norm_g[3 * GROUP_WIDTH:].reshape(1, GROUP_WIDTH), lam_init)
    w_out_bf16 = w_out.astype(BF16)
    x1, xn2, s1, s2, w1, w2, tau = _out_router(
        yabc, yd, x2d, w_out_bf16[:3 * GROUP_WIDTH], w_out_bf16[3 * GROUP_WIDTH:],
        norm2_g.reshape(1, D_MODEL), w_query.astype(BF16), sub_keys.astype(BF16))
    return _peer(x1, xn2, peer_u.astype(BF16), peer_v.T.astype(BF16), s1, s2, w1, w2, tau)


def kernel(x, norm1_g, w_in, w_pool, swa_q_norm, swa_k_norm, swa_sinks, conv_w, diff_q_norm, diff_k_norm,
           lam_q1, lam_k1, lam_q2, lam_k2, out_norm_g, w_out, norm2_g, w_query, sub_keys, peer_u, peer_v):
    batch, seq, d = x.shape
    x2d = x.reshape(batch * seq, d)
    depth = norm1_g.shape[0]
    for l in range(depth):
        x2d = _layer(x2d, batch, seq, l, norm1_g[l], w_in[l], w_pool[l], swa_q_norm[l], swa_k_norm[l],
                     swa_sinks[l], conv_w[l], diff_q_norm[l], diff_k_norm[l], lam_q1[l], lam_k1[l],
                     lam_q2[l], lam_k2[l], out_norm_g[l], w_out[l], norm2_g[l], w_query[l], sub_keys[l],
                     peer_u[l], peer_v[l])
    return x2d.reshape(batch, seq, d)
```

```python
import functools

import jax
import jax.numpy as jnp
import numpy as np
from jax import lax
from jax.experimental import pallas as pl
from jax.experimental.pallas import tpu as pltpu

F32 = jnp.float32
BF16 = jnp.bfloat16

D_MODEL = 1024
EPS = 1e-6
HEAD_DIM = 64
GROUP_WIDTH = 256
POOL_WINDOWS = (2, 4, 8, 16)
SWA_Q_HEADS = 4
SWA_GQA = 2
SWA_WINDOW = 128
DIFF_HEADS = 4
DIFF_QK_DIM = 32
IN_PROJ_WIDTH = 2304
PEER_HEADS = 8
PEER_KEYS = 128
PEER_EXPERTS = PEER_KEYS * PEER_KEYS
PEER_HALF = 128
PEER_TOPK = 16
ALIBI_SLOPES = tuple(2.0 ** (-(i + 1)) for i in range(SWA_Q_HEADS + DIFF_HEADS))

NEG = -1e30
HALO = 16
ROW_BLOCK = 128
DIFF_Q_BLOCK = 256
IN_PROJ_ROWS = 512
ROUTER_ROWS = 256
PEER_TOKENS = 512
PEER_EXPERT_CHUNK = 1024
VMEM_LIMIT = 56 * 1024 * 1024

NT_DIMS = (((1,), (1,)), ((), ()))


def _group_sumsq(x, bd):
    x2 = x * x
    hi = x2.astype(BF16)
    lo = (x2 - hi.astype(F32)).astype(BF16)
    return (jnp.dot(hi, bd, preferred_element_type=F32)
            + jnp.dot(lo, bd, preferred_element_type=F32))


def _group_rms_norm(x, bd, group, gain):
    return x * lax.rsqrt(_group_sumsq(x, bd) * (1.0 / group) + EPS) * gain


def _gelu(h):
    return 0.5 * h * (1.0 + lax.erf(h * np.float32(np.sqrt(0.5))))


def _in_proj_kernel(x_ref, g_ref, w_ref, z_ref):
    x = x_ref[...]
    xn = x * lax.rsqrt(jnp.mean(x * x, axis=-1, keepdims=True) + EPS) * g_ref[...]
    z = jnp.dot(xn.astype(BF16), w_ref[...], preferred_element_type=F32)
    z_ref[...] = z.astype(z_ref.dtype)


def _in_proj(x2d, g, w_bf16):
    t = x2d.shape[0]
    tm = min(IN_PROJ_ROWS, t)
    return pl.pallas_call(
        _in_proj_kernel,
        grid=(t // tm,),
        in_specs=[pl.BlockSpec((tm, D_MODEL), lambda i: (i, 0)),
                  pl.BlockSpec((1, D_MODEL), lambda i: (0, 0)),
                  pl.BlockSpec((D_MODEL, IN_PROJ_WIDTH), lambda i: (0, 0))],
        out_specs=pl.BlockSpec((tm, IN_PROJ_WIDTH), lambda i: (i, 0)),
        out_shape=jax.ShapeDtypeStruct((t, IN_PROJ_WIDTH), BF16),
        compiler_params=pltpu.CompilerParams(dimension_semantics=("parallel",),
                                             vmem_limit_bytes=VMEM_LIMIT),
        name="in_proj",
    )(x2d, g, w_bf16)


def _local_mix_kernel(sinks_ref, a_ref, a_halo_ref, bq_ref, bkv_ref, bkv_prev_ref, ch_ref, ch_halo_ref,
                      cb_ref, cc_ref, cc_halo_ref, dq_ref, dk_ref,
                      wpool_ref, bd64_ref, bd32_ref, gq_swa_ref, gk_swa_ref, conv_w_ref, g_out_ref,
                      gq_diff_ref, gk_diff_ref,
                      y_ref, qdn_ref, kdn_ref):
    n = pl.program_id(1)
    not_first = (n > 0).astype(F32)
    bd64 = bd64_ref[...]
    rows = lax.broadcasted_iota(jnp.int32, (ROW_BLOCK, GROUP_WIDTH), 0)
    cols = lax.broadcasted_iota(jnp.int32, (ROW_BLOCK, GROUP_WIDTH), 1)

    p = a_ref[...].astype(F32)
    p_ext = jnp.concatenate([a_halo_ref[...].astype(F32) * not_first, p], axis=0)
    s2 = p_ext + pltpu.roll(p_ext, 1, axis=0)
    s4 = s2 + pltpu.roll(s2, 2, axis=0)
    s8 = s4 + pltpu.roll(s4, 4, axis=0)
    s16 = s8 + pltpu.roll(s8, 8, axis=0)
    wsel = jnp.where(cols < 64, 2, jnp.where(cols < 128, 4, jnp.where(cols < 192, 8, 16)))
    ssel = jnp.where(cols < 64, s2[HALO:], jnp.where(cols < 128, s4[HALO:],
                                                      jnp.where(cols < 192, s8[HALO:], s16[HALO:])))
    cnt = jnp.minimum(n * ROW_BLOCK + rows + 1, wsel).astype(F32)
    d = ssel / cnt - p
    y_a = jnp.dot(d.astype(BF16), wpool_ref[...], preferred_element_type=F32)

    q = _group_rms_norm(bq_ref[...].astype(F32), bd64, HEAD_DIM, gq_swa_ref[...]) * (HEAD_DIM ** -0.5)
    qn = q.astype(BF16)
    bd64k = bd64[:128, :128]
    kv_cur = bkv_ref[...]
    kv_prev = bkv_prev_ref[...]
    k_cur = _group_rms_norm(kv_cur[:, :128].astype(F32), bd64k, HEAD_DIM, gk_swa_ref[...])
    k_prev = _group_rms_norm(kv_prev[:, :128].astype(F32), bd64k, HEAD_DIM, gk_swa_ref[...])
    kn = jnp.concatenate([k_prev, k_cur], axis=0).astype(BF16)
    v = jnp.concatenate([kv_prev[:, 128:], kv_cur[:, 128:]], axis=0)
    qi = lax.broadcasted_iota(jnp.int32, (ROW_BLOCK, 2 * ROW_BLOCK), 0)
    kj = lax.broadcasted_iota(jnp.int32, (ROW_BLOCK, 2 * ROW_BLOCK), 1)
    dist = qi + ROW_BLOCK - kj
    valid = (dist >= 0) & (dist < SWA_WINDOW) & ((n - 1) * ROW_BLOCK + kj >= 0)
    distf = dist.astype(F32)
    heads = []
    for h in range(SWA_Q_HEADS):
        hk = h // SWA_GQA
        sc = lax.dot_general(qn[:, h * 64:(h + 1) * 64], kn[:, hk * 64:(hk + 1) * 64], NT_DIMS,
                             preferred_element_type=F32)
        sc = jnp.where(valid, sc - ALIBI_SLOPES[h] * distf, NEG)
        sink = sinks_ref[h]
        mx = jnp.maximum(jnp.max(sc, axis=-1, keepdims=True), sink)
        e = jnp.exp(sc - mx)
        den = jnp.sum(e, axis=-1, keepdims=True) + jnp.exp(sink - mx)
        o = jnp.dot(e.astype(BF16), v[:, hk * 64:(hk + 1) * 64], preferred_element_type=F32)
        heads.append(o / den)
    y_b = jnp.concatenate(heads, axis=-1)

    zc = cc_ref[...].astype(F32) * ch_ref[...].astype(F32)
    zc_halo = cc_halo_ref[...].astype(F32) * ch_halo_ref[...].astype(F32) * not_first
    zc_ext = jnp.concatenate([zc_halo, zc], axis=0)
    cw = conv_w_ref[...]
    conv = (cw[0:1] * pltpu.roll(zc_ext, 2, axis=0)[HALO:] + cw[1:2] * pltpu.roll(zc_ext, 1, axis=0)[HALO:]
            + cw[2:3] * zc)
    y_c = cb_ref[...].astype(F32) * conv

    g_out = g_out_ref[...]
    for gi, y in enumerate((y_a, y_b, y_c)):
        lo, hi = gi * GROUP_WIDTH, (gi + 1) * GROUP_WIDTH
        y_ref[:, lo:hi] = _group_rms_norm(y, bd64, HEAD_DIM, g_out[:, lo:hi]).astype(y_ref.dtype)

    bd32 = bd32_ref[...]
    qd = _group_rms_norm(dq_ref[...].astype(F32), bd32, DIFF_QK_DIM, gq_diff_ref[...]) * (DIFF_QK_DIM ** -0.5)
    kd = _group_rms_norm(dk_ref[...].astype(F32), bd32, DIFF_QK_DIM, gk_diff_ref[...])
    qdn_ref[...] = qd.astype(qdn_ref.dtype)
    kdn_ref[...] = kd.astype(kdn_ref.dtype)


def _local_mix(z, batch, seq, sinks, wpool_bd, bd64, bd32, gq_swa, gk_swa, conv_w, g_out, gq_diff, gk_diff):
    t = batch * seq
    nb = seq // ROW_BLOCK
    halo_per_block = ROW_BLOCK // HALO

    def cur(col):
        return pl.BlockSpec((ROW_BLOCK, GROUP_WIDTH), lambda b, n: (b * nb + n, col))

    def prev(col):
        return pl.BlockSpec((ROW_BLOCK, GROUP_WIDTH), lambda b, n: (b * nb + jnp.maximum(n - 1, 0), col))

    def halo(col):
        return pl.BlockSpec(
            (HALO, GROUP_WIDTH),
            lambda b, n: (jnp.maximum((b * nb + n) * halo_per_block - 1, 0), col))

    def whole(arr):
        return pl.BlockSpec(arr.shape, lambda b, n: (0,) * arr.ndim)

    params = (wpool_bd, bd64, bd32, gq_swa, gk_swa, conv_w, g_out, gq_diff, gk_diff)
    in_specs = ([pl.BlockSpec(memory_space=pltpu.SMEM),
                 cur(0), halo(0), cur(1), cur(2), prev(2), cur(3), halo(3), cur(4), cur(5), halo(5),
                 cur(6), cur(7)] + [whole(a) for a in params])
    out_specs = [pl.BlockSpec((ROW_BLOCK, 3 * GROUP_WIDTH), lambda b, n: (b * nb + n, 0)),
                 pl.BlockSpec((ROW_BLOCK, GROUP_WIDTH), lambda b, n: (b * nb + n, 0)),
                 pl.BlockSpec((ROW_BLOCK, GROUP_WIDTH), lambda b, n: (b * nb + n, 0))]
    out_shape = [jax.ShapeDtypeStruct((t, 3 * GROUP_WIDTH), BF16),
                 jax.ShapeDtypeStruct((t, GROUP_WIDTH), BF16),
                 jax.ShapeDtypeStruct((t, GROUP_WIDTH), BF16)]
    return pl.pallas_call(
        _local_mix_kernel,
        grid=(batch, nb),
        in_specs=in_specs,
        out_specs=out_specs,
        out_shape=out_shape,
        compiler_params=pltpu.CompilerParams(dimension_semantics=("parallel", "parallel"),
                                             vmem_limit_bytes=VMEM_LIMIT),
        name="local_mix",
    )(sinks, *([z] * 12), *params)


def _diff_attn_kernel(q_ref, k_ref, v_ref, lamv_ref, bd64_ref, g_ref, y_ref,
                      qs_ref, p_ref, m_ref, den_ref, acc_ref, *, one_minus_lam_init, lam_init):
    qi = pl.program_id(1)
    qb = DIFF_Q_BLOCK
    lamv = lamv_ref[...]
    lam = (jnp.exp(jnp.sum(lamv[0:1] * lamv[1:2], axis=-1, keepdims=True))
           - jnp.exp(jnp.sum(lamv[2:3] * lamv[3:4], axis=-1, keepdims=True)) + lam_init)
    n_maps = 2 * DIFF_HEADS
    q = q_ref[...]
    lane_group = lax.broadcasted_iota(jnp.int32, (qb, GROUP_WIDTH), 1) // DIFF_QK_DIM
    for g in range(n_maps):
        qs_ref[g * qb:(g + 1) * qb, :] = jnp.where(lane_group == g, q, jnp.zeros_like(q))
    m_ref[...] = jnp.full(m_ref.shape, NEG, F32)
    den_ref[...] = jnp.zeros(den_ref.shape, F32)
    acc_ref[...] = jnp.zeros(acc_ref.shape, F32)
    rel = (lax.broadcasted_iota(jnp.int32, (qb, qb), 0) - lax.broadcasted_iota(jnp.int32, (qb, qb), 1))
    relf = rel.astype(F32)

    def kv_step(c, diagonal):
        start = pl.multiple_of(c * qb, qb)
        sc_all = lax.dot_general(qs_ref[...], k_ref[pl.ds(start, qb), :], NT_DIMS, preferred_element_type=F32)
        off = ((qi - c) * qb).astype(F32)
        alphas = []
        for g in range(n_maps):
            rows = slice(g * qb, (g + 1) * qb)
            slope = ALIBI_SLOPES[SWA_Q_HEADS + g // 2]
            sc = sc_all[rows] - slope * relf
            if diagonal:
                sc = jnp.where(rel >= 0, sc, NEG)
            shift = slope * off
            m_old = m_ref[rows]
            m_new = jnp.maximum(m_old, jnp.max(sc, axis=-1, keepdims=True) - shift)
            alphas.append(jnp.exp(m_old - m_new))
            e = jnp.exp(sc - (m_new + shift))
            den_ref[rows] = alphas[g] * den_ref[rows] + jnp.sum(e, axis=-1, keepdims=True)
            m_ref[rows] = m_new
            p_ref[rows, :] = e.astype(BF16)
        pv = jnp.dot(p_ref[...], v_ref[pl.ds(start, qb), :], preferred_element_type=F32)
        for g in range(n_maps):
            rows = slice(g * qb, (g + 1) * qb)
            acc_ref[rows, :] = alphas[g] * acc_ref[rows, :] + pv[rows]

    def off_diagonal(c, carry):
        kv_step(c, False)
        return carry

    lax.fori_loop(0, qi, off_diagonal, 0)
    kv_step(qi, True)

    lane_head = lax.broadcasted_iota(jnp.int32, (qb, GROUP_WIDTH), 1) // HEAD_DIM
    o = jnp.zeros((qb, GROUP_WIDTH), F32)
    for h in range(DIFF_HEADS):
        r1 = slice(2 * h * qb, (2 * h + 1) * qb)
        r2 = slice((2 * h + 1) * qb, (2 * h + 2) * qb)
        o_h = acc_ref[r1, :] / den_ref[r1] - lam * (acc_ref[r2, :] / den_ref[r2])
        o = jnp.where(lane_head == h, o_h, o)
    y = _group_rms_norm(o, bd64_ref[...], HEAD_DIM, g_ref[...]) * one_minus_lam_init
    y_ref[...] = y.astype(y_ref.dtype)


def _diff_attn(qdn, kdn, z, batch, seq, lamv, bd64, g_d, lam_init):
    t = batch * seq
    nq = seq // DIFF_Q_BLOCK
    v_col = 8
    stacked = 2 * DIFF_HEADS * DIFF_Q_BLOCK
    kernel = functools.partial(_diff_attn_kernel, one_minus_lam_init=1.0 - lam_init, lam_init=lam_init)
    return pl.pallas_call(
        kernel,
        grid=(batch, nq),
        in_specs=[pl.BlockSpec((DIFF_Q_BLOCK, GROUP_WIDTH), lambda b, i: (b * nq + i, 0)),
                  pl.BlockSpec((seq, GROUP_WIDTH), lambda b, i: (b, 0)),
                  pl.BlockSpec((seq, GROUP_WIDTH), lambda b, i: (b, v_col)),
                  pl.BlockSpec(lamv.shape, lambda b, i: (0, 0)),
                  pl.BlockSpec(bd64.shape, lambda b, i: (0, 0)),
                  pl.BlockSpec(g_d.shape, lambda b, i: (0, 0))],
        out_specs=pl.BlockSpec((DIFF_Q_BLOCK, GROUP_WIDTH), lambda b, i: (b * nq + i, 0)),
        out_shape=jax.ShapeDtypeStruct((t, GROUP_WIDTH), BF16),
        scratch_shapes=[pltpu.VMEM((stacked, GROUP_WIDTH), BF16),
                        pltpu.VMEM((stacked, DIFF_Q_BLOCK), BF16),
                        pltpu.VMEM((stacked, 1), F32),
                        pltpu.VMEM((stacked, 1), F32),
                        pltpu.VMEM((stacked, GROUP_WIDTH), F32)],
        compiler_params=pltpu.CompilerParams(dimension_semantics=("parallel", "parallel"),
                                             vmem_limit_bytes=VMEM_LIMIT),
        name="diff_attn",
    )(qdn, kdn, z, lamv, bd64, g_d)


def _top_values(s, count, with_rank=False):
    vals = []
    cur = s
    rank = jnp.full(s.shape, float(count), F32) if with_rank else None
    for r in range(count):
        m = jnp.max(cur, axis=0, keepdims=True)
        vals.append(m)
        hit = cur >= m
        if with_rank:
            rank = jnp.where(hit, float(r), rank)
        if r + 1 < count:
            cur = jnp.where(hit, NEG, cur)
    return (vals, rank) if with_rank else vals


def _stack_rows(rows_list):
    n = len(rows_list)
    lanes = rows_list[0].shape[-1]
    ri = lax.broadcasted_iota(jnp.int32, (n, lanes), 0)
    out = jnp.zeros((n, lanes), F32)
    for r, row in enumerate(rows_list):
        out = jnp.where(ri == r, row, out)
    return out


def _out_router_kernel(yabc_ref, yd_ref, x_ref, wo1_ref, wo2_ref, g2_ref, wq_ref, keys_ref,
                       x1_ref, xn2_ref, n1_ref, w1_ref, r2_ref, w2_ref):
    acc = (jnp.dot(yabc_ref[...], wo1_ref[...], preferred_element_type=F32)
           + jnp.dot(yd_ref[...], wo2_ref[...], preferred_element_type=F32))
    x1 = x_ref[...] + acc
    x1_ref[...] = x1
    xn2 = (x1 * lax.rsqrt(jnp.mean(x1 * x1, axis=-1, keepdims=True) + EPS) * g2_ref[...]).astype(BF16)
    xn2_ref[...] = xn2
    q = jnp.dot(xn2, wq_ref[...], preferred_element_type=F32)
    tm = q.shape[0]
    row8 = lax.broadcasted_iota(jnp.int32, (8, tm), 0)
    for h in range(PEER_HEADS):
        scores = []
        for c in range(2):
            lo = (2 * h + c) * PEER_HALF
            qhc = q[:, lo:lo + PEER_HALF].astype(BF16)
            scores.append(lax.dot_general(keys_ref[c], qhc, NT_DIMS, preferred_element_type=F32))
        s1, s2 = scores
        t1 = _top_values(s1, PEER_TOPK)
        t2, rank2 = _top_values(s2, PEER_TOPK, with_rank=True)
        v1 = _stack_rows(t1)
        v2_hi = _stack_rows(t2[8:])
        groups = [v1 + t2[0]]
        for b in range(1, 8):
            groups.append(jnp.where(row8 < PEER_TOPK // (b + 1), v1[:8] + t2[b], NEG))
        groups.append(t1[0] + v2_hi)
        best = _top_values(jnp.concatenate(groups, axis=0), PEER_TOPK)
        tau = best[PEER_TOPK - 1]
        zsum = jnp.sum(jnp.exp(_stack_rows(best) - best[0]), axis=0, keepdims=True)
        n1 = jnp.zeros(s1.shape, F32)
        for b in range(PEER_TOPK):
            n1 = n1 + jnp.where(s1 + t2[b] >= tau, 1.0, 0.0)
        n1_ref[h] = n1
        w1_ref[h] = jnp.exp(s1 - t1[0]) / zsum
        r2_ref[h] = pltpu.bitcast(rank2.astype(BF16), jnp.uint32)
        w2_ref[h] = pltpu.bitcast(jnp.exp(s2 - t2[0]).astype(BF16), jnp.uint32)


def _out_router(yabc, yd, x2d, wo1, wo2, g2, wq, keys):
    t = x2d.shape[0]
    tm = min(ROUTER_ROWS, t)

    def whole(arr):
        return pl.BlockSpec(arr.shape, lambda i: (0,) * arr.ndim)

    row_spec = pl.BlockSpec((PEER_HEADS, PEER_KEYS, tm), lambda i: (0, 0, i))
    row_shape = jax.ShapeDtypeStruct((PEER_HEADS, PEER_KEYS, t), F32)
    tile_spec = pl.BlockSpec((PEER_HEADS, PEER_KEYS // 2, tm), lambda i: (0, 0, i))
    tile_shape = jax.ShapeDtypeStruct((PEER_HEADS, PEER_KEYS // 2, t), jnp.uint32)
    return pl.pallas_call(
        _out_router_kernel,
        grid=(t // tm,),
        in_specs=[pl.BlockSpec((tm, 3 * GROUP_WIDTH), lambda i: (i, 0)),
                  pl.BlockSpec((tm, GROUP_WIDTH), lambda i: (i, 0)),
                  pl.BlockSpec((tm, D_MODEL), lambda i: (i, 0)),
                  whole(wo1), whole(wo2), whole(g2), whole(wq), whole(keys)],
        out_specs=[pl.BlockSpec((tm, D_MODEL), lambda i: (i, 0)),
                   pl.BlockSpec((tm, D_MODEL), lambda i: (i, 0)),
                   row_spec, row_spec, tile_spec, tile_spec],
        out_shape=[jax.ShapeDtypeStruct((t, D_MODEL), F32),
                   jax.ShapeDtypeStruct((t, D_MODEL), BF16),
                   row_shape, row_shape, tile_shape, tile_shape],
        compiler_params=pltpu.CompilerParams(dimension_semantics=("parallel",),
                                             vmem_limit_bytes=VMEM_LIMIT),
        name="out_router",
    )(yabc, yd, x2d, wo1, wo2, g2, wq, keys)


def _peer_kernel(x_ref, xn_ref, u_ref, vt_ref, n1_ref, w1_ref, r2_ref, w2_ref, o_ref,
                 acc_ref, h_ref, a_ref, *, rows_per_step):
    j = pl.program_id(1)
    tb = xn_ref.shape[0]

    @pl.when(j == 0)
    def _():
        acc_ref[...] = jnp.zeros_like(acc_ref)

    h_ref[...] = lax.dot_general(u_ref[...], xn_ref[...], NT_DIMS, preferred_element_type=F32)

    first_row = pl.multiple_of(j * rows_per_step, rows_per_step)
    zero = jnp.zeros((), BF16)
    half = PEER_KEYS // 2
    for lc in range(tb // 128):
        ls = slice(lc * 128, (lc + 1) * 128)
        n1_rows = [n1_ref[hd, pl.ds(first_row, rows_per_step), ls] for hd in range(PEER_HEADS)]
        w1_rows = [w1_ref[hd, pl.ds(first_row, rows_per_step), ls] for hd in range(PEER_HEADS)]
        for r in range(rows_per_step):
            w = None
            for hd in range(PEER_HEADS):
                n1b = jnp.broadcast_to(n1_rows[hd][r:r + 1], (PEER_KEYS, 128)).astype(BF16)
                w1b = jnp.broadcast_to(w1_rows[hd][r:r + 1], (PEER_KEYS, 128)).astype(BF16)
                rank2 = pltpu.bitcast(r2_ref[hd, :, ls], BF16)
                w2 = pltpu.bitcast(w2_ref[hd, :, ls], BF16)
                term = jnp.where(rank2 < n1b, w2 * w1b, zero)
                w = term if w is None else w + term
            g = _gelu(h_ref[r * PEER_KEYS:(r + 1) * PEER_KEYS, ls]).astype(BF16)
            a_ref[r * half:(r + 1) * half, ls] = pltpu.bitcast(w * g, jnp.uint32)
    acc_ref[...] += jnp.dot(vt_ref[...], pltpu.bitcast(a_ref[...], BF16), preferred_element_type=F32)

    @pl.when(j == pl.num_programs(1) - 1)
    def _():
        o_ref[...] = x_ref[...] + acc_ref[...].T


def _peer(x1, xn2, u_bf16, vt_bf16, n1, w1, r2, w2):
    t = x1.shape[0]
    tb = min(PEER_TOKENS, t)
    ec = PEER_EXPERT_CHUNK
    rows_per_step = ec // PEER_KEYS
    assert rows_per_step % 8 == 0 and tb % 128 == 0 and t % tb == 0
    row_spec = pl.BlockSpec((PEER_HEADS, PEER_KEYS, tb), lambda i, j: (0, 0, i))
    tile_spec = pl.BlockSpec((PEER_HEADS, PEER_KEYS // 2, tb), lambda i, j: (0, 0, i))
    kernel = functools.partial(_peer_kernel, rows_per_step=rows_per_step)
    return pl.pallas_call(
        kernel,
        grid=(t // tb, PEER_EXPERTS // ec),
        in_specs=[pl.BlockSpec((tb, D_MODEL), lambda i, j: (i, 0)),
                  pl.BlockSpec((tb, D_MODEL), lambda i, j: (i, 0)),
                  pl.BlockSpec((ec, D_MODEL), lambda i, j: (j, 0)),
                  pl.BlockSpec((D_MODEL, ec), lambda i, j: (0, j)),
                  row_spec, row_spec, tile_spec, tile_spec],
        out_specs=pl.BlockSpec((tb, D_MODEL), lambda i, j: (i, 0)),
        out_shape=jax.ShapeDtypeStruct((t, D_MODEL), F32),
        scratch_shapes=[pltpu.VMEM((D_MODEL, tb), F32),
                        pltpu.VMEM((ec, tb), F32),
                        pltpu.VMEM((ec // 2, tb), jnp.uint32)],
        compiler_params=pltpu.CompilerParams(dimension_semantics=("parallel", "arbitrary"),
                                             vmem_limit_bytes=VMEM_LIMIT),
        name="peer",
    )(x1, xn2, u_bf16, vt_bf16, n1, w1, r2, w2)


def _block_diag_ones(width, group):
    idx = np.arange(width) // group
    return jnp.asarray((idx[:, None] == idx[None, :]).astype(np.float32), BF16)


def _layer(x2d, batch, seq, layer_idx, norm1_g, w_in, w_pool, swa_q_norm, swa_k_norm, swa_sinks, conv_w,
           diff_q_norm, diff_k_norm, lam_q1, lam_k1, lam_q2, lam_k2, out_norm_g, w_out, norm2_g, w_query,
           sub_keys, peer_u, peer_v):
    lam_init = 0.8 - 0.6 * float(np.exp(-0.3 * layer_idx))
    bd64 = _block_diag_ones(GROUP_WIDTH, HEAD_DIM)
    bd32 = _block_diag_ones(GROUP_WIDTH, DIFF_QK_DIM)
    wpool_bd = jax.scipy.linalg.block_diag(*[w_pool[g] for g in range(len(POOL_WINDOWS))]).astype(BF16)

    z = _in_proj(x2d, norm1_g.reshape(1, D_MODEL), w_in.astype(BF16))
    yabc, qdn, kdn = _local_mix(
        z, batch, seq, swa_sinks.astype(F32), wpool_bd, bd64, bd32,
        jnp.tile(swa_q_norm, SWA_Q_HEADS).reshape(1, GROUP_WIDTH),
        jnp.tile(swa_k_norm, 2).reshape(1, 128),
        conv_w, out_norm_g[:3 * GROUP_WIDTH].reshape(1, 3 * GROUP_WIDTH),
        jnp.tile(diff_q_norm, 2 * DIFF_HEADS).reshape(1, GROUP_WIDTH),
        jnp.tile(diff_k_norm, 2 * DIFF_HEADS).reshape(1, GROUP_WIDTH))
    lamv = jnp.stack([lam_q1, lam_k1, lam_q2, lam_k2]).astype(F32)
    yd = _diff_attn(qdn, kdn, z, batch, seq, lamv, bd64,
                    out_norm_g[3 * GROUP_WIDTH:].reshape(1, GROUP_WIDTH), lam_init)
    w_out_bf16 = w_out.astype(BF16)
    x1, xn2, n1, w1, r2, w2 = _out_router(
        yabc, yd, x2d, w_out_bf16[:3 * GROUP_WIDTH], w_out_bf16[3 * GROUP_WIDTH:],
        norm2_g.reshape(1, D_MODEL), w_query.astype(BF16), sub_keys.astype(BF16))
    return _peer(x1, xn2, peer_u.astype(BF16), peer_v.T.astype(BF16), n1, w1, r2, w2)


def kernel(x, norm1_g, w_in, w_pool, swa_q_norm, swa_k_norm, swa_sinks, conv_w, diff_q_norm, diff_k_norm,
           lam_q1, lam_k1, lam_q2, lam_k2, out_norm_g, w_out, norm2_g, w_query, sub_keys, peer_u, peer_v):
    batch, seq, d = x.shape
    x2d = x.reshape(batch * seq, d)
    depth = norm1_g.shape[0]
    for l in range(depth):
        x2d = _layer(x2d, batch, seq, l, norm1_g[l], w_in[l], w_pool[l], swa_q_norm[l], swa_k_norm[l],
                     swa_sinks[l], conv_w[l], diff_q_norm[l], diff_k_norm[l], lam_q1[l], lam_k1[l],
                     lam_q2[l], lam_k2[l], out_norm_g[l], w_out[l], norm2_g[l], w_query[l], sub_keys[l],
                     peer_u[l], peer_v[l])
    return x2d.reshape(batch, seq, d)
```

```python
import functools

import jax
import jax.numpy as jnp
import numpy as np
from jax import lax
from jax.experimental import pallas as pl
from jax.experimental.pallas import tpu as pltpu

F32 = jnp.float32
BF16 = jnp.bfloat16

D_MODEL = 1024
EPS = 1e-6
HEAD_DIM = 64
GROUP_WIDTH = 256
POOL_WINDOWS = (2, 4, 8, 16)
SWA_Q_HEADS = 4
SWA_GQA = 2
SWA_WINDOW = 128
DIFF_HEADS = 4
DIFF_QK_DIM = 32
IN_PROJ_WIDTH = 2304
PEER_HEADS = 8
PEER_KEYS = 128
PEER_EXPERTS = PEER_KEYS * PEER_KEYS
PEER_HALF = 128
PEER_TOPK = 16
ALIBI_SLOPES = tuple(2.0 ** (-(i + 1)) for i in range(SWA_Q_HEADS + DIFF_HEADS))

NEG = -1e30
HALO = 16
ROW_BLOCK = 128
DIFF_Q_BLOCK = 256
IN_PROJ_ROWS = 512
ROUTER_ROWS = 256
PEER_TOKENS = 512
PEER_EXPERT_CHUNK = 1024
VMEM_LIMIT = 56 * 1024 * 1024

NT_DIMS = (((1,), (1,)), ((), ()))


def _group_sumsq(x, bd):
    x2 = x * x
    hi = x2.astype(BF16)
    lo = (x2 - hi.astype(F32)).astype(BF16)
    return (jnp.dot(hi, bd, preferred_element_type=F32)
            + jnp.dot(lo, bd, preferred_element_type=F32))


def _group_rms_norm(x, bd, group, gain):
    return x * lax.rsqrt(_group_sumsq(x, bd) * (1.0 / group) + EPS) * gain


def _gelu(h):
    return 0.5 * h * (1.0 + lax.erf(h * np.float32(np.sqrt(0.5))))


def _in_proj_kernel(x_ref, g_ref, w_ref, z_ref):
    x = x_ref[...]
    xn = x * lax.rsqrt(jnp.mean(x * x, axis=-1, keepdims=True) + EPS) * g_ref[...]
    z = jnp.dot(xn.astype(BF16), w_ref[...], preferred_element_type=F32)
    z_ref[...] = z.astype(z_ref.dtype)


def _in_proj(x2d, g, w_bf16):
    t = x2d.shape[0]
    tm = min(IN_PROJ_ROWS, t)
    return pl.pallas_call(
        _in_proj_kernel,
        grid=(t // tm,),
        in_specs=[pl.BlockSpec((tm, D_MODEL), lambda i: (i, 0)),
                  pl.BlockSpec((1, D_MODEL), lambda i: (0, 0)),
                  pl.BlockSpec((D_MODEL, IN_PROJ_WIDTH), lambda i: (0, 0))],
        out_specs=pl.BlockSpec((tm, IN_PROJ_WIDTH), lambda i: (i, 0)),
        out_shape=jax.ShapeDtypeStruct((t, IN_PROJ_WIDTH), BF16),
        compiler_params=pltpu.CompilerParams(dimension_semantics=("parallel",),
                                             vmem_limit_bytes=VMEM_LIMIT),
        name="in_proj",
    )(x2d, g, w_bf16)


def _local_mix_kernel(sinks_ref, a_ref, a_halo_ref, bq_ref, bkv_ref, bkv_prev_ref, ch_ref, ch_halo_ref,
                      cb_ref, cc_ref, cc_halo_ref, dq_ref, dk_ref,
                      wpool_ref, bd64_ref, bd32_ref, gq_swa_ref, gk_swa_ref, conv_w_ref, g_out_ref,
                      gq_diff_ref, gk_diff_ref,
                      y_ref, qdn_ref, kdn_ref):
    n = pl.program_id(1)
    not_first = (n > 0).astype(F32)
    bd64 = bd64_ref[...]
    rows = lax.broadcasted_iota(jnp.int32, (ROW_BLOCK, GROUP_WIDTH), 0)
    cols = lax.broadcasted_iota(jnp.int32, (ROW_BLOCK, GROUP_WIDTH), 1)

    p = a_ref[...].astype(F32)
    p_ext = jnp.concatenate([a_halo_ref[...].astype(F32) * not_first, p], axis=0)
    s2 = p_ext + pltpu.roll(p_ext, 1, axis=0)
    s4 = s2 + pltpu.roll(s2, 2, axis=0)
    s8 = s4 + pltpu.roll(s4, 4, axis=0)
    s16 = s8 + pltpu.roll(s8, 8, axis=0)
    wsel = jnp.where(cols < 64, 2, jnp.where(cols < 128, 4, jnp.where(cols < 192, 8, 16)))
    ssel = jnp.where(cols < 64, s2[HALO:], jnp.where(cols < 128, s4[HALO:],
                                                      jnp.where(cols < 192, s8[HALO:], s16[HALO:])))
    cnt = jnp.minimum(n * ROW_BLOCK + rows + 1, wsel).astype(F32)
    d = ssel / cnt - p
    y_a = jnp.dot(d.astype(BF16), wpool_ref[...], preferred_element_type=F32)

    q = _group_rms_norm(bq_ref[...].astype(F32), bd64, HEAD_DIM, gq_swa_ref[...]) * (HEAD_DIM ** -0.5)
    qn = q.astype(BF16)
    bd64k = bd64[:128, :128]
    kv_cur = bkv_ref[...]
    kv_prev = bkv_prev_ref[...]
    k_cur = _group_rms_norm(kv_cur[:, :128].astype(F32), bd64k, HEAD_DIM, gk_swa_ref[...])
    k_prev = _group_rms_norm(kv_prev[:, :128].astype(F32), bd64k, HEAD_DIM, gk_swa_ref[...])
    kn = jnp.concatenate([k_prev, k_cur], axis=0).astype(BF16)
    v = jnp.concatenate([kv_prev[:, 128:], kv_cur[:, 128:]], axis=0)
    qi = lax.broadcasted_iota(jnp.int32, (ROW_BLOCK, 2 * ROW_BLOCK), 0)
    kj = lax.broadcasted_iota(jnp.int32, (ROW_BLOCK, 2 * ROW_BLOCK), 1)
    dist = qi + ROW_BLOCK - kj
    valid = (dist >= 0) & (dist < SWA_WINDOW) & ((n - 1) * ROW_BLOCK + kj >= 0)
    distf = dist.astype(F32)
    heads = []
    for h in range(SWA_Q_HEADS):
        hk = h // SWA_GQA
        sc = lax.dot_general(qn[:, h * 64:(h + 1) * 64], kn[:, hk * 64:(hk + 1) * 64], NT_DIMS,
                             preferred_element_type=F32)
        sc = jnp.where(valid, sc - ALIBI_SLOPES[h] * distf, NEG)
        sink = sinks_ref[h]
        mx = jnp.maximum(jnp.max(sc, axis=-1, keepdims=True), sink)
        e = jnp.exp(sc - mx)
        den = jnp.sum(e, axis=-1, keepdims=True) + jnp.exp(sink - mx)
        o = jnp.dot(e.astype(BF16), v[:, hk * 64:(hk + 1) * 64], preferred_element_type=F32)
        heads.append(o / den)
    y_b = jnp.concatenate(heads, axis=-1)

    zc = cc_ref[...].astype(F32) * ch_ref[...].astype(F32)
    zc_halo = cc_halo_ref[...].astype(F32) * ch_halo_ref[...].astype(F32) * not_first
    zc_ext = jnp.concatenate([zc_halo, zc], axis=0)
    cw = conv_w_ref[...]
    conv = (cw[0:1] * pltpu.roll(zc_ext, 2, axis=0)[HALO:] + cw[1:2] * pltpu.roll(zc_ext, 1, axis=0)[HALO:]
            + cw[2:3] * zc)
    y_c = cb_ref[...].astype(F32) * conv

    g_out = g_out_ref[...]
    for gi, y in enumerate((y_a, y_b, y_c)):
        lo, hi = gi * GROUP_WIDTH, (gi + 1) * GROUP_WIDTH
        y_ref[:, lo:hi] = _group_rms_norm(y, bd64, HEAD_DIM, g_out[:, lo:hi]).astype(y_ref.dtype)

    bd32 = bd32_ref[...]
    qd = _group_rms_norm(dq_ref[...].astype(F32), bd32, DIFF_QK_DIM, gq_diff_ref[...]) * (DIFF_QK_DIM ** -0.5)
    kd = _group_rms_norm(dk_ref[...].astype(F32), bd32, DIFF_QK_DIM, gk_diff_ref[...])
    qdn_ref[...] = qd.astype(qdn_ref.dtype)
    kdn_ref[...] = kd.astype(kdn_ref.dtype)


def _local_mix(z, batch, seq, sinks, wpool_bd, bd64, bd32, gq_swa, gk_swa, conv_w, g_out, gq_diff, gk_diff):
    t = batch * seq
    nb = seq // ROW_BLOCK
    halo_per_block = ROW_BLOCK // HALO

    def cur(col):
        return pl.BlockSpec((ROW_BLOCK, GROUP_WIDTH), lambda b, n: (b * nb + n, col))

    def prev(col):
        return pl.BlockSpec((ROW_BLOCK, GROUP_WIDTH), lambda b, n: (b * nb + jnp.maximum(n - 1, 0), col))

    def halo(col):
        return pl.BlockSpec(
            (HALO, GROUP_WIDTH),
            lambda b, n: (jnp.maximum((b * nb + n) * halo_per_block - 1, 0), col))

    def whole(arr):
        return pl.BlockSpec(arr.shape, lambda b, n: (0,) * arr.ndim)

    params = (wpool_bd, bd64, bd32, gq_swa, gk_swa, conv_w, g_out, gq_diff, gk_diff)
    in_specs = ([pl.BlockSpec(memory_space=pltpu.SMEM),
                 cur(0), halo(0), cur(1), cur(2), prev(2), cur(3), halo(3), cur(4), cur(5), halo(5),
                 cur(6), cur(7)] + [whole(a) for a in params])
    out_specs = [pl.BlockSpec((ROW_BLOCK, 3 * GROUP_WIDTH), lambda b, n: (b * nb + n, 0)),
                 pl.BlockSpec((ROW_BLOCK, GROUP_WIDTH), lambda b, n: (b * nb + n, 0)),
                 pl.BlockSpec((ROW_BLOCK, GROUP_WIDTH), lambda b, n: (b * nb + n, 0))]
    out_shape = [jax.ShapeDtypeStruct((t, 3 * GROUP_WIDTH), BF16),
                 jax.ShapeDtypeStruct((t, GROUP_WIDTH), BF16),
                 jax.ShapeDtypeStruct((t, GROUP_WIDTH), BF16)]
    return pl.pallas_call(
        _local_mix_kernel,
        grid=(batch, nb),
        in_specs=in_specs,
        out_specs=out_specs,
        out_shape=out_shape,
        compiler_params=pltpu.CompilerParams(dimension_semantics=("parallel", "parallel"),
                                             vmem_limit_bytes=VMEM_LIMIT),
        name="local_mix",
    )(sinks, *([z] * 12), *params)


def _diff_attn_kernel(q_ref, k_ref, v_ref, lamv_ref, bd64_ref, g_ref, y_ref,
                      qs_ref, vt_ref, p_ref, m_ref, den_ref, alpha_ref, acc_ref, *, one_minus_lam_init, lam_init):
    qi = pl.program_id(1)
    qb = DIFF_Q_BLOCK
    lamv = lamv_ref[...]
    lam = (jnp.exp(jnp.sum(lamv[0:1] * lamv[1:2], axis=-1, keepdims=True))
           - jnp.exp(jnp.sum(lamv[2:3] * lamv[3:4], axis=-1, keepdims=True)) + lam_init)
    n_maps = 2 * DIFF_HEADS

    @pl.when(qi == 0)
    def _():
        for c in range(vt_ref.shape[0]):
            vt_ref[c] = v_ref[c * qb:(c + 1) * qb, :].astype(F32).T.astype(BF16)

    q = q_ref[...]
    lane_group = lax.broadcasted_iota(jnp.int32, (qb, GROUP_WIDTH), 1) // DIFF_QK_DIM
    for g in range(n_maps):
        qs_ref[g * qb:(g + 1) * qb, :] = jnp.where(lane_group == g, q, jnp.zeros_like(q))
    m_ref[...] = jnp.full(m_ref.shape, NEG, F32)
    den_ref[...] = jnp.zeros(den_ref.shape, F32)
    acc_ref[...] = jnp.zeros(acc_ref.shape, F32)
    rel = (lax.broadcasted_iota(jnp.int32, (qb, qb), 1) - lax.broadcasted_iota(jnp.int32, (qb, qb), 0))
    relf = rel.astype(F32)

    def kv_step(c, diagonal):
        start = pl.multiple_of(c * qb, qb)
        sc_all = lax.dot_general(k_ref[pl.ds(start, qb), :], qs_ref[...], NT_DIMS,
                                 preferred_element_type=F32)
        off = ((qi - c) * qb).astype(F32)
        for g in range(n_maps):
            cols = slice(g * qb, (g + 1) * qb)
            slope = ALIBI_SLOPES[SWA_Q_HEADS + g // 2]
            sc = sc_all[:, cols] - slope * relf
            if diagonal:
                sc = jnp.where(rel >= 0, sc, NEG)
            shift = slope * off
            m_old = m_ref[:, cols]
            m_new = jnp.maximum(m_old, jnp.max(sc, axis=0, keepdims=True) - shift)
            alpha = jnp.exp(m_old - m_new)
            e = jnp.exp(sc - (m_new + shift))
            den_ref[:, cols] = alpha * den_ref[:, cols] + jnp.sum(e, axis=0, keepdims=True)
            m_ref[:, cols] = m_new
            alpha_ref[:, cols] = alpha
            p_ref[:, cols] = e.astype(BF16)
        pv = jnp.dot(vt_ref[c], p_ref[...], preferred_element_type=F32)
        for g in range(n_maps):
            cols = slice(g * qb, (g + 1) * qb)
            rows = slice((g // 2) * HEAD_DIM, (g // 2 + 1) * HEAD_DIM)
            acc_ref[:, cols] = alpha_ref[:, cols] * acc_ref[:, cols] + pv[rows, cols]

    def off_diagonal(c, carry):
        kv_step(c, False)
        return carry

    lax.fori_loop(0, qi, off_diagonal, 0)
    kv_step(qi, True)

    heads = []
    for h in range(DIFF_HEADS):
        c1 = slice(2 * h * qb, (2 * h + 1) * qb)
        c2 = slice((2 * h + 1) * qb, (2 * h + 2) * qb)
        heads.append(acc_ref[:, c1] / den_ref[:, c1] - lam * (acc_ref[:, c2] / den_ref[:, c2]))
    o = jnp.concatenate(heads, axis=0).T
    y = _group_rms_norm(o, bd64_ref[...], HEAD_DIM, g_ref[...]) * one_minus_lam_init
    y_ref[...] = y.astype(y_ref.dtype)


def _diff_attn(qdn, kdn, z, batch, seq, lamv, bd64, g_d, lam_init):
    t = batch * seq
    nq = seq // DIFF_Q_BLOCK
    v_col = 8
    stacked = 2 * DIFF_HEADS * DIFF_Q_BLOCK
    kernel = functools.partial(_diff_attn_kernel, one_minus_lam_init=1.0 - lam_init, lam_init=lam_init)
    return pl.pallas_call(
        kernel,
        grid=(batch, nq),
        in_specs=[pl.BlockSpec((DIFF_Q_BLOCK, GROUP_WIDTH), lambda b, i: (b * nq + i, 0)),
                  pl.BlockSpec((seq, GROUP_WIDTH), lambda b, i: (b, 0)),
                  pl.BlockSpec((seq, GROUP_WIDTH), lambda b, i: (b, v_col)),
                  pl.BlockSpec(lamv.shape, lambda b, i: (0, 0)),
                  pl.BlockSpec(bd64.shape, lambda b, i: (0, 0)),
                  pl.BlockSpec(g_d.shape, lambda b, i: (0, 0))],
        out_specs=pl.BlockSpec((DIFF_Q_BLOCK, GROUP_WIDTH), lambda b, i: (b * nq + i, 0)),
        out_shape=jax.ShapeDtypeStruct((t, GROUP_WIDTH), BF16),
        scratch_shapes=[pltpu.VMEM((stacked, GROUP_WIDTH), BF16),
                        pltpu.VMEM((seq // DIFF_Q_BLOCK, GROUP_WIDTH, DIFF_Q_BLOCK), BF16),
                        pltpu.VMEM((DIFF_Q_BLOCK, stacked), BF16),
                        pltpu.VMEM((1, stacked), F32),
                        pltpu.VMEM((1, stacked), F32),
                        pltpu.VMEM((1, stacked), F32),
                        pltpu.VMEM((HEAD_DIM, stacked), F32)],
        compiler_params=pltpu.CompilerParams(dimension_semantics=("parallel", "arbitrary"),
                                             vmem_limit_bytes=VMEM_LIMIT),
        name="diff_attn",
    )(qdn, kdn, z, lamv, bd64, g_d)


def _top_values(s, count, with_rank=False):
    vals = []
    cur = s
    rank = jnp.full(s.shape, float(count), F32) if with_rank else None
    for r in range(count):
        m = jnp.max(cur, axis=0, keepdims=True)
        vals.append(m)
        hit = cur >= m
        if with_rank:
            rank = jnp.where(hit, float(r), rank)
        if r + 1 < count:
            cur = jnp.where(hit, NEG, cur)
    return (vals, rank) if with_rank else vals


def _stack_rows(rows_list):
    n = len(rows_list)
    lanes = rows_list[0].shape[-1]
    ri = lax.broadcasted_iota(jnp.int32, (n, lanes), 0)
    out = jnp.zeros((n, lanes), F32)
    for r, row in enumerate(rows_list):
        out = jnp.where(ri == r, row, out)
    return out


def _out_router_kernel(yabc_ref, yd_ref, x_ref, wo1_ref, wo2_ref, g2_ref, wq_ref, keys_ref,
                       x1_ref, xn2_ref, n1_ref, w1_ref, r2_ref, w2_ref):
    acc = (jnp.dot(yabc_ref[...], wo1_ref[...], preferred_element_type=F32)
           + jnp.dot(yd_ref[...], wo2_ref[...], preferred_element_type=F32))
    x1 = x_ref[...] + acc
    x1_ref[...] = x1
    xn2 = (x1 * lax.rsqrt(jnp.mean(x1 * x1, axis=-1, keepdims=True) + EPS) * g2_ref[...]).astype(BF16)
    xn2_ref[...] = pltpu.bitcast(xn2, jnp.uint32)
    q = jnp.dot(xn2, wq_ref[...], preferred_element_type=F32)
    tm = q.shape[0]
    row8 = lax.broadcasted_iota(jnp.int32, (8, tm), 0)
    for h in range(PEER_HEADS):
        scores = []
        for c in range(2):
            lo = (2 * h + c) * PEER_HALF
            qhc = q[:, lo:lo + PEER_HALF].astype(BF16)
            scores.append(lax.dot_general(keys_ref[c], qhc, NT_DIMS, preferred_element_type=F32))
        s1, s2 = scores
        t1 = _top_values(s1, PEER_TOPK)
        t2, rank2 = _top_values(s2, PEER_TOPK, with_rank=True)
        v1 = _stack_rows(t1)
        v2_hi = _stack_rows(t2[8:])
        groups = [v1 + t2[0]]
        for b in range(1, 8):
            groups.append(jnp.where(row8 < PEER_TOPK // (b + 1), v1[:8] + t2[b], NEG))
        groups.append(t1[0] + v2_hi)
        best = _top_values(jnp.concatenate(groups, axis=0), PEER_TOPK)
        tau = best[PEER_TOPK - 1]
        zsum = jnp.sum(jnp.exp(_stack_rows(best) - best[0]), axis=0, keepdims=True)
        n1 = jnp.zeros(s1.shape, F32)
        for b in range(PEER_TOPK):
            n1 = n1 + jnp.where(s1 + t2[b] >= tau, 1.0, 0.0)
        n1_ref[h] = n1
        w1_ref[h] = jnp.exp(s1 - t1[0]) / zsum
        r2_ref[h] = pltpu.bitcast(rank2.astype(BF16), jnp.uint32)
        w2_ref[h] = pltpu.bitcast(jnp.exp(s2 - t2[0]).astype(BF16), jnp.uint32)


def _out_router(yabc, yd, x2d, wo1, wo2, g2, wq, keys):
    t = x2d.shape[0]
    tm = min(ROUTER_ROWS, t)

    def whole(arr):
        return pl.BlockSpec(arr.shape, lambda i: (0,) * arr.ndim)

    row_spec = pl.BlockSpec((PEER_HEADS, PEER_KEYS, tm), lambda i: (0, 0, i))
    row_shape = jax.ShapeDtypeStruct((PEER_HEADS, PEER_KEYS, t), F32)
    tile_spec = pl.BlockSpec((PEER_HEADS, PEER_KEYS // 2, tm), lambda i: (0, 0, i))
    tile_shape = jax.ShapeDtypeStruct((PEER_HEADS, PEER_KEYS // 2, t), jnp.uint32)
    return pl.pallas_call(
        _out_router_kernel,
        grid=(t // tm,),
        in_specs=[pl.BlockSpec((tm, 3 * GROUP_WIDTH), lambda i: (i, 0)),
                  pl.BlockSpec((tm, GROUP_WIDTH), lambda i: (i, 0)),
                  pl.BlockSpec((tm, D_MODEL), lambda i: (i, 0)),
                  whole(wo1), whole(wo2), whole(g2), whole(wq), whole(keys)],
        out_specs=[pl.BlockSpec((tm, D_MODEL), lambda i: (i, 0)),
                   pl.BlockSpec((tm // 2, D_MODEL), lambda i: (i, 0)),
                   row_spec, row_spec, tile_spec, tile_spec],
        out_shape=[jax.ShapeDtypeStruct((t, D_MODEL), F32),
                   jax.ShapeDtypeStruct((t // 2, D_MODEL), jnp.uint32),
                   row_shape, row_shape, tile_shape, tile_shape],
        compiler_params=pltpu.CompilerParams(dimension_semantics=("parallel",),
                                             vmem_limit_bytes=VMEM_LIMIT),
        name="out_router",
    )(yabc, yd, x2d, wo1, wo2, g2, wq, keys)


def _gate_activations(h_ref, a_ref, n1_ref, w1_ref, r2_ref, w2_ref, first_row, rows):
    tb = h_ref.shape[1]
    zero = jnp.zeros((), BF16)
    half = PEER_KEYS // 2
    for lc in range(tb // 128):
        ls = slice(lc * 128, (lc + 1) * 128)
        n1_rows = [n1_ref[hd, pl.ds(first_row, rows), ls] for hd in range(PEER_HEADS)]
        w1_rows = [w1_ref[hd, pl.ds(first_row, rows), ls] for hd in range(PEER_HEADS)]
        for r in range(rows):
            w = None
            for hd in range(PEER_HEADS):
                n1b = jnp.broadcast_to(n1_rows[hd][r:r + 1], (PEER_KEYS, 128)).astype(BF16)
                w1b = jnp.broadcast_to(w1_rows[hd][r:r + 1], (PEER_KEYS, 128)).astype(BF16)
                rank2 = pltpu.bitcast(r2_ref[hd, :, ls], BF16)
                w2 = pltpu.bitcast(w2_ref[hd, :, ls], BF16)
                term = jnp.where(rank2 < n1b, w2 * w1b, zero)
                w = term if w is None else w + term
            g = _gelu(h_ref[r * PEER_KEYS:(r + 1) * PEER_KEYS, ls]).astype(BF16)
            a_ref[r * half:(r + 1) * half, ls] = pltpu.bitcast(w * g, jnp.uint32)


def _peer_kernel(x_ref, xn_ref, u_ref, vt_ref, n1_ref, w1_ref, r2_ref, w2_ref, o_ref,
                 acc_ref, h_ref, a_ref, *, rows):
    j = pl.program_id(1)

    @pl.when(j == 0)
    def _():
        acc_ref[...] = jnp.zeros_like(acc_ref)

    h_ref[...] = lax.dot_general(pltpu.bitcast(u_ref[...], BF16), pltpu.bitcast(xn_ref[...], BF16), NT_DIMS,
                                 preferred_element_type=F32)
    first_row = pl.multiple_of(j * rows, rows)
    _gate_activations(h_ref, a_ref, n1_ref, w1_ref, r2_ref, w2_ref, first_row, rows)
    acc_ref[...] += jnp.dot(pltpu.bitcast(vt_ref[...], BF16), pltpu.bitcast(a_ref[...], BF16),
                            preferred_element_type=F32)

    @pl.when(j == pl.num_programs(1) - 1)
    def _():
        o_ref[...] = x_ref[...] + acc_ref[...].T


def _peer(x1, xn2, u_words, vt_words, n1, w1, r2, w2):
    t = x1.shape[0]
    tb = min(PEER_TOKENS, t)
    ec = PEER_EXPERT_CHUNK
    rows = ec // PEER_KEYS
    assert rows % 8 == 0 and tb % 128 == 0 and t % tb == 0 and PEER_EXPERTS % ec == 0
    row_spec = pl.BlockSpec((PEER_HEADS, PEER_KEYS, tb), lambda i, j: (0, 0, i))
    tile_spec = pl.BlockSpec((PEER_HEADS, PEER_KEYS // 2, tb), lambda i, j: (0, 0, i))
    kernel = functools.partial(_peer_kernel, rows=rows)
    return pl.pallas_call(
        kernel,
        grid=(t // tb, PEER_EXPERTS // ec),
        in_specs=[pl.BlockSpec((tb, D_MODEL), lambda i, j: (i, 0)),
                  pl.BlockSpec((tb // 2, D_MODEL), lambda i, j: (i, 0)),
                  pl.BlockSpec((ec // 2, D_MODEL), lambda i, j: (j, 0)),
                  pl.BlockSpec((D_MODEL // 2, ec), lambda i, j: (0, j)),
                  row_spec, row_spec, tile_spec, tile_spec],
        out_specs=pl.BlockSpec((tb, D_MODEL), lambda i, j: (i, 0)),
        out_shape=jax.ShapeDtypeStruct((t, D_MODEL), F32),
        scratch_shapes=[pltpu.VMEM((D_MODEL, tb), F32),
                        pltpu.VMEM((ec, tb), F32),
                        pltpu.VMEM((ec // 2, tb), jnp.uint32)],
        compiler_params=pltpu.CompilerParams(dimension_semantics=("parallel", "arbitrary"),
                                             vmem_limit_bytes=VMEM_LIMIT),
        name="peer",
    )(x1, xn2, u_words, vt_words, n1, w1, r2, w2)


def _pack_row_pairs(x):
    bits = lax.bitcast_convert_type(x, jnp.uint16).astype(jnp.uint32)
    return bits[0::2] | (bits[1::2] << 16)


def _block_diag_ones(width, group):
    idx = np.arange(width) // group
    return jnp.asarray((idx[:, None] == idx[None, :]).astype(np.float32), BF16)


def _layer(x2d, batch, seq, layer_idx, norm1_g, w_in, w_pool, swa_q_norm, swa_k_norm, swa_sinks, conv_w,
           diff_q_norm, diff_k_norm, lam_q1, lam_k1, lam_q2, lam_k2, out_norm_g, w_out, norm2_g, w_query,
           sub_keys, peer_u, peer_v):
    lam_init = 0.8 - 0.6 * float(np.exp(-0.3 * layer_idx))
    bd64 = _block_diag_ones(GROUP_WIDTH, HEAD_DIM)
    bd32 = _block_diag_ones(GROUP_WIDTH, DIFF_QK_DIM)
    wpool_bd = jax.scipy.linalg.block_diag(*[w_pool[g] for g in range(len(POOL_WINDOWS))]).astype(BF16)

    z = _in_proj(x2d, norm1_g.reshape(1, D_MODEL), w_in.astype(BF16))
    yabc, qdn, kdn = _local_mix(
        z, batch, seq, swa_sinks.astype(F32), wpool_bd, bd64, bd32,
        jnp.tile(swa_q_norm, SWA_Q_HEADS).reshape(1, GROUP_WIDTH),
        jnp.tile(swa_k_norm, 2).reshape(1, 128),
        conv_w, out_norm_g[:3 * GROUP_WIDTH].reshape(1, 3 * GROUP_WIDTH),
        jnp.tile(diff_q_norm, 2 * DIFF_HEADS).reshape(1, GROUP_WIDTH),
        jnp.tile(diff_k_norm, 2 * DIFF_HEADS).reshape(1, GROUP_WIDTH))
    lamv = jnp.stack([lam_q1, lam_k1, lam_q2, lam_k2]).astype(F32)
    yd = _diff_attn(qdn, kdn, z, batch, seq, lamv, bd64,
                    out_norm_g[3 * GROUP_WIDTH:].reshape(1, GROUP_WIDTH), lam_init)
    w_out_bf16 = w_out.astype(BF16)
    x1, xn2, n1, w1, r2, w2 = _out_router(
        yabc, yd, x2d, w_out_bf16[:3 * GROUP_WIDTH], w_out_bf16[3 * GROUP_WIDTH:],
        norm2_g.reshape(1, D_MODEL), w_query.astype(BF16), sub_keys.astype(BF16))
    return _peer(x1, xn2, _pack_row_pairs(peer_u.astype(BF16)), _pack_row_pairs(peer_v.T.astype(BF16)),
                 n1, w1, r2, w2)


def kernel(x, norm1_g, w_in, w_pool, swa_q_norm, swa_k_norm, swa_sinks, conv_w, diff_q_norm, diff_k_norm,
           lam_q1, lam_k1, lam_q2, lam_k2, out_norm_g, w_out, norm2_g, w_query, sub_keys, peer_u, peer_v):
    batch, seq, d = x.shape
    x2d = x.reshape(batch * seq, d)
    depth = norm1_g.shape[0]
    for l in range(depth):
        x2d = _layer(x2d, batch, seq, l, norm1_g[l], w_in[l], w_pool[l], swa_q_norm[l], swa_k_norm[l],
                     swa_sinks[l], conv_w[l], diff_q_norm[l], diff_k_norm[l], lam_q1[l], lam_k1[l],
                     lam_q2[l], lam_k2[l], out_norm_g[l], w_out[l], norm2_g[l], w_query[l], sub_keys[l],
                     peer_u[l], peer_v[l])
    return x2d.reshape(batch, seq, d)
```

```python
import functools

import jax
import jax.numpy as jnp
import numpy as np
from jax import lax
from jax.experimental import pallas as pl
from jax.experimental.pallas import tpu as pltpu

F32 = jnp.float32
BF16 = jnp.bfloat16

D_MODEL = 1024
EPS = 1e-6
HEAD_DIM = 64
GROUP_WIDTH = 256
POOL_WINDOWS = (2, 4, 8, 16)
SWA_Q_HEADS = 4
SWA_GQA = 2
SWA_WINDOW = 128
DIFF_HEADS = 4
DIFF_QK_DIM = 32
IN_PROJ_WIDTH = 2304
PEER_HEADS = 8
PEER_KEYS = 128
PEER_EXPERTS = PEER_KEYS * PEER_KEYS
PEER_HALF = 128
PEER_TOPK = 16
ALIBI_SLOPES = tuple(2.0 ** (-(i + 1)) for i in range(SWA_Q_HEADS + DIFF_HEADS))

NEG = -1e30
HALO = 16
ROW_BLOCK = 128
DIFF_Q_BLOCK = 256
IN_PROJ_ROWS = 512
ROUTER_ROWS = 256
PEER_TOKENS = 512
PEER_EXPERT_CHUNK = 2048
VMEM_LIMIT = 56 * 1024 * 1024

NT_DIMS = (((1,), (1,)), ((), ()))


def _group_sumsq(x, bd):
    x2 = x * x
    hi = x2.astype(BF16)
    lo = (x2 - hi.astype(F32)).astype(BF16)
    return (jnp.dot(hi, bd, preferred_element_type=F32)
            + jnp.dot(lo, bd, preferred_element_type=F32))


def _group_rms_norm(x, bd, group, gain):
    return x * lax.rsqrt(_group_sumsq(x, bd) * (1.0 / group) + EPS) * gain


def _gelu(h):
    return 0.5 * h * (1.0 + lax.erf(h * np.float32(np.sqrt(0.5))))


def _in_proj_kernel(x_ref, g_ref, w_ref, z_ref):
    x = x_ref[...]
    xn = x * lax.rsqrt(jnp.mean(x * x, axis=-1, keepdims=True) + EPS) * g_ref[...]
    z = jnp.dot(xn.astype(BF16), w_ref[...], preferred_element_type=F32)
    z_ref[...] = z.astype(z_ref.dtype)


def _in_proj(x2d, g, w_bf16):
    t = x2d.shape[0]
    tm = min(IN_PROJ_ROWS, t)
    return pl.pallas_call(
        _in_proj_kernel,
        grid=(t // tm,),
        in_specs=[pl.BlockSpec((tm, D_MODEL), lambda i: (i, 0)),
                  pl.BlockSpec((1, D_MODEL), lambda i: (0, 0)),
                  pl.BlockSpec((D_MODEL, IN_PROJ_WIDTH), lambda i: (0, 0))],
        out_specs=pl.BlockSpec((tm, IN_PROJ_WIDTH), lambda i: (i, 0)),
        out_shape=jax.ShapeDtypeStruct((t, IN_PROJ_WIDTH), BF16),
        compiler_params=pltpu.CompilerParams(dimension_semantics=("parallel",),
                                             vmem_limit_bytes=VMEM_LIMIT),
        name="in_proj",
    )(x2d, g, w_bf16)


def _local_mix_kernel(sinks_ref, a_ref, a_halo_ref, bq_ref, bkv_ref, bkv_prev_ref, ch_ref, ch_halo_ref,
                      cb_ref, cc_ref, cc_halo_ref, dq_ref, dk_ref,
                      wpool_ref, bd64_ref, bd32_ref, gq_swa_ref, gk_swa_ref, conv_w_ref, g_out_ref,
                      gq_diff_ref, gk_diff_ref,
                      y_ref, qdn_ref, kdn_ref):
    n = pl.program_id(1)
    not_first = (n > 0).astype(F32)
    bd64 = bd64_ref[...]
    rows = lax.broadcasted_iota(jnp.int32, (ROW_BLOCK, GROUP_WIDTH), 0)
    cols = lax.broadcasted_iota(jnp.int32, (ROW_BLOCK, GROUP_WIDTH), 1)

    p = a_ref[...].astype(F32)
    p_ext = jnp.concatenate([a_halo_ref[...].astype(F32) * not_first, p], axis=0)
    s2 = p_ext + pltpu.roll(p_ext, 1, axis=0)
    s4 = s2 + pltpu.roll(s2, 2, axis=0)
    s8 = s4 + pltpu.roll(s4, 4, axis=0)
    s16 = s8 + pltpu.roll(s8, 8, axis=0)
    wsel = jnp.where(cols < 64, 2, jnp.where(cols < 128, 4, jnp.where(cols < 192, 8, 16)))
    ssel = jnp.where(cols < 64, s2[HALO:], jnp.where(cols < 128, s4[HALO:],
                                                      jnp.where(cols < 192, s8[HALO:], s16[HALO:])))
    cnt = jnp.minimum(n * ROW_BLOCK + rows + 1, wsel).astype(F32)
    d = ssel / cnt - p
    y_a = jnp.dot(d.astype(BF16), wpool_ref[...], preferred_element_type=F32)

    q = _group_rms_norm(bq_ref[...].astype(F32), bd64, HEAD_DIM, gq_swa_ref[...]) * (HEAD_DIM ** -0.5)
    qn = q.astype(BF16)
    bd64k = bd64[:128, :128]
    kv_cur = bkv_ref[...]
    kv_prev = bkv_prev_ref[...]
    k_cur = _group_rms_norm(kv_cur[:, :128].astype(F32), bd64k, HEAD_DIM, gk_swa_ref[...])
    k_prev = _group_rms_norm(kv_prev[:, :128].astype(F32), bd64k, HEAD_DIM, gk_swa_ref[...])
    kn = jnp.concatenate([k_prev, k_cur], axis=0).astype(BF16)
    v = jnp.concatenate([kv_prev[:, 128:], kv_cur[:, 128:]], axis=0)
    qi = lax.broadcasted_iota(jnp.int32, (ROW_BLOCK, 2 * ROW_BLOCK), 0)
    kj = lax.broadcasted_iota(jnp.int32, (ROW_BLOCK, 2 * ROW_BLOCK), 1)
    dist = qi + ROW_BLOCK - kj
    valid = (dist >= 0) & (dist < SWA_WINDOW) & ((n - 1) * ROW_BLOCK + kj >= 0)
    distf = dist.astype(F32)
    heads = []
    for h in range(SWA_Q_HEADS):
        hk = h // SWA_GQA
        sc = lax.dot_general(qn[:, h * 64:(h + 1) * 64], kn[:, hk * 64:(hk + 1) * 64], NT_DIMS,
                             preferred_element_type=F32)
        sc = jnp.where(valid, sc - ALIBI_SLOPES[h] * distf, NEG)
        sink = sinks_ref[h]
        mx = jnp.maximum(jnp.max(sc, axis=-1, keepdims=True), sink)
        e = jnp.exp(sc - mx)
        den = jnp.sum(e, axis=-1, keepdims=True) + jnp.exp(sink - mx)
        o = jnp.dot(e.astype(BF16), v[:, hk * 64:(hk + 1) * 64], preferred_element_type=F32)
        heads.append(o / den)
    y_b = jnp.concatenate(heads, axis=-1)

    zc = cc_ref[...].astype(F32) * ch_ref[...].astype(F32)
    zc_halo = cc_halo_ref[...].astype(F32) * ch_halo_ref[...].astype(F32) * not_first
    zc_ext = jnp.concatenate([zc_halo, zc], axis=0)
    cw = conv_w_ref[...]
    conv = (cw[0:1] * pltpu.roll(zc_ext, 2, axis=0)[HALO:] + cw[1:2] * pltpu.roll(zc_ext, 1, axis=0)[HALO:]
            + cw[2:3] * zc)
    y_c = cb_ref[...].astype(F32) * conv

    g_out = g_out_ref[...]
    for gi, y in enumerate((y_a, y_b, y_c)):
        lo, hi = gi * GROUP_WIDTH, (gi + 1) * GROUP_WIDTH
        y_ref[:, lo:hi] = _group_rms_norm(y, bd64, HEAD_DIM, g_out[:, lo:hi]).astype(y_ref.dtype)

    bd32 = bd32_ref[...]
    qd = _group_rms_norm(dq_ref[...].astype(F32), bd32, DIFF_QK_DIM, gq_diff_ref[...]) * (DIFF_QK_DIM ** -0.5)
    kd = _group_rms_norm(dk_ref[...].astype(F32), bd32, DIFF_QK_DIM, gk_diff_ref[...])
    qdn_ref[...] = qd.astype(qdn_ref.dtype)
    kdn_ref[...] = kd.astype(kdn_ref.dtype)


def _local_mix(z, batch, seq, sinks, wpool_bd, bd64, bd32, gq_swa, gk_swa, conv_w, g_out, gq_diff, gk_diff):
    t = batch * seq
    nb = seq // ROW_BLOCK
    halo_per_block = ROW_BLOCK // HALO

    def cur(col):
        return pl.BlockSpec((ROW_BLOCK, GROUP_WIDTH), lambda b, n: (b * nb + n, col))

    def prev(col):
        return pl.BlockSpec((ROW_BLOCK, GROUP_WIDTH), lambda b, n: (b * nb + jnp.maximum(n - 1, 0), col))

    def halo(col):
        return pl.BlockSpec(
            (HALO, GROUP_WIDTH),
            lambda b, n: (jnp.maximum((b * nb + n) * halo_per_block - 1, 0), col))

    def whole(arr):
        return pl.BlockSpec(arr.shape, lambda b, n: (0,) * arr.ndim)

    params = (wpool_bd, bd64, bd32, gq_swa, gk_swa, conv_w, g_out, gq_diff, gk_diff)
    in_specs = ([pl.BlockSpec(memory_space=pltpu.SMEM),
                 cur(0), halo(0), cur(1), cur(2), prev(2), cur(3), halo(3), cur(4), cur(5), halo(5),
                 cur(6), cur(7)] + [whole(a) for a in params])
    out_specs = [pl.BlockSpec((ROW_BLOCK, 3 * GROUP_WIDTH), lambda b, n: (b * nb + n, 0)),
                 pl.BlockSpec((ROW_BLOCK, GROUP_WIDTH), lambda b, n: (b * nb + n, 0)),
                 pl.BlockSpec((ROW_BLOCK, GROUP_WIDTH), lambda b, n: (b * nb + n, 0))]
    out_shape = [jax.ShapeDtypeStruct((t, 3 * GROUP_WIDTH), BF16),
                 jax.ShapeDtypeStruct((t, GROUP_WIDTH), BF16),
                 jax.ShapeDtypeStruct((t, GROUP_WIDTH), BF16)]
    return pl.pallas_call(
        _local_mix_kernel,
        grid=(batch, nb),
        in_specs=in_specs,
        out_specs=out_specs,
        out_shape=out_shape,
        compiler_params=pltpu.CompilerParams(dimension_semantics=("parallel", "parallel"),
                                             vmem_limit_bytes=VMEM_LIMIT),
        name="local_mix",
    )(sinks, *([z] * 12), *params)


def _diff_attn_kernel(q_ref, k_ref, v_ref, lamv_ref, bd64_ref, g_ref, y_ref,
                      qs_ref, vt_ref, p_ref, m_ref, den_ref, alpha_ref, acc_ref, *, one_minus_lam_init, lam_init):
    qi = pl.program_id(1)
    qb = DIFF_Q_BLOCK
    lamv = lamv_ref[...]
    lam = (jnp.exp(jnp.sum(lamv[0:1] * lamv[1:2], axis=-1, keepdims=True))
           - jnp.exp(jnp.sum(lamv[2:3] * lamv[3:4], axis=-1, keepdims=True)) + lam_init)
    n_maps = 2 * DIFF_HEADS

    @pl.when(qi == 0)
    def _():
        for c in range(vt_ref.shape[0]):
            vt_ref[c] = v_ref[c * qb:(c + 1) * qb, :].astype(F32).T.astype(BF16)

    q = q_ref[...]
    lane_group = lax.broadcasted_iota(jnp.int32, (qb, GROUP_WIDTH), 1) // DIFF_QK_DIM
    for g in range(n_maps):
        qs_ref[g * qb:(g + 1) * qb, :] = jnp.where(lane_group == g, q, jnp.zeros_like(q))
    m_ref[...] = jnp.full(m_ref.shape, NEG, F32)
    den_ref[...] = jnp.zeros(den_ref.shape, F32)
    acc_ref[...] = jnp.zeros(acc_ref.shape, F32)
    rel = (lax.broadcasted_iota(jnp.int32, (qb, qb), 1) - lax.broadcasted_iota(jnp.int32, (qb, qb), 0))
    relf = rel.astype(F32)

    def kv_step(c, diagonal):
        start = pl.multiple_of(c * qb, qb)
        sc_all = lax.dot_general(k_ref[pl.ds(start, qb), :], qs_ref[...], NT_DIMS,
                                 preferred_element_type=F32)
        off = ((qi - c) * qb).astype(F32)
        for g in range(n_maps):
            cols = slice(g * qb, (g + 1) * qb)
            slope = ALIBI_SLOPES[SWA_Q_HEADS + g // 2]
            sc = sc_all[:, cols] - slope * relf
            if diagonal:
                sc = jnp.where(rel >= 0, sc, NEG)
            shift = slope * off
            m_old = m_ref[:, cols]
            m_new = jnp.maximum(m_old, jnp.max(sc, axis=0, keepdims=True) - shift)
            alpha = jnp.exp(m_old - m_new)
            e = jnp.exp(sc - (m_new + shift))
            den_ref[:, cols] = alpha * den_ref[:, cols] + jnp.sum(e, axis=0, keepdims=True)
            m_ref[:, cols] = m_new
            alpha_ref[:, cols] = alpha
            p_ref[:, cols] = e.astype(BF16)
        pv = jnp.dot(vt_ref[c], p_ref[...], preferred_element_type=F32)
        for g in range(n_maps):
            cols = slice(g * qb, (g + 1) * qb)
            rows = slice((g // 2) * HEAD_DIM, (g // 2 + 1) * HEAD_DIM)
            acc_ref[:, cols] = alpha_ref[:, cols] * acc_ref[:, cols] + pv[rows, cols]

    def off_diagonal(c, carry):
        kv_step(c, False)
        return carry

    lax.fori_loop(0, qi, off_diagonal, 0)
    kv_step(qi, True)

    heads = []
    for h in range(DIFF_HEADS):
        c1 = slice(2 * h * qb, (2 * h + 1) * qb)
        c2 = slice((2 * h + 1) * qb, (2 * h + 2) * qb)
        heads.append(acc_ref[:, c1] / den_ref[:, c1] - lam * (acc_ref[:, c2] / den_ref[:, c2]))
    o = jnp.concatenate(heads, axis=0).T
    y = _group_rms_norm(o, bd64_ref[...], HEAD_DIM, g_ref[...]) * one_minus_lam_init
    y_ref[...] = y.astype(y_ref.dtype)


def _diff_attn(qdn, kdn, z, batch, seq, lamv, bd64, g_d, lam_init):
    t = batch * seq
    nq = seq // DIFF_Q_BLOCK
    v_col = 8
    stacked = 2 * DIFF_HEADS * DIFF_Q_BLOCK
    kernel = functools.partial(_diff_attn_kernel, one_minus_lam_init=1.0 - lam_init, lam_init=lam_init)
    return pl.pallas_call(
        kernel,
        grid=(batch, nq),
        in_specs=[pl.BlockSpec((DIFF_Q_BLOCK, GROUP_WIDTH), lambda b, i: (b * nq + i, 0)),
                  pl.BlockSpec((seq, GROUP_WIDTH), lambda b, i: (b, 0)),
                  pl.BlockSpec((seq, GROUP_WIDTH), lambda b, i: (b, v_col)),
                  pl.BlockSpec(lamv.shape, lambda b, i: (0, 0)),
                  pl.BlockSpec(bd64.shape, lambda b, i: (0, 0)),
                  pl.BlockSpec(g_d.shape, lambda b, i: (0, 0))],
        out_specs=pl.BlockSpec((DIFF_Q_BLOCK, GROUP_WIDTH), lambda b, i: (b * nq + i, 0)),
        out_shape=jax.ShapeDtypeStruct((t, GROUP_WIDTH), BF16),
        scratch_shapes=[pltpu.VMEM((stacked, GROUP_WIDTH), BF16),
                        pltpu.VMEM((seq // DIFF_Q_BLOCK, GROUP_WIDTH, DIFF_Q_BLOCK), BF16),
                        pltpu.VMEM((DIFF_Q_BLOCK, stacked), BF16),
                        pltpu.VMEM((1, stacked), F32),
                        pltpu.VMEM((1, stacked), F32),
                        pltpu.VMEM((1, stacked), F32),
                        pltpu.VMEM((HEAD_DIM, stacked), F32)],
        compiler_params=pltpu.CompilerParams(dimension_semantics=("parallel", "arbitrary"),
                                             vmem_limit_bytes=VMEM_LIMIT),
        name="diff_attn",
    )(qdn, kdn, z, lamv, bd64, g_d)


def _top_values(s, count, with_rank=False):
    vals = []
    cur = s
    rank = jnp.full(s.shape, float(count), F32) if with_rank else None
    for r in range(count):
        m = jnp.max(cur, axis=0, keepdims=True)
        vals.append(m)
        hit = cur >= m
        if with_rank:
            rank = jnp.where(hit, float(r), rank)
        if r + 1 < count:
            cur = jnp.where(hit, NEG, cur)
    return (vals, rank) if with_rank else vals


def _dup_bf16_words(x):
    bits = pltpu.bitcast(x.astype(BF16).astype(F32), jnp.uint32)
    return bits | (bits >> 16)


def _stack_rows(rows_list):
    n = len(rows_list)
    lanes = rows_list[0].shape[-1]
    ri = lax.broadcasted_iota(jnp.int32, (n, lanes), 0)
    out = jnp.zeros((n, lanes), F32)
    for r, row in enumerate(rows_list):
        out = jnp.where(ri == r, row, out)
    return out


def _out_router_kernel(yabc_ref, yd_ref, x_ref, wo1_ref, wo2_ref, g2_ref, wq_ref, keys_ref,
                       x1_ref, xn2_ref, n1_ref, w1_ref, r2_ref, w2_ref):
    acc = (jnp.dot(yabc_ref[...], wo1_ref[...], preferred_element_type=F32)
           + jnp.dot(yd_ref[...], wo2_ref[...], preferred_element_type=F32))
    x1 = x_ref[...] + acc
    x1_ref[...] = x1
    xn2 = (x1 * lax.rsqrt(jnp.mean(x1 * x1, axis=-1, keepdims=True) + EPS) * g2_ref[...]).astype(BF16)
    xn2_ref[...] = pltpu.bitcast(xn2, jnp.uint32)
    q = jnp.dot(xn2, wq_ref[...], preferred_element_type=F32)
    tm = q.shape[0]
    row8 = lax.broadcasted_iota(jnp.int32, (8, tm), 0)
    for h in range(PEER_HEADS):
        scores = []
        for c in range(2):
            lo = (2 * h + c) * PEER_HALF
            qhc = q[:, lo:lo + PEER_HALF].astype(BF16)
            scores.append(lax.dot_general(keys_ref[c], qhc, NT_DIMS, preferred_element_type=F32))
        s1, s2 = scores
        t1 = _top_values(s1, PEER_TOPK)
        t2, rank2 = _top_values(s2, PEER_TOPK, with_rank=True)
        v1 = _stack_rows(t1)
        v2_hi = _stack_rows(t2[8:])
        groups = [v1 + t2[0]]
        for b in range(1, 8):
            groups.append(jnp.where(row8 < PEER_TOPK // (b + 1), v1[:8] + t2[b], NEG))
        groups.append(t1[0] + v2_hi)
        best = _top_values(jnp.concatenate(groups, axis=0), PEER_TOPK)
        tau = best[PEER_TOPK - 1]
        zsum = jnp.sum(jnp.exp(_stack_rows(best) - best[0]), axis=0, keepdims=True)
        n1 = jnp.zeros(s1.shape, F32)
        for b in range(PEER_TOPK):
            n1 = n1 + jnp.where(s1 + t2[b] >= tau, 1.0, 0.0)
        n1_ref[h] = _dup_bf16_words(n1)
        w1_ref[h] = _dup_bf16_words(jnp.exp(s1 - t1[0]) / zsum)
        r2_ref[h] = pltpu.bitcast(rank2.astype(BF16), jnp.uint32)
        w2_ref[h] = pltpu.bitcast(jnp.exp(s2 - t2[0]).astype(BF16), jnp.uint32)


def _out_router(yabc, yd, x2d, wo1, wo2, g2, wq, keys):
    t = x2d.shape[0]
    tm = min(ROUTER_ROWS, t)

    def whole(arr):
        return pl.BlockSpec(arr.shape, lambda i: (0,) * arr.ndim)

    row_spec = pl.BlockSpec((PEER_HEADS, PEER_KEYS, tm), lambda i: (0, 0, i))
    row_shape = jax.ShapeDtypeStruct((PEER_HEADS, PEER_KEYS, t), jnp.uint32)
    tile_spec = pl.BlockSpec((PEER_HEADS, PEER_KEYS // 2, tm), lambda i: (0, 0, i))
    tile_shape = jax.ShapeDtypeStruct((PEER_HEADS, PEER_KEYS // 2, t), jnp.uint32)
    return pl.pallas_call(
        _out_router_kernel,
        grid=(t // tm,),
        in_specs=[pl.BlockSpec((tm, 3 * GROUP_WIDTH), lambda i: (i, 0)),
                  pl.BlockSpec((tm, GROUP_WIDTH), lambda i: (i, 0)),
                  pl.BlockSpec((tm, D_MODEL), lambda i: (i, 0)),
                  whole(wo1), whole(wo2), whole(g2), whole(wq), whole(keys)],
        out_specs=[pl.BlockSpec((tm, D_MODEL), lambda i: (i, 0)),
                   pl.BlockSpec((tm // 2, D_MODEL), lambda i: (i, 0)),
                   row_spec, row_spec, tile_spec, tile_spec],
        out_shape=[jax.ShapeDtypeStruct((t, D_MODEL), F32),
                   jax.ShapeDtypeStruct((t // 2, D_MODEL), jnp.uint32),
                   row_shape, row_shape, tile_shape, tile_shape],
        compiler_params=pltpu.CompilerParams(dimension_semantics=("parallel",),
                                             vmem_limit_bytes=VMEM_LIMIT),
        name="out_router",
    )(yabc, yd, x2d, wo1, wo2, g2, wq, keys)


def _gate_activations(h_ref, a_ref, n1_ref, w1_ref, r2_ref, w2_ref, first_row, rows):
    tb = h_ref.shape[1]
    zero = jnp.zeros((), BF16)
    half = PEER_KEYS // 2
    for lc in range(tb // 128):
        ls = slice(lc * 128, (lc + 1) * 128)
        n1_rows = [n1_ref[hd, pl.ds(first_row, rows), ls] for hd in range(PEER_HEADS)]
        w1_rows = [w1_ref[hd, pl.ds(first_row, rows), ls] for hd in range(PEER_HEADS)]
        for r in range(rows):
            w = None
            for hd in range(PEER_HEADS):
                n1b = pltpu.bitcast(jnp.broadcast_to(n1_rows[hd][r:r + 1], (half, 128)), BF16)
                w1b = pltpu.bitcast(jnp.broadcast_to(w1_rows[hd][r:r + 1], (half, 128)), BF16)
                rank2 = pltpu.bitcast(r2_ref[hd, :, ls], BF16)
                w2 = pltpu.bitcast(w2_ref[hd, :, ls], BF16)
                term = jnp.where(rank2 < n1b, w2 * w1b, zero)
                w = term if w is None else w + term
            g = _gelu(h_ref[r * PEER_KEYS:(r + 1) * PEER_KEYS, ls]).astype(BF16)
            a_ref[r * half:(r + 1) * half, ls] = pltpu.bitcast(w * g, jnp.uint32)


def _peer_kernel(x_ref, xn_ref, u_ref, vt_ref, n1_ref, w1_ref, r2_ref, w2_ref, o_ref,
                 acc_ref, h_ref, a_ref, *, rows):
    j = pl.program_id(1)

    @pl.when(j == 0)
    def _():
        acc_ref[...] = jnp.zeros_like(acc_ref)

    h_ref[...] = lax.dot_general(u_ref[...], pltpu.bitcast(xn_ref[...], BF16), NT_DIMS,
                                 preferred_element_type=F32)
    first_row = pl.multiple_of(j * rows, rows)
    _gate_activations(h_ref, a_ref, n1_ref, w1_ref, r2_ref, w2_ref, first_row, rows)
    acc_ref[...] += jnp.dot(vt_ref[...], pltpu.bitcast(a_ref[...], BF16), preferred_element_type=F32)

    @pl.when(j == pl.num_programs(1) - 1)
    def _():
        o_ref[...] = x_ref[...] + acc_ref[...].T


def _peer(x1, xn2, u_bf16, vt_bf16, n1, w1, r2, w2):
    t = x1.shape[0]
    tb = min(PEER_TOKENS, t)
    ec = PEER_EXPERT_CHUNK
    rows = ec // PEER_KEYS
    assert rows % 8 == 0 and tb % 128 == 0 and t % tb == 0 and PEER_EXPERTS % ec == 0
    row_spec = pl.BlockSpec((PEER_HEADS, PEER_KEYS, tb), lambda i, j: (0, 0, i))
    tile_spec = pl.BlockSpec((PEER_HEADS, PEER_KEYS // 2, tb), lambda i, j: (0, 0, i))
    kernel = functools.partial(_peer_kernel, rows=rows)
    return pl.pallas_call(
        kernel,
        grid=(t // tb, PEER_EXPERTS // ec),
        in_specs=[pl.BlockSpec((tb, D_MODEL), lambda i, j: (i, 0)),
                  pl.BlockSpec((tb // 2, D_MODEL), lambda i, j: (i, 0)),
                  pl.BlockSpec((ec, D_MODEL), lambda i, j: (j, 0)),
                  pl.BlockSpec((D_MODEL, ec), lambda i, j: (0, j)),
                  row_spec, row_spec, tile_spec, tile_spec],
        out_specs=pl.BlockSpec((tb, D_MODEL), lambda i, j: (i, 0)),
        out_shape=jax.ShapeDtypeStruct((t, D_MODEL), F32),
        scratch_shapes=[pltpu.VMEM((D_MODEL, tb), F32),
                        pltpu.VMEM((ec, tb), F32),
                        pltpu.VMEM((ec // 2, tb), jnp.uint32)],
        compiler_params=pltpu.CompilerParams(dimension_semantics=("parallel", "arbitrary"),
                                             vmem_limit_bytes=VMEM_LIMIT),
        name="peer",
    )(x1, xn2, u_bf16, vt_bf16, n1, w1, r2, w2)


def _block_diag_ones(width, group):
    idx = np.arange(width) // group
    return jnp.asarray((idx[:, None] == idx[None, :]).astype(np.float32), BF16)


def _layer(x2d, batch, seq, layer_idx, norm1_g, w_in, w_pool, swa_q_norm, swa_k_norm, swa_sinks, conv_w,
           diff_q_norm, diff_k_norm, lam_q1, lam_k1, lam_q2, lam_k2, out_norm_g, w_out, norm2_g, w_query,
           sub_keys, peer_u, peer_v):
    lam_init = 0.8 - 0.6 * float(np.exp(-0.3 * layer_idx))
    bd64 = _block_diag_ones(GROUP_WIDTH, HEAD_DIM)
    bd32 = _block_diag_ones(GROUP_WIDTH, DIFF_QK_DIM)
    wpool_bd = jax.scipy.linalg.block_diag(*[w_pool[g] for g in range(len(POOL_WINDOWS))]).astype(BF16)

    z = _in_proj(x2d, norm1_g.reshape(1, D_MODEL), w_in.astype(BF16))
    yabc, qdn, kdn = _local_mix(
        z, batch, seq, swa_sinks.astype(F32), wpool_bd, bd64, bd32,
        jnp.tile(swa_q_norm, SWA_Q_HEADS).reshape(1, GROUP_WIDTH),
        jnp.tile(swa_k_norm, 2).reshape(1, 128),
        conv_w, out_norm_g[:3 * GROUP_WIDTH].reshape(1, 3 * GROUP_WIDTH),
        jnp.tile(diff_q_norm, 2 * DIFF_HEADS).reshape(1, GROUP_WIDTH),
        jnp.tile(diff_k_norm, 2 * DIFF_HEADS).reshape(1, GROUP_WIDTH))
    lamv = jnp.stack([lam_q1, lam_k1, lam_q2, lam_k2]).astype(F32)
    yd = _diff_attn(qdn, kdn, z, batch, seq, lamv, bd64,
                    out_norm_g[3 * GROUP_WIDTH:].reshape(1, GROUP_WIDTH), lam_init)
    w_out_bf16 = w_out.astype(BF16)
    x1, xn2, n1, w1, r2, w2 = _out_router(
        yabc, yd, x2d, w_out_bf16[:3 * GROUP_WIDTH], w_out_bf16[3 * GROUP_WIDTH:],
        norm2_g.reshape(1, D_MODEL), w_query.astype(BF16), sub_keys.astype(BF16))
    return _peer(x1, xn2, peer_u.astype(BF16), peer_v.T.astype(BF16), n1, w1, r2, w2)


def kernel(x, norm1_g, w_in, w_pool, swa_q_norm, swa_k_norm, swa_sinks, conv_w, diff_q_norm, diff_k_norm,
           lam_q1, lam_k1, lam_q2, lam_k2, out_norm_g, w_out, norm2_g, w_query, sub_keys, peer_u, peer_v):
    batch, seq, d = x.shape
    x2d = x.reshape(batch * seq, d)
    depth = norm1_g.shape[0]
    for l in range(depth):
        x2d = _layer(x2d, batch, seq, l, norm1_g[l], w_in[l], w_pool[l], swa_q_norm[l], swa_k_norm[l],
                     swa_sinks[l], conv_w[l], diff_q_norm[l], diff_k_norm[l], lam_q1[l], lam_k1[l],
                     lam_q2[l], lam_k2[l], out_norm_g[l], w_out[l], norm2_g[l], w_query[l], sub_keys[l],
                     peer_u[l], peer_v[l])
    return x2d.reshape(batch, seq, d)
```

```python
import functools

import jax
import jax.numpy as jnp
import numpy as np
from jax import lax
from jax.experimental import pallas as pl
from jax.experimental.pallas import tpu as pltpu

F32 = jnp.float32
BF16 = jnp.bfloat16

D_MODEL = 1024
EPS = 1e-6
HEAD_DIM = 64
GROUP_WIDTH = 256
POOL_WINDOWS = (2, 4, 8, 16)
SWA_Q_HEADS = 4
SWA_GQA = 2
SWA_WINDOW = 128
DIFF_HEADS = 4
DIFF_QK_DIM = 32
IN_PROJ_WIDTH = 2304
PEER_HEADS = 8
PEER_KEYS = 128
PEER_EXPERTS = PEER_KEYS * PEER_KEYS
PEER_HALF = 128
PEER_TOPK = 16
ALIBI_SLOPES = tuple(2.0 ** (-(i + 1)) for i in range(SWA_Q_HEADS + DIFF_HEADS))

NEG = -1e30
HALO = 16
ROW_BLOCK = 128
DIFF_Q_BLOCK = 256
IN_PROJ_ROWS = 512
ROUTER_ROWS = 512
ROUTER_LANES = 256
PEER_TOKENS = 512
PEER_EXPERT_CHUNK = 1024
VMEM_LIMIT = 56 * 1024 * 1024

NT_DIMS = (((1,), (1,)), ((), ()))


def _group_sumsq(x, bd):
    x2 = x * x
    hi = x2.astype(BF16)
    lo = (x2 - hi.astype(F32)).astype(BF16)
    return (jnp.dot(hi, bd, preferred_element_type=F32)
            + jnp.dot(lo, bd, preferred_element_type=F32))


def _group_rms_norm(x, bd, group, gain):
    return x * lax.rsqrt(_group_sumsq(x, bd) * (1.0 / group) + EPS) * gain


def _gelu(h):
    return 0.5 * h * (1.0 + lax.erf(h * np.float32(np.sqrt(0.5))))


def _in_proj_kernel(x_ref, g_ref, w_ref, z_ref):
    x = x_ref[...]
    xn = x * lax.rsqrt(jnp.mean(x * x, axis=-1, keepdims=True) + EPS) * g_ref[...]
    z = jnp.dot(xn.astype(BF16), w_ref[...], preferred_element_type=F32)
    z_ref[...] = z.astype(z_ref.dtype)


def _in_proj(x2d, g, w_bf16):
    t = x2d.shape[0]
    tm = min(IN_PROJ_ROWS, t)
    return pl.pallas_call(
        _in_proj_kernel,
        grid=(t // tm,),
        in_specs=[pl.BlockSpec((tm, D_MODEL), lambda i: (i, 0)),
                  pl.BlockSpec((1, D_MODEL), lambda i: (0, 0)),
                  pl.BlockSpec((D_MODEL, IN_PROJ_WIDTH), lambda i: (0, 0))],
        out_specs=pl.BlockSpec((tm, IN_PROJ_WIDTH), lambda i: (i, 0)),
        out_shape=jax.ShapeDtypeStruct((t, IN_PROJ_WIDTH), BF16),
        compiler_params=pltpu.CompilerParams(dimension_semantics=("parallel",),
                                             vmem_limit_bytes=VMEM_LIMIT),
        name="in_proj",
    )(x2d, g, w_bf16)


def _local_mix_kernel(sinks_ref, a_ref, a_halo_ref, bq_ref, bkv_ref, bkv_prev_ref, ch_ref, ch_halo_ref,
                      cb_ref, cc_ref, cc_halo_ref, dq_ref, dk_ref,
                      wpool_ref, bd64_ref, bd32_ref, gq_swa_ref, gk_swa_ref, conv_w_ref, g_out_ref,
                      gq_diff_ref, gk_diff_ref,
                      y_ref, qdn_ref, kdn_ref):
    n = pl.program_id(1)
    not_first = (n > 0).astype(F32)
    bd64 = bd64_ref[...]
    rows = lax.broadcasted_iota(jnp.int32, (ROW_BLOCK, GROUP_WIDTH), 0)
    cols = lax.broadcasted_iota(jnp.int32, (ROW_BLOCK, GROUP_WIDTH), 1)

    p = a_ref[...].astype(F32)
    p_ext = jnp.concatenate([a_halo_ref[...].astype(F32) * not_first, p], axis=0)
    s2 = p_ext + pltpu.roll(p_ext, 1, axis=0)
    s4 = s2 + pltpu.roll(s2, 2, axis=0)
    s8 = s4 + pltpu.roll(s4, 4, axis=0)
    s16 = s8 + pltpu.roll(s8, 8, axis=0)
    wsel = jnp.where(cols < 64, 2, jnp.where(cols < 128, 4, jnp.where(cols < 192, 8, 16)))
    ssel = jnp.where(cols < 64, s2[HALO:], jnp.where(cols < 128, s4[HALO:],
                                                      jnp.where(cols < 192, s8[HALO:], s16[HALO:])))
    cnt = jnp.minimum(n * ROW_BLOCK + rows + 1, wsel).astype(F32)
    d = ssel / cnt - p
    y_a = jnp.dot(d.astype(BF16), wpool_ref[...], preferred_element_type=F32)

    q = _group_rms_norm(bq_ref[...].astype(F32), bd64, HEAD_DIM, gq_swa_ref[...]) * (HEAD_DIM ** -0.5)
    qn = q.astype(BF16)
    bd64k = bd64[:128, :128]
    kv_cur = bkv_ref[...]
    kv_prev = bkv_prev_ref[...]
    k_cur = _group_rms_norm(kv_cur[:, :128].astype(F32), bd64k, HEAD_DIM, gk_swa_ref[...])
    k_prev = _group_rms_norm(kv_prev[:, :128].astype(F32), bd64k, HEAD_DIM, gk_swa_ref[...])
    kn = jnp.concatenate([k_prev, k_cur], axis=0).astype(BF16)
    v = jnp.concatenate([kv_prev[:, 128:], kv_cur[:, 128:]], axis=0)
    qi = lax.broadcasted_iota(jnp.int32, (ROW_BLOCK, 2 * ROW_BLOCK), 0)
    kj = lax.broadcasted_iota(jnp.int32, (ROW_BLOCK, 2 * ROW_BLOCK), 1)
    dist = qi + ROW_BLOCK - kj
    valid = (dist >= 0) & (dist < SWA_WINDOW) & ((n - 1) * ROW_BLOCK + kj >= 0)
    distf = dist.astype(F32)
    heads = []
    for h in range(SWA_Q_HEADS):
        hk = h // SWA_GQA
        sc = lax.dot_general(qn[:, h * 64:(h + 1) * 64], kn[:, hk * 64:(hk + 1) * 64], NT_DIMS,
                             preferred_element_type=F32)
        sc = jnp.where(valid, sc - ALIBI_SLOPES[h] * distf, NEG)
        sink = sinks_ref[h]
        mx = jnp.maximum(jnp.max(sc, axis=-1, keepdims=True), sink)
        e = jnp.exp(sc - mx)
        den = jnp.sum(e, axis=-1, keepdims=True) + jnp.exp(sink - mx)
        o = jnp.dot(e.astype(BF16), v[:, hk * 64:(hk + 1) * 64], preferred_element_type=F32)
        heads.append(o / den)
    y_b = jnp.concatenate(heads, axis=-1)

    zc = cc_ref[...].astype(F32) * ch_ref[...].astype(F32)
    zc_halo = cc_halo_ref[...].astype(F32) * ch_halo_ref[...].astype(F32) * not_first
    zc_ext = jnp.concatenate([zc_halo, zc], axis=0)
    cw = conv_w_ref[...]
    conv = (cw[0:1] * pltpu.roll(zc_ext, 2, axis=0)[HALO:] + cw[1:2] * pltpu.roll(zc_ext, 1, axis=0)[HALO:]
            + cw[2:3] * zc)
    y_c = cb_ref[...].astype(F32) * conv

    g_out = g_out_ref[...]
    for gi, y in enumerate((y_a, y_b, y_c)):
        lo, hi = gi * GROUP_WIDTH, (gi + 1) * GROUP_WIDTH
        y_ref[:, lo:hi] = _group_rms_norm(y, bd64, HEAD_DIM, g_out[:, lo:hi]).astype(y_ref.dtype)

    bd32 = bd32_ref[...]
    qd = _group_rms_norm(dq_ref[...].astype(F32), bd32, DIFF_QK_DIM, gq_diff_ref[...]) * (DIFF_QK_DIM ** -0.5)
    kd = _group_rms_norm(dk_ref[...].astype(F32), bd32, DIFF_QK_DIM, gk_diff_ref[...])
    qdn_ref[...] = qd.astype(qdn_ref.dtype)
    kdn_ref[...] = kd.astype(kdn_ref.dtype)


def _local_mix(z, batch, seq, sinks, wpool_bd, bd64, bd32, gq_swa, gk_swa, conv_w, g_out, gq_diff, gk_diff):
    t = batch * seq
    nb = seq // ROW_BLOCK
    halo_per_block = ROW_BLOCK // HALO

    def cur(col):
        return pl.BlockSpec((ROW_BLOCK, GROUP_WIDTH), lambda b, n: (b * nb + n, col))

    def prev(col):
        return pl.BlockSpec((ROW_BLOCK, GROUP_WIDTH), lambda b, n: (b * nb + jnp.maximum(n - 1, 0), col))

    def halo(col):
        return pl.BlockSpec(
            (HALO, GROUP_WIDTH),
            lambda b, n: (jnp.maximum((b * nb + n) * halo_per_block - 1, 0), col))

    def whole(arr):
        return pl.BlockSpec(arr.shape, lambda b, n: (0,) * arr.ndim)

    params = (wpool_bd, bd64, bd32, gq_swa, gk_swa, conv_w, g_out, gq_diff, gk_diff)
    in_specs = ([pl.BlockSpec(memory_space=pltpu.SMEM),
                 cur(0), halo(0), cur(1), cur(2), prev(2), cur(3), halo(3), cur(4), cur(5), halo(5),
                 cur(6), cur(7)] + [whole(a) for a in params])
    out_specs = [pl.BlockSpec((ROW_BLOCK, 3 * GROUP_WIDTH), lambda b, n: (b * nb + n, 0)),
                 pl.BlockSpec((ROW_BLOCK, GROUP_WIDTH), lambda b, n: (b * nb + n, 0)),
                 pl.BlockSpec((ROW_BLOCK, GROUP_WIDTH), lambda b, n: (b * nb + n, 0))]
    out_shape = [jax.ShapeDtypeStruct((t, 3 * GROUP_WIDTH), BF16),
                 jax.ShapeDtypeStruct((t, GROUP_WIDTH), BF16),
                 jax.ShapeDtypeStruct((t, GROUP_WIDTH), BF16)]
    return pl.pallas_call(
        _local_mix_kernel,
        grid=(batch, nb),
        in_specs=in_specs,
        out_specs=out_specs,
        out_shape=out_shape,
        compiler_params=pltpu.CompilerParams(dimension_semantics=("parallel", "parallel"),
                                             vmem_limit_bytes=VMEM_LIMIT),
        name="local_mix",
    )(sinks, *([z] * 12), *params)


def _diff_attn_kernel(q_ref, k_ref, v_ref, lamv_ref, bd64_ref, g_ref, y_ref,
                      qs_ref, vt_ref, p_ref, m_ref, den_ref, alpha_ref, acc_ref, *, one_minus_lam_init, lam_init):
    qi = pl.program_id(1)
    qb = DIFF_Q_BLOCK
    lamv = lamv_ref[...]
    lam = (jnp.exp(jnp.sum(lamv[0:1] * lamv[1:2], axis=-1, keepdims=True))
           - jnp.exp(jnp.sum(lamv[2:3] * lamv[3:4], axis=-1, keepdims=True)) + lam_init)
    n_maps = 2 * DIFF_HEADS

    @pl.when(qi == 0)
    def _():
        for c in range(vt_ref.shape[0]):
            vt_ref[c] = v_ref[c * qb:(c + 1) * qb, :].astype(F32).T.astype(BF16)

    q = q_ref[...]
    lane_group = lax.broadcasted_iota(jnp.int32, (qb, GROUP_WIDTH), 1) // DIFF_QK_DIM
    for g in range(n_maps):
        qs_ref[g * qb:(g + 1) * qb, :] = jnp.where(lane_group == g, q, jnp.zeros_like(q))
    m_ref[...] = jnp.full(m_ref.shape, NEG, F32)
    den_ref[...] = jnp.zeros(den_ref.shape, F32)
    acc_ref[...] = jnp.zeros(acc_ref.shape, F32)
    rel = (lax.broadcasted_iota(jnp.int32, (qb, qb), 1) - lax.broadcasted_iota(jnp.int32, (qb, qb), 0))
    relf = rel.astype(F32)

    def kv_step(c, diagonal):
        start = pl.multiple_of(c * qb, qb)
        sc_all = lax.dot_general(k_ref[pl.ds(start, qb), :], qs_ref[...], NT_DIMS,
                                 preferred_element_type=F32)
        off = ((qi - c) * qb).astype(F32)
        for g in range(n_maps):
            cols = slice(g * qb, (g + 1) * qb)
            slope = ALIBI_SLOPES[SWA_Q_HEADS + g // 2]
            sc = sc_all[:, cols] - slope * relf
            if diagonal:
                sc = jnp.where(rel >= 0, sc, NEG)
            shift = slope * off
            m_old = m_ref[:, cols]
            m_new = jnp.maximum(m_old, jnp.max(sc, axis=0, keepdims=True) - shift)
            alpha = jnp.exp(m_old - m_new)
            e = jnp.exp(sc - (m_new + shift))
            den_ref[:, cols] = alpha * den_ref[:, cols] + jnp.sum(e, axis=0, keepdims=True)
            m_ref[:, cols] = m_new
            alpha_ref[:, cols] = alpha
            p_ref[:, cols] = e.astype(BF16)
        pv = jnp.dot(vt_ref[c], p_ref[...], preferred_element_type=F32)
        for g in range(n_maps):
            cols = slice(g * qb, (g + 1) * qb)
            rows = slice((g // 2) * HEAD_DIM, (g // 2 + 1) * HEAD_DIM)
            acc_ref[:, cols] = alpha_ref[:, cols] * acc_ref[:, cols] + pv[rows, cols]

    def off_diagonal(c, carry):
        kv_step(c, False)
        return carry

    lax.fori_loop(0, qi, off_diagonal, 0)
    kv_step(qi, True)

    heads = []
    for h in range(DIFF_HEADS):
        c1 = slice(2 * h * qb, (2 * h + 1) * qb)
        c2 = slice((2 * h + 1) * qb, (2 * h + 2) * qb)
        heads.append(acc_ref[:, c1] / den_ref[:, c1] - lam * (acc_ref[:, c2] / den_ref[:, c2]))
    o = jnp.concatenate(heads, axis=0).T
    y = _group_rms_norm(o, bd64_ref[...], HEAD_DIM, g_ref[...]) * one_minus_lam_init
    y_ref[...] = y.astype(y_ref.dtype)


def _diff_attn(qdn, kdn, z, batch, seq, lamv, bd64, g_d, lam_init):
    t = batch * seq
    nq = seq // DIFF_Q_BLOCK
    v_col = 8
    stacked = 2 * DIFF_HEADS * DIFF_Q_BLOCK
    kernel = functools.partial(_diff_attn_kernel, one_minus_lam_init=1.0 - lam_init, lam_init=lam_init)
    return pl.pallas_call(
        kernel,
        grid=(batch, nq),
        in_specs=[pl.BlockSpec((DIFF_Q_BLOCK, GROUP_WIDTH), lambda b, i: (b * nq + i, 0)),
                  pl.BlockSpec((seq, GROUP_WIDTH), lambda b, i: (b, 0)),
                  pl.BlockSpec((seq, GROUP_WIDTH), lambda b, i: (b, v_col)),
                  pl.BlockSpec(lamv.shape, lambda b, i: (0, 0)),
                  pl.BlockSpec(bd64.shape, lambda b, i: (0, 0)),
                  pl.BlockSpec(g_d.shape, lambda b, i: (0, 0))],
        out_specs=pl.BlockSpec((DIFF_Q_BLOCK, GROUP_WIDTH), lambda b, i: (b * nq + i, 0)),
        out_shape=jax.ShapeDtypeStruct((t, GROUP_WIDTH), BF16),
        scratch_shapes=[pltpu.VMEM((stacked, GROUP_WIDTH), BF16),
                        pltpu.VMEM((seq // DIFF_Q_BLOCK, GROUP_WIDTH, DIFF_Q_BLOCK), BF16),
                        pltpu.VMEM((DIFF_Q_BLOCK, stacked), BF16),
                        pltpu.VMEM((1, stacked), F32),
                        pltpu.VMEM((1, stacked), F32),
                        pltpu.VMEM((1, stacked), F32),
                        pltpu.VMEM((HEAD_DIM, stacked), F32)],
        compiler_params=pltpu.CompilerParams(dimension_semantics=("parallel", "arbitrary"),
                                             vmem_limit_bytes=VMEM_LIMIT),
        name="diff_attn",
    )(qdn, kdn, z, lamv, bd64, g_d)


def _top_values(s, count):
    vals = []
    cur = s
    for r in range(count):
        m = jnp.max(cur, axis=0, keepdims=True)
        vals.append(m)
        if r + 1 < count:
            cur = jnp.where(cur >= m, NEG, cur)
    return vals


def _odd_even_merge_sort_pairs(n):
    pairs = []

    def merge(lo, length, r):
        step = 2 * r
        if step < length:
            merge(lo, length, step)
            merge(lo + r, length, step)
            pairs.extend((i, i + r) for i in range(lo + r, lo + length - r, step))
        else:
            pairs.append((lo, lo + r))

    def sort(lo, length):
        if length > 1:
            sort(lo, length // 2)
            sort(lo + length // 2, length // 2)
            merge(lo, length, 1)

    sort(0, n)
    return pairs


_SORT16 = _odd_even_merge_sort_pairs(PEER_TOPK)


def _top16_sorted(s):
    n = PEER_TOPK
    a = [s[8 * j:8 * j + 8] for j in range(n)]
    for i, j in _SORT16:
        a[i], a[j] = jnp.maximum(a[i], a[j]), jnp.minimum(a[i], a[j])
    for shift in (4, 2, 1):
        b = [pltpu.roll(x, shift, axis=0) for x in a]
        a = [jnp.maximum(a[j], b[n - 1 - j]) for j in range(n)]
        d = n // 2
        while d:
            for i in range(n):
                if not i & d:
                    a[i], a[i + d] = jnp.maximum(a[i], a[i + d]), jnp.minimum(a[i], a[i + d])
            d //= 2
    return a


def _sublane_pick(arrays):
    row = lax.broadcasted_iota(jnp.int32, arrays[0].shape, 0)
    out = arrays[0]
    for s in range(1, 8):
        out = jnp.where(row == s, arrays[s], out)
    return out


def _dup_bf16_words(x):
    bits = pltpu.bitcast(x.astype(BF16).astype(F32), jnp.uint32)
    return bits | (bits >> 16)


def _stack_rows(rows_list):
    n = len(rows_list)
    lanes = rows_list[0].shape[-1]
    ri = lax.broadcasted_iota(jnp.int32, (n, lanes), 0)
    out = jnp.zeros((n, lanes), F32)
    for r, row in enumerate(rows_list):
        out = jnp.where(ri == r, row, out)
    return out


def _out_router_kernel(yabc_ref, yd_ref, x_ref, wo1_ref, wo2_ref, g2_ref, wq_ref, keys_ref,
                       x1_ref, xn2_ref, n1_ref, w1_ref, r2_ref, w2_ref):
    acc = (jnp.dot(yabc_ref[...], wo1_ref[...], preferred_element_type=F32)
           + jnp.dot(yd_ref[...], wo2_ref[...], preferred_element_type=F32))
    x1 = x_ref[...] + acc
    x1_ref[...] = x1
    xn2 = (x1 * lax.rsqrt(jnp.mean(x1 * x1, axis=-1, keepdims=True) + EPS) * g2_ref[...]).astype(BF16)
    xn2_ref[...] = pltpu.bitcast(xn2, jnp.uint32)
    q = jnp.dot(xn2, wq_ref[...], preferred_element_type=F32)
    tm = q.shape[0]
    for h in range(PEER_HEADS):
        scores = []
        for c in range(2):
            lo = (2 * h + c) * PEER_HALF
            qhc = q[:, lo:lo + PEER_HALF].astype(BF16)
            scores.append(lax.dot_general(keys_ref[c], qhc, NT_DIMS, preferred_element_type=F32))
        for p in range(tm // ROUTER_LANES):
            ls = slice(p * ROUTER_LANES, (p + 1) * ROUTER_LANES)
            _route_head(scores[0][:, ls], scores[1][:, ls],
                        n1_ref.at[h, :, ls], w1_ref.at[h, :, ls], r2_ref.at[h, :, ls], w2_ref.at[h, :, ls])


def _route_head(s1, s2, n1_ref, w1_ref, r2_ref, w2_ref):
    lanes = s1.shape[1]
    row8 = lax.broadcasted_iota(jnp.int32, (8, lanes), 0)
    t1 = _top16_sorted(s1)
    t2 = _top16_sorted(s2)
    v1_lo, v1_hi, v2_hi = _sublane_pick(t1[:8]), _sublane_pick(t1[8:]), _sublane_pick(t2[8:])
    groups = [v1_lo + t2[0], v1_hi + t2[0]]
    for b in range(1, 8):
        groups.append(jnp.where(row8 < PEER_TOPK // (b + 1), v1_lo + t2[b], NEG))
    groups.append(t1[0] + v2_hi)
    best = _top_values(jnp.concatenate(groups, axis=0), PEER_TOPK)
    tau = jnp.broadcast_to(best[PEER_TOPK - 1], (8, lanes))
    zsum = jnp.sum(jnp.exp(_stack_rows(best) - best[0]), axis=0, keepdims=True)
    inv_z = jnp.broadcast_to(1.0 / zsum, (8, lanes))
    for j in range(PEER_KEYS // 16):
        words = slice(8 * j, 8 * j + 8)
        rank2, w2 = [], []
        for half in range(2):
            ks = slice(16 * j + 8 * half, 16 * j + 8 * half + 8)
            s1k, s2k = s1[ks], s2[ks]
            n1 = jnp.zeros((8, lanes), F32)
            rk = jnp.full((8, lanes), float(PEER_TOPK), F32)
            for b in reversed(range(PEER_TOPK)):
                n1 = n1 + jnp.where(s1k + t2[b] >= tau, 1.0, 0.0)
                rk = jnp.where(s2k >= t2[b], float(b), rk)
            n1_ref[ks, :] = _dup_bf16_words(n1)
            w1_ref[ks, :] = _dup_bf16_words(jnp.exp(s1k - t1[0]) * inv_z)
            rank2.append(rk)
            w2.append(jnp.exp(s2k - t2[0]))
        r2_ref[words, :] = pltpu.bitcast(jnp.concatenate(rank2, axis=0).astype(BF16), jnp.uint32)
        w2_ref[words, :] = pltpu.bitcast(jnp.concatenate(w2, axis=0).astype(BF16), jnp.uint32)


def _out_router(yabc, yd, x2d, wo1, wo2, g2, wq, keys):
    t = x2d.shape[0]
    tm = min(ROUTER_ROWS, t)

    def whole(arr):
        return pl.BlockSpec(arr.shape, lambda i: (0,) * arr.ndim)

    row_spec = pl.BlockSpec((PEER_HEADS, PEER_KEYS, tm), lambda i: (0, 0, i))
    row_shape = jax.ShapeDtypeStruct((PEER_HEADS, PEER_KEYS, t), jnp.uint32)
    tile_spec = pl.BlockSpec((PEER_HEADS, PEER_KEYS // 2, tm), lambda i: (0, 0, i))
    tile_shape = jax.ShapeDtypeStruct((PEER_HEADS, PEER_KEYS // 2, t), jnp.uint32)
    return pl.pallas_call(
        _out_router_kernel,
        grid=(t // tm,),
        in_specs=[pl.BlockSpec((tm, 3 * GROUP_WIDTH), lambda i: (i, 0)),
                  pl.BlockSpec((tm, GROUP_WIDTH), lambda i: (i, 0)),
                  pl.BlockSpec((tm, D_MODEL), lambda i: (i, 0)),
                  whole(wo1), whole(wo2), whole(g2), whole(wq), whole(keys)],
        out_specs=[pl.BlockSpec((tm, D_MODEL), lambda i: (i, 0)),
                   pl.BlockSpec((tm // 2, D_MODEL), lambda i: (i, 0)),
                   row_spec, row_spec, tile_spec, tile_spec],
        out_shape=[jax.ShapeDtypeStruct((t, D_MODEL), F32),
                   jax.ShapeDtypeStruct((t // 2, D_MODEL), jnp.uint32),
                   row_shape, row_shape, tile_shape, tile_shape],
        compiler_params=pltpu.CompilerParams(dimension_semantics=("parallel",),
                                             vmem_limit_bytes=VMEM_LIMIT),
        name="out_router",
    )(yabc, yd, x2d, wo1, wo2, g2, wq, keys)


def _apply_gates(a_ref, n1_ref, w1_ref, r2_ref, w2_ref):
    tb = a_ref.shape[1]
    rows = n1_ref.shape[1]
    zero = jnp.zeros((), BF16)
    half = PEER_KEYS // 2
    for lc in range(tb // 128):
        ls = slice(lc * 128, (lc + 1) * 128)
        for r in range(rows):
            w = None
            for hd in range(PEER_HEADS):
                n1b = pltpu.bitcast(jnp.broadcast_to(n1_ref[hd, r:r + 1, ls], (half, 128)), BF16)
                w1b = pltpu.bitcast(jnp.broadcast_to(w1_ref[hd, r:r + 1, ls], (half, 128)), BF16)
                rank2 = pltpu.bitcast(r2_ref[hd, :, ls], BF16)
                w2 = pltpu.bitcast(w2_ref[hd, :, ls], BF16)
                term = jnp.where(rank2 < n1b, w2 * w1b, zero)
                w = term if w is None else w + term
            ws = slice(r * half, (r + 1) * half)
            a_ref[ws, ls] = pltpu.bitcast(w * pltpu.bitcast(a_ref[ws, ls], BF16), jnp.uint32)


def _peer_kernel(x_ref, xn_ref, u_ref, vt_ref, n1_ref, w1_ref, r2_ref, w2_ref, o_ref, acc_ref, a_ref):
    j = pl.program_id(1)

    @pl.when(j == 0)
    def _():
        acc_ref[...] = jnp.zeros_like(acc_ref)

    h = lax.dot_general(u_ref[...], pltpu.bitcast(xn_ref[...], BF16), NT_DIMS, preferred_element_type=F32)
    a_ref[...] = pltpu.bitcast(_gelu(h).astype(BF16), jnp.uint32)
    _apply_gates(a_ref, n1_ref, w1_ref, r2_ref, w2_ref)
    acc_ref[...] += jnp.dot(vt_ref[...], pltpu.bitcast(a_ref[...], BF16), preferred_element_type=F32)

    @pl.when(j == pl.num_programs(1) - 1)
    def _():
        o_ref[...] = x_ref[...] + acc_ref[...].T


def _peer(x1, xn2, u_bf16, vt_bf16, n1, w1, r2, w2):
    t = x1.shape[0]
    tb = min(PEER_TOKENS, t)
    ec = PEER_EXPERT_CHUNK
    rows = ec // PEER_KEYS
    assert rows % 8 == 0 and tb % 128 == 0 and t % tb == 0 and PEER_EXPERTS % ec == 0
    row_spec = pl.BlockSpec((PEER_HEADS, rows, tb), lambda i, j: (0, j, i))
    tile_spec = pl.BlockSpec((PEER_HEADS, PEER_KEYS // 2, tb), lambda i, j: (0, 0, i))
    kernel = _peer_kernel
    return pl.pallas_call(
        kernel,
        grid=(t // tb, PEER_EXPERTS // ec),
        in_specs=[pl.BlockSpec((tb, D_MODEL), lambda i, j: (i, 0)),
                  pl.BlockSpec((tb // 2, D_MODEL), lambda i, j: (i, 0)),
                  pl.BlockSpec((ec, D_MODEL), lambda i, j: (j, 0)),
                  pl.BlockSpec((D_MODEL, ec), lambda i, j: (0, j)),
                  row_spec, row_spec, tile_spec, tile_spec],
        out_specs=pl.BlockSpec((tb, D_MODEL), lambda i, j: (i, 0)),
        out_shape=jax.ShapeDtypeStruct((t, D_MODEL), F32),
        scratch_shapes=[pltpu.VMEM((D_MODEL, tb), F32),
                        pltpu.VMEM((ec // 2, tb), jnp.uint32)],
        compiler_params=pltpu.CompilerParams(dimension_semantics=("parallel", "arbitrary"),
                                             vmem_limit_bytes=VMEM_LIMIT),
        name="peer",
    )(x1, xn2, u_bf16, vt_bf16, n1, w1, r2, w2)


def _block_diag_ones(width, group):
    idx = np.arange(width) // group
    return jnp.asarray((idx[:, None] == idx[None, :]).astype(np.float32), BF16)


def _layer(x2d, batch, seq, layer_idx, norm1_g, w_in, w_pool, swa_q_norm, swa_k_norm, swa_sinks, conv_w,
           diff_q_norm, diff_k_norm, lam_q1, lam_k1, lam_q2, lam_k2, out_norm_g, w_out, norm2_g, w_query,
           sub_keys, peer_u, peer_v):
    lam_init = 0.8 - 0.6 * float(np.exp(-0.3 * layer_idx))
    bd64 = _block_diag_ones(GROUP_WIDTH, HEAD_DIM)
    bd32 = _block_diag_ones(GROUP_WIDTH, DIFF_QK_DIM)
    wpool_bd = jax.scipy.linalg.block_diag(*[w_pool[g] for g in range(len(POOL_WINDOWS))]).astype(BF16)

    z = _in_proj(x2d, norm1_g.reshape(1, D_MODEL), w_in.astype(BF16))
    yabc, qdn, kdn = _local_mix(
        z, batch, seq, swa_sinks.astype(F32), wpool_bd, bd64, bd32,
        jnp.tile(swa_q_norm, SWA_Q_HEADS).reshape(1, GROUP_WIDTH),
        jnp.tile(swa_k_norm, 2).reshape(1, 128),
        conv_w, out_norm_g[:3 * GROUP_WIDTH].reshape(1, 3 * GROUP_WIDTH),
        jnp.tile(diff_q_norm, 2 * DIFF_HEADS).reshape(1, GROUP_WIDTH),
        jnp.tile(diff_k_norm, 2 * DIFF_HEADS).reshape(1, GROUP_WIDTH))
    lamv = jnp.stack([lam_q1, lam_k1, lam_q2, lam_k2]).astype(F32)
    yd = _diff_attn(qdn, kdn, z, batch, seq, lamv, bd64,
                    out_norm_g[3 * GROUP_WIDTH:].reshape(1, GROUP_WIDTH), lam_init)
    w_out_bf16 = w_out.astype(BF16)
    x1, xn2, n1, w1, r2, w2 = _out_router(
        yabc, yd, x2d, w_out_bf16[:3 * GROUP_WIDTH], w_out_bf16[3 * GROUP_WIDTH:],
        norm2_g.reshape(1, D_MODEL), w_query.astype(BF16), sub_keys.astype(BF16))
    return _peer(x1, xn2, peer_u.astype(BF16), peer_v.T.astype(BF16), n1, w1, r2, w2)


def kernel(x, norm1_g, w_in, w_pool, swa_q_norm, swa_k_norm, swa_sinks, conv_w, diff_q_norm, diff_k_norm,
           lam_q1, lam_k1, lam_q2, lam_k2, out_norm_g, w_out, norm2_g, w_query, sub_keys, peer_u, peer_v):
    batch, seq, d = x.shape
    x2d = x.reshape(batch * seq, d)
    depth = norm1_g.shape[0]
    for l in range(depth):
        x2d = _layer(x2d, batch, seq, l, norm1_g[l], w_in[l], w_pool[l], swa_q_norm[l], swa_k_norm[l],
                     swa_sinks[l], conv_w[l], diff_q_norm[l], diff_k_norm[l], lam_q1[l], lam_k1[l],
                     lam_q2[l], lam_k2[l], out_norm_g[l], w_out[l], norm2_g[l], w_query[l], sub_keys[l],
                     peer_u[l], peer_v[l])
    return x2d.reshape(batch, seq, d)
```

```python
import functools

import jax
import jax.numpy as jnp
import numpy as np
from jax import lax
from jax.experimental import pallas as pl
from jax.experimental.pallas import tpu as pltpu

F32 = jnp.float32
BF16 = jnp.bfloat16

D_MODEL = 1024
EPS = 1e-6
HEAD_DIM = 64
GROUP_WIDTH = 256
POOL_WINDOWS = (2, 4, 8, 16)
SWA_Q_HEADS = 4
SWA_GQA = 2
SWA_WINDOW = 128
DIFF_HEADS = 4
DIFF_QK_DIM = 32
IN_PROJ_WIDTH = 2304
PEER_HEADS = 8
PEER_KEYS = 128
PEER_EXPERTS = PEER_KEYS * PEER_KEYS
PEER_HALF = 128
PEER_TOPK = 16
ALIBI_SLOPES = tuple(2.0 ** (-(i + 1)) for i in range(SWA_Q_HEADS + DIFF_HEADS))

NEG = -1e30
LOG2_E = 1.4426950408889634
HALO = 16
ROW_BLOCK = 128
DIFF_Q_BLOCK = 256
IN_PROJ_ROWS = 512
ROUTER_ROWS = 512
ROUTER_LANES = 256
PEER_TOKENS = 512
PEER_EXPERT_CHUNK = 2048
VMEM_LIMIT = 56 * 1024 * 1024

NT_DIMS = (((1,), (1,)), ((), ()))


def _group_sumsq(x, bd):
    x2 = x * x
    hi = x2.astype(BF16)
    lo = (x2 - hi.astype(F32)).astype(BF16)
    return (jnp.dot(hi, bd, preferred_element_type=F32)
            + jnp.dot(lo, bd, preferred_element_type=F32))


def _group_rms_norm(x, bd, group, gain):
    return x * lax.rsqrt(_group_sumsq(x, bd) * (1.0 / group) + EPS) * gain


def _gelu(h):
    return 0.5 * h * (1.0 + lax.erf(h * np.float32(np.sqrt(0.5))))


def _in_proj_kernel(x_ref, g_ref, w_ref, z_ref):
    x = x_ref[...]
    xn = x * lax.rsqrt(jnp.mean(x * x, axis=-1, keepdims=True) + EPS) * g_ref[...]
    z = jnp.dot(xn.astype(BF16), w_ref[...], preferred_element_type=F32)
    z_ref[...] = z.astype(z_ref.dtype)


def _in_proj(x2d, g, w_bf16):
    t = x2d.shape[0]
    tm = min(IN_PROJ_ROWS, t)
    return pl.pallas_call(
        _in_proj_kernel,
        grid=(t // tm,),
        in_specs=[pl.BlockSpec((tm, D_MODEL), lambda i: (i, 0)),
                  pl.BlockSpec((1, D_MODEL), lambda i: (0, 0)),
                  pl.BlockSpec((D_MODEL, IN_PROJ_WIDTH), lambda i: (0, 0))],
        out_specs=pl.BlockSpec((tm, IN_PROJ_WIDTH), lambda i: (i, 0)),
        out_shape=jax.ShapeDtypeStruct((t, IN_PROJ_WIDTH), BF16),
        compiler_params=pltpu.CompilerParams(dimension_semantics=("parallel",),
                                             vmem_limit_bytes=VMEM_LIMIT),
        name="in_proj",
    )(x2d, g, w_bf16)


def _local_mix_kernel(sinks_ref, a_ref, a_halo_ref, bq_ref, bkv_ref, bkv_prev_ref, ch_ref, ch_halo_ref,
                      cb_ref, cc_ref, cc_halo_ref, dq_ref, dk_ref,
                      wpool_ref, bd64_ref, bd32_ref, gq_swa_ref, gk_swa_ref, conv_w_ref, g_out_ref,
                      gq_diff_ref, gk_diff_ref,
                      y_ref, qdn_ref, kdn_ref):
    n = pl.program_id(1)
    not_first = (n > 0).astype(F32)
    bd64 = bd64_ref[...]
    rows = lax.broadcasted_iota(jnp.int32, (ROW_BLOCK, GROUP_WIDTH), 0)
    cols = lax.broadcasted_iota(jnp.int32, (ROW_BLOCK, GROUP_WIDTH), 1)

    p = a_ref[...].astype(F32)
    p_ext = jnp.concatenate([a_halo_ref[...].astype(F32) * not_first, p], axis=0)
    s2 = p_ext + pltpu.roll(p_ext, 1, axis=0)
    s4 = s2 + pltpu.roll(s2, 2, axis=0)
    s8 = s4 + pltpu.roll(s4, 4, axis=0)
    s16 = s8 + pltpu.roll(s8, 8, axis=0)
    wsel = jnp.where(cols < 64, 2, jnp.where(cols < 128, 4, jnp.where(cols < 192, 8, 16)))
    ssel = jnp.where(cols < 64, s2[HALO:], jnp.where(cols < 128, s4[HALO:],
                                                      jnp.where(cols < 192, s8[HALO:], s16[HALO:])))
    cnt = jnp.minimum(n * ROW_BLOCK + rows + 1, wsel).astype(F32)
    d = ssel / cnt - p
    y_a = jnp.dot(d.astype(BF16), wpool_ref[...], preferred_element_type=F32)

    q = _group_rms_norm(bq_ref[...].astype(F32), bd64, HEAD_DIM, gq_swa_ref[...]) * (HEAD_DIM ** -0.5)
    qn = q.astype(BF16)
    bd64k = bd64[:128, :128]
    kv_cur = bkv_ref[...]
    kv_prev = bkv_prev_ref[...]
    k_cur = _group_rms_norm(kv_cur[:, :128].astype(F32), bd64k, HEAD_DIM, gk_swa_ref[...])
    k_prev = _group_rms_norm(kv_prev[:, :128].astype(F32), bd64k, HEAD_DIM, gk_swa_ref[...])
    kn = jnp.concatenate([k_prev, k_cur], axis=0).astype(BF16)
    v = jnp.concatenate([kv_prev[:, 128:], kv_cur[:, 128:]], axis=0)
    qi = lax.broadcasted_iota(jnp.int32, (ROW_BLOCK, 2 * ROW_BLOCK), 0)
    kj = lax.broadcasted_iota(jnp.int32, (ROW_BLOCK, 2 * ROW_BLOCK), 1)
    dist = qi + ROW_BLOCK - kj
    valid = (dist >= 0) & (dist < SWA_WINDOW) & ((n - 1) * ROW_BLOCK + kj >= 0)
    distf = dist.astype(F32)
    heads = []
    for h in range(SWA_Q_HEADS):
        hk = h // SWA_GQA
        sc = lax.dot_general(qn[:, h * 64:(h + 1) * 64], kn[:, hk * 64:(hk + 1) * 64], NT_DIMS,
                             preferred_element_type=F32)
        sc = jnp.where(valid, sc - ALIBI_SLOPES[h] * distf, NEG)
        sink = sinks_ref[h]
        mx = jnp.maximum(jnp.max(sc, axis=-1, keepdims=True), sink)
        e = jnp.exp(sc - mx)
        den = jnp.sum(e, axis=-1, keepdims=True) + jnp.exp(sink - mx)
        o = jnp.dot(e.astype(BF16), v[:, hk * 64:(hk + 1) * 64], preferred_element_type=F32)
        heads.append(o / den)
    y_b = jnp.concatenate(heads, axis=-1)

    zc = cc_ref[...].astype(F32) * ch_ref[...].astype(F32)
    zc_halo = cc_halo_ref[...].astype(F32) * ch_halo_ref[...].astype(F32) * not_first
    zc_ext = jnp.concatenate([zc_halo, zc], axis=0)
    cw = conv_w_ref[...]
    conv = (cw[0:1] * pltpu.roll(zc_ext, 2, axis=0)[HALO:] + cw[1:2] * pltpu.roll(zc_ext, 1, axis=0)[HALO:]
            + cw[2:3] * zc)
    y_c = cb_ref[...].astype(F32) * conv

    g_out = g_out_ref[...]
    for gi, y in enumerate((y_a, y_b, y_c)):
        lo, hi = gi * GROUP_WIDTH, (gi + 1) * GROUP_WIDTH
        y_ref[:, lo:hi] = _group_rms_norm(y, bd64, HEAD_DIM, g_out[:, lo:hi]).astype(y_ref.dtype)

    bd32 = bd32_ref[...]
    qd = (_group_rms_norm(dq_ref[...].astype(F32), bd32, DIFF_QK_DIM, gq_diff_ref[...])
          * (DIFF_QK_DIM ** -0.5 * LOG2_E))
    kd = _group_rms_norm(dk_ref[...].astype(F32), bd32, DIFF_QK_DIM, gk_diff_ref[...])
    qdn_ref[...] = qd.astype(qdn_ref.dtype)
    kdn_ref[...] = kd.astype(kdn_ref.dtype)


def _local_mix(z, batch, seq, sinks, wpool_bd, bd64, bd32, gq_swa, gk_swa, conv_w, g_out, gq_diff, gk_diff):
    t = batch * seq
    nb = seq // ROW_BLOCK
    halo_per_block = ROW_BLOCK // HALO

    def cur(col):
        return pl.BlockSpec((ROW_BLOCK, GROUP_WIDTH), lambda b, n: (b * nb + n, col))

    def prev(col):
        return pl.BlockSpec((ROW_BLOCK, GROUP_WIDTH), lambda b, n: (b * nb + jnp.maximum(n - 1, 0), col))

    def halo(col):
        return pl.BlockSpec(
            (HALO, GROUP_WIDTH),
            lambda b, n: (jnp.maximum((b * nb + n) * halo_per_block - 1, 0), col))

    def whole(arr):
        return pl.BlockSpec(arr.shape, lambda b, n: (0,) * arr.ndim)

    params = (wpool_bd, bd64, bd32, gq_swa, gk_swa, conv_w, g_out, gq_diff, gk_diff)
    in_specs = ([pl.BlockSpec(memory_space=pltpu.SMEM),
                 cur(0), halo(0), cur(1), cur(2), prev(2), cur(3), halo(3), cur(4), cur(5), halo(5),
                 cur(6), cur(7)] + [whole(a) for a in params])
    out_specs = [pl.BlockSpec((ROW_BLOCK, 3 * GROUP_WIDTH), lambda b, n: (b * nb + n, 0)),
                 pl.BlockSpec((ROW_BLOCK, GROUP_WIDTH), lambda b, n: (b * nb + n, 0)),
                 pl.BlockSpec((ROW_BLOCK, GROUP_WIDTH), lambda b, n: (b * nb + n, 0))]
    out_shape = [jax.ShapeDtypeStruct((t, 3 * GROUP_WIDTH), BF16),
                 jax.ShapeDtypeStruct((t, GROUP_WIDTH), BF16),
                 jax.ShapeDtypeStruct((t, GROUP_WIDTH), BF16)]
    return pl.pallas_call(
        _local_mix_kernel,
        grid=(batch, nb),
        in_specs=in_specs,
        out_specs=out_specs,
        out_shape=out_shape,
        compiler_params=pltpu.CompilerParams(dimension_semantics=("parallel", "parallel"),
                                             vmem_limit_bytes=VMEM_LIMIT),
        name="local_mix",
    )(sinks, *([z] * 12), *params)


def _diff_attn_kernel(q_ref, k_ref, v_ref, lamv_ref, bd64_ref, g_ref, y_ref,
                      qs_ref, vt_ref, p_ref, m_ref, den_ref, alpha_ref, acc_ref, *, one_minus_lam_init, lam_init):
    qi = pl.program_id(1)
    qb = DIFF_Q_BLOCK
    lamv = lamv_ref[...]
    lam = (jnp.exp(jnp.sum(lamv[0:1] * lamv[1:2], axis=-1, keepdims=True))
           - jnp.exp(jnp.sum(lamv[2:3] * lamv[3:4], axis=-1, keepdims=True)) + lam_init)
    n_maps = 2 * DIFF_HEADS

    @pl.when(qi == 0)
    def _():
        for c in range(vt_ref.shape[0]):
            vt_ref[c] = v_ref[c * qb:(c + 1) * qb, :].astype(F32).T.astype(BF16)

    q = q_ref[...]
    lane_group = lax.broadcasted_iota(jnp.int32, (qb, GROUP_WIDTH), 1) // DIFF_QK_DIM
    for g in range(n_maps):
        qs_ref[g * qb:(g + 1) * qb, :] = jnp.where(lane_group == g, q, jnp.zeros_like(q))
    m_ref[...] = jnp.full(m_ref.shape, NEG, F32)
    den_ref[...] = jnp.zeros(den_ref.shape, F32)
    acc_ref[...] = jnp.zeros(acc_ref.shape, F32)
    rel = (lax.broadcasted_iota(jnp.int32, (qb, qb), 1) - lax.broadcasted_iota(jnp.int32, (qb, qb), 0))
    relf = rel.astype(F32)
    slopes = [ALIBI_SLOPES[SWA_Q_HEADS + h] * LOG2_E for h in range(DIFF_HEADS)]
    bias = [slopes[h] * relf for h in range(DIFF_HEADS)]

    def kv_step(c, diagonal):
        start = pl.multiple_of(c * qb, qb)
        sc_all = lax.dot_general(k_ref[pl.ds(start, qb), :], qs_ref[...], NT_DIMS,
                                 preferred_element_type=F32)
        off = ((qi - c) * qb).astype(F32)
        for g in range(n_maps):
            cols = slice(g * qb, (g + 1) * qb)
            sc = sc_all[:, cols] - bias[g // 2]
            if diagonal:
                sc = jnp.where(rel >= 0, sc, NEG)
            shift = slopes[g // 2] * off
            m_old = m_ref[:, cols]
            m_new = jnp.maximum(m_old, jnp.max(sc, axis=0, keepdims=True) - shift)
            alpha = jnp.exp2(m_old - m_new)
            e = jnp.exp2(sc - (m_new + shift))
            den_ref[:, cols] = alpha * den_ref[:, cols] + jnp.sum(e, axis=0, keepdims=True)
            m_ref[:, cols] = m_new
            alpha_ref[:, cols] = alpha
            p_ref[:, cols] = e.astype(BF16)
        pv = jnp.dot(vt_ref[c], p_ref[...], preferred_element_type=F32)
        for g in range(n_maps):
            cols = slice(g * qb, (g + 1) * qb)
            rows = slice((g // 2) * HEAD_DIM, (g // 2 + 1) * HEAD_DIM)
            acc_ref[:, cols] = alpha_ref[:, cols] * acc_ref[:, cols] + pv[rows, cols]

    def off_diagonal(c, carry):
        kv_step(c, False)
        return carry

    lax.fori_loop(0, qi, off_diagonal, 0)
    kv_step(qi, True)

    heads = []
    for h in range(DIFF_HEADS):
        c1 = slice(2 * h * qb, (2 * h + 1) * qb)
        c2 = slice((2 * h + 1) * qb, (2 * h + 2) * qb)
        heads.append(acc_ref[:, c1] / den_ref[:, c1] - lam * (acc_ref[:, c2] / den_ref[:, c2]))
    o = jnp.concatenate(heads, axis=0).T
    y = _group_rms_norm(o, bd64_ref[...], HEAD_DIM, g_ref[...]) * one_minus_lam_init
    y_ref[...] = y.astype(y_ref.dtype)


def _diff_attn(qdn, kdn, z, batch, seq, lamv, bd64, g_d, lam_init):
    t = batch * seq
    nq = seq // DIFF_Q_BLOCK
    v_col = 8
    stacked = 2 * DIFF_HEADS * DIFF_Q_BLOCK
    kernel = functools.partial(_diff_attn_kernel, one_minus_lam_init=1.0 - lam_init, lam_init=lam_init)
    return pl.pallas_call(
        kernel,
        grid=(batch, nq),
        in_specs=[pl.BlockSpec((DIFF_Q_BLOCK, GROUP_WIDTH), lambda b, i: (b * nq + i, 0)),
                  pl.BlockSpec((seq, GROUP_WIDTH), lambda b, i: (b, 0)),
                  pl.BlockSpec((seq, GROUP_WIDTH), lambda b, i: (b, v_col)),
                  pl.BlockSpec(lamv.shape, lambda b, i: (0, 0)),
                  pl.BlockSpec(bd64.shape, lambda b, i: (0, 0)),
                  pl.BlockSpec(g_d.shape, lambda b, i: (0, 0))],
        out_specs=pl.BlockSpec((DIFF_Q_BLOCK, GROUP_WIDTH), lambda b, i: (b * nq + i, 0)),
        out_shape=jax.ShapeDtypeStruct((t, GROUP_WIDTH), BF16),
        scratch_shapes=[pltpu.VMEM((stacked, GROUP_WIDTH), BF16),
                        pltpu.VMEM((seq // DIFF_Q_BLOCK, GROUP_WIDTH, DIFF_Q_BLOCK), BF16),
                        pltpu.VMEM((DIFF_Q_BLOCK, stacked), BF16),
                        pltpu.VMEM((1, stacked), F32),
                        pltpu.VMEM((1, stacked), F32),
                        pltpu.VMEM((1, stacked), F32),
                        pltpu.VMEM((HEAD_DIM, stacked), F32)],
        compiler_params=pltpu.CompilerParams(dimension_semantics=("parallel", "arbitrary"),
                                             vmem_limit_bytes=VMEM_LIMIT),
        name="diff_attn",
    )(qdn, kdn, z, lamv, bd64, g_d)


def _top_values(s, count):
    vals = []
    cur = s
    for r in range(count):
        m = jnp.max(cur, axis=0, keepdims=True)
        vals.append(m)
        if r + 1 < count:
            cur = jnp.where(cur >= m, NEG, cur)
    return vals


def _odd_even_merge_sort_pairs(n):
    pairs = []

    def merge(lo, length, r):
        step = 2 * r
        if step < length:
            merge(lo, length, step)
            merge(lo + r, length, step)
            pairs.extend((i, i + r) for i in range(lo + r, lo + length - r, step))
        else:
            pairs.append((lo, lo + r))

    def sort(lo, length):
        if length > 1:
            sort(lo, length // 2)
            sort(lo + length // 2, length // 2)
            merge(lo, length, 1)

    sort(0, n)
    return pairs


_SORT16 = _odd_even_merge_sort_pairs(PEER_TOPK)


def _top16_sorted(s):
    n = PEER_TOPK
    a = [s[8 * j:8 * j + 8] for j in range(n)]
    for i, j in _SORT16:
        a[i], a[j] = jnp.maximum(a[i], a[j]), jnp.minimum(a[i], a[j])
    for shift in (4, 2, 1):
        b = [pltpu.roll(x, shift, axis=0) for x in a]
        a = [jnp.maximum(a[j], b[n - 1 - j]) for j in range(n)]
        d = n // 2
        while d:
            for i in range(n):
                if not i & d:
                    a[i], a[i + d] = jnp.maximum(a[i], a[i + d]), jnp.minimum(a[i], a[i + d])
            d //= 2
    return a


def _sublane_pick(arrays):
    row = lax.broadcasted_iota(jnp.int32, arrays[0].shape, 0)
    out = arrays[0]
    for s in range(1, 8):
        out = jnp.where(row == s, arrays[s], out)
    return out


def _dup_bf16_words(x):
    bits = pltpu.bitcast(x.astype(BF16).astype(F32), jnp.uint32)
    return bits | (bits >> 16)


def _stack_rows(rows_list):
    n = len(rows_list)
    lanes = rows_list[0].shape[-1]
    ri = lax.broadcasted_iota(jnp.int32, (n, lanes), 0)
    out = jnp.zeros((n, lanes), F32)
    for r, row in enumerate(rows_list):
        out = jnp.where(ri == r, row, out)
    return out


def _out_router_kernel(yabc_ref, yd_ref, x_ref, wo1_ref, wo2_ref, g2_ref, wq_ref, keys_ref,
                       x1_ref, xn2_ref, n1_ref, w1_ref, r2_ref, w2_ref):
    acc = (jnp.dot(yabc_ref[...], wo1_ref[...], preferred_element_type=F32)
           + jnp.dot(yd_ref[...], wo2_ref[...], preferred_element_type=F32))
    x1 = x_ref[...] + acc
    x1_ref[...] = x1
    xn2 = (x1 * lax.rsqrt(jnp.mean(x1 * x1, axis=-1, keepdims=True) + EPS) * g2_ref[...]).astype(BF16)
    xn2_ref[...] = pltpu.bitcast(xn2, jnp.uint32)
    q = jnp.dot(xn2, wq_ref[...], preferred_element_type=F32)
    tm = q.shape[0]
    for h in range(PEER_HEADS):
        scores = []
        for c in range(2):
            lo = (2 * h + c) * PEER_HALF
            qhc = q[:, lo:lo + PEER_HALF].astype(BF16)
            scores.append(lax.dot_general(keys_ref[c], qhc, NT_DIMS, preferred_element_type=F32))
        for p in range(tm // ROUTER_LANES):
            ls = slice(p * ROUTER_LANES, (p + 1) * ROUTER_LANES)
            _route_head(scores[0][:, ls], scores[1][:, ls],
                        n1_ref.at[h, :, ls], w1_ref.at[h, :, ls], r2_ref.at[h, :, ls], w2_ref.at[h, :, ls])


def _route_head(s1, s2, n1_ref, w1_ref, r2_ref, w2_ref):
    lanes = s1.shape[1]
    row8 = lax.broadcasted_iota(jnp.int32, (8, lanes), 0)
    t1 = _top16_sorted(s1)
    t2 = _top16_sorted(s2)
    v1_lo, v1_hi, v2_hi = _sublane_pick(t1[:8]), _sublane_pick(t1[8:]), _sublane_pick(t2[8:])
    groups = [v1_lo + t2[0], v1_hi + t2[0]]
    for b in range(1, 8):
        groups.append(jnp.where(row8 < PEER_TOPK // (b + 1), v1_lo + t2[b], NEG))
    groups.append(t1[0] + v2_hi)
    best = _top_values(jnp.concatenate(groups, axis=0), PEER_TOPK)
    tau = jnp.broadcast_to(best[PEER_TOPK - 1], (8, lanes))
    zsum = jnp.sum(jnp.exp(_stack_rows(best) - best[0]), axis=0, keepdims=True)
    inv_z = jnp.broadcast_to(1.0 / zsum, (8, lanes))
    for j in range(PEER_KEYS // 16):
        words = slice(8 * j, 8 * j + 8)
        rank2, w2 = [], []
        for half in range(2):
            ks = slice(16 * j + 8 * half, 16 * j + 8 * half + 8)
            s1k, s2k = s1[ks], s2[ks]
            n1 = jnp.zeros((8, lanes), F32)
            rk = jnp.full((8, lanes), float(PEER_TOPK), F32)
            for b in reversed(range(PEER_TOPK)):
                n1 = n1 + jnp.where(s1k + t2[b] >= tau, 1.0, 0.0)
                rk = jnp.where(s2k >= t2[b], float(b), rk)
            n1_ref[ks, :] = _dup_bf16_words(n1)
            w1_ref[ks, :] = _dup_bf16_words(jnp.exp(s1k - t1[0]) * inv_z)
            rank2.append(rk)
            w2.append(jnp.exp(s2k - t2[0]))
        r2_ref[words, :] = pltpu.bitcast(jnp.concatenate(rank2, axis=0).astype(BF16), jnp.uint32)
        w2_ref[words, :] = pltpu.bitcast(jnp.concatenate(w2, axis=0).astype(BF16), jnp.uint32)


def _out_router(yabc, yd, x2d, wo1, wo2, g2, wq, keys):
    t = x2d.shape[0]
    tm = min(ROUTER_ROWS, t)

    def whole(arr):
        return pl.BlockSpec(arr.shape, lambda i: (0,) * arr.ndim)

    row_spec = pl.BlockSpec((PEER_HEADS, PEER_KEYS, tm), lambda i: (0, 0, i))
    row_shape = jax.ShapeDtypeStruct((PEER_HEADS, PEER_KEYS, t), jnp.uint32)
    tile_spec = pl.BlockSpec((PEER_HEADS, PEER_KEYS // 2, tm), lambda i: (0, 0, i))
    tile_shape = jax.ShapeDtypeStruct((PEER_HEADS, PEER_KEYS // 2, t), jnp.uint32)
    return pl.pallas_call(
        _out_router_kernel,
        grid=(t // tm,),
        in_specs=[pl.BlockSpec((tm, 3 * GROUP_WIDTH), lambda i: (i, 0)),
                  pl.BlockSpec((tm, GROUP_WIDTH), lambda i: (i, 0)),
                  pl.BlockSpec((tm, D_MODEL), lambda i: (i, 0)),
                  whole(wo1), whole(wo2), whole(g2), whole(wq), whole(keys)],
        out_specs=[pl.BlockSpec((tm, D_MODEL), lambda i: (i, 0)),
                   pl.BlockSpec((tm // 2, D_MODEL), lambda i: (i, 0)),
                   row_spec, row_spec, tile_spec, tile_spec],
        out_shape=[jax.ShapeDtypeStruct((t, D_MODEL), F32),
                   jax.ShapeDtypeStruct((t // 2, D_MODEL), jnp.uint32),
                   row_shape, row_shape, tile_shape, tile_shape],
        compiler_params=pltpu.CompilerParams(dimension_semantics=("parallel",),
                                             vmem_limit_bytes=VMEM_LIMIT),
        name="out_router",
    )(yabc, yd, x2d, wo1, wo2, g2, wq, keys)


def _gate_activations(h_ref, a_ref, n1_ref, w1_ref, r2_ref, w2_ref, first_row, rows):
    tb = h_ref.shape[1]
    zero = jnp.zeros((), BF16)
    half = PEER_KEYS // 2
    for lc in range(tb // 128):
        ls = slice(lc * 128, (lc + 1) * 128)
        n1_rows = [n1_ref[hd, pl.ds(first_row, rows), ls] for hd in range(PEER_HEADS)]
        w1_rows = [w1_ref[hd, pl.ds(first_row, rows), ls] for hd in range(PEER_HEADS)]
        for r in range(rows):
            w = None
            for hd in range(PEER_HEADS):
                n1b = pltpu.bitcast(jnp.broadcast_to(n1_rows[hd][r:r + 1], (half, 128)), BF16)
                w1b = pltpu.bitcast(jnp.broadcast_to(w1_rows[hd][r:r + 1], (half, 128)), BF16)
                rank2 = pltpu.bitcast(r2_ref[hd, :, ls], BF16)
                w2 = pltpu.bitcast(w2_ref[hd, :, ls], BF16)
                term = jnp.where(rank2 < n1b, w2 * w1b, zero)
                w = term if w is None else w + term
            g = _gelu(h_ref[r * PEER_KEYS:(r + 1) * PEER_KEYS, ls]).astype(BF16)
            a_ref[r * half:(r + 1) * half, ls] = pltpu.bitcast(w * g, jnp.uint32)


def _peer_kernel(x_ref, xn_ref, u_ref, vt_ref, n1_ref, w1_ref, r2_ref, w2_ref, o_ref,
                 acc_ref, h_ref, a_ref, *, rows):
    j = pl.program_id(1)

    @pl.when(j == 0)
    def _():
        acc_ref[...] = jnp.zeros_like(acc_ref)

    h_ref[...] = lax.dot_general(u_ref[...], pltpu.bitcast(xn_ref[...], BF16), NT_DIMS,
                                 preferred_element_type=F32)
    first_row = pl.multiple_of(j * rows, rows)
    _gate_activations(h_ref, a_ref, n1_ref, w1_ref, r2_ref, w2_ref, first_row, rows)
    acc_ref[...] += jnp.dot(vt_ref[...], pltpu.bitcast(a_ref[...], BF16), preferred_element_type=F32)

    @pl.when(j == pl.num_programs(1) - 1)
    def _():
        o_ref[...] = x_ref[...] + acc_ref[...].T


def _peer(x1, xn2, u_bf16, vt_bf16, n1, w1, r2, w2):
    t = x1.shape[0]
    tb = min(PEER_TOKENS, t)
    ec = PEER_EXPERT_CHUNK
    rows = ec // PEER_KEYS
    assert rows % 8 == 0 and tb % 128 == 0 and t % tb == 0 and PEER_EXPERTS % ec == 0
    row_spec = pl.BlockSpec((PEER_HEADS, PEER_KEYS, tb), lambda i, j: (0, 0, i))
    tile_spec = pl.BlockSpec((PEER_HEADS, PEER_KEYS // 2, tb), lambda i, j: (0, 0, i))
    kernel = functools.partial(_peer_kernel, rows=rows)
    return pl.pallas_call(
        kernel,
        grid=(t // tb, PEER_EXPERTS // ec),
        in_specs=[pl.BlockSpec((tb, D_MODEL), lambda i, j: (i, 0)),
                  pl.BlockSpec((tb // 2, D_MODEL), lambda i, j: (i, 0)),
                  pl.BlockSpec((ec, D_MODEL), lambda i, j: (j, 0)),
                  pl.BlockSpec((D_MODEL, ec), lambda i, j: (0, j)),
                  row_spec, row_spec, tile_spec, tile_spec],
        out_specs=pl.BlockSpec((tb, D_MODEL), lambda i, j: (i, 0)),
        out_shape=jax.ShapeDtypeStruct((t, D_MODEL), F32),
        scratch_shapes=[pltpu.VMEM((D_MODEL, tb), F32),
                        pltpu.VMEM((ec, tb), F32),
                        pltpu.VMEM((ec // 2, tb), jnp.uint32)],
        compiler_params=pltpu.CompilerParams(dimension_semantics=("parallel", "arbitrary"),
                                             vmem_limit_bytes=VMEM_LIMIT),
        name="peer",
    )(x1, xn2, u_bf16, vt_bf16, n1, w1, r2, w2)


def _block_diag_ones(width, group):
    idx = np.arange(width) // group
    return jnp.asarray((idx[:, None] == idx[None, :]).astype(np.float32), BF16)


def _layer(x2d, batch, seq, layer_idx, norm1_g, w_in, w_pool, swa_q_norm, swa_k_norm, swa_sinks, conv_w,
           diff_q_norm, diff_k_norm, lam_q1, lam_k1, lam_q2, lam_k2, out_norm_g, w_out, norm2_g, w_query,
           sub_keys, peer_u, peer_v):
    lam_init = 0.8 - 0.6 * float(np.exp(-0.3 * layer_idx))
    bd64 = _block_diag_ones(GROUP_WIDTH, HEAD_DIM)
    bd32 = _block_diag_ones(GROUP_WIDTH, DIFF_QK_DIM)
    wpool_bd = jax.scipy.linalg.block_diag(*[w_pool[g] for g in range(len(POOL_WINDOWS))]).astype(BF16)

    z = _in_proj(x2d, norm1_g.reshape(1, D_MODEL), w_in.astype(BF16))
    yabc, qdn, kdn = _local_mix(
        z, batch, seq, swa_sinks.astype(F32), wpool_bd, bd64, bd32,
        jnp.tile(swa_q_norm, SWA_Q_HEADS).reshape(1, GROUP_WIDTH),
        jnp.tile(swa_k_norm, 2).reshape(1, 128),
        conv_w, out_norm_g[:3 * GROUP_WIDTH].reshape(1, 3 * GROUP_WIDTH),
        jnp.tile(diff_q_norm, 2 * DIFF_HEADS).reshape(1, GROUP_WIDTH),
        jnp.tile(diff_k_norm, 2 * DIFF_HEADS).reshape(1, GROUP_WIDTH))
    lamv = jnp.stack([lam_q1, lam_k1, lam_q2, lam_k2]).astype(F32)
    yd = _diff_attn(qdn, kdn, z, batch, seq, lamv, bd64,
                    out_norm_g[3 * GROUP_WIDTH:].reshape(1, GROUP_WIDTH), lam_init)
    w_out_bf16 = w_out.astype(BF16)
    x1, xn2, n1, w1, r2, w2 = _out_router(
        yabc, yd, x2d, w_out_bf16[:3 * GROUP_WIDTH], w_out_bf16[3 * GROUP_WIDTH:],
        norm2_g.reshape(1, D_MODEL), w_query.astype(BF16), sub_keys.astype(BF16))
    return _peer(x1, xn2, peer_u.astype(BF16), peer_v.T.astype(BF16), n1, w1, r2, w2)


def kernel(x, norm1_g, w_in, w_pool, swa_q_norm, swa_k_norm, swa_sinks, conv_w, diff_q_norm, diff_k_norm,
           lam_q1, lam_k1, lam_q2, lam_k2, out_norm_g, w_out, norm2_g, w_query, sub_keys, peer_u, peer_v):
    batch, seq, d = x.shape
    x2d = x.reshape(batch * seq, d)
    depth = norm1_g.shape[0]
    for l in range(depth):
        x2d = _layer(x2d, batch, seq, l, norm1_g[l], w_in[l], w_pool[l], swa_q_norm[l], swa_k_norm[l],
                     swa_sinks[l], conv_w[l], diff_q_norm[l], diff_k_norm[l], lam_q1[l], lam_k1[l],
                     lam_q2[l], lam_k2[l], out_norm_g[l], w_out[l], norm2_g[l], w_query[l], sub_keys[l],
                     peer_u[l], peer_v[l])
    return x2d.reshape(batch, seq, d)
```

```python
import functools

import jax
import jax.numpy as jnp
import numpy as np
from jax import lax
from jax.experimental import pallas as pl
from jax.experimental.pallas import tpu as pltpu

F32 = jnp.float32
BF16 = jnp.bfloat16

D_MODEL = 1024
EPS = 1e-6
HEAD_DIM = 64
GROUP_WIDTH = 256
POOL_WINDOWS = (2, 4, 8, 16)
SWA_Q_HEADS = 4
SWA_GQA = 2
SWA_WINDOW = 128
DIFF_HEADS = 4
DIFF_QK_DIM = 32
IN_PROJ_WIDTH = 2304
PEER_HEADS = 8
PEER_KEYS = 128
PEER_EXPERTS = PEER_KEYS * PEER_KEYS
PEER_HALF = 128
PEER_TOPK = 16
ALIBI_SLOPES = tuple(2.0 ** (-(i + 1)) for i in range(SWA_Q_HEADS + DIFF_HEADS))

NEG = -1e30
LOG2_E = 1.4426950408889634
HALO = 16
ROW_BLOCK = 128
DIFF_Q_BLOCK = 256
IN_PROJ_ROWS = 512
ROUTER_ROWS = 512
ROUTER_LANES = 256
PEER_TOKENS = 512
PEER_EXPERT_CHUNK = 2048
PEER_H_ROWS = 1024
VMEM_LIMIT = 56 * 1024 * 1024

NT_DIMS = (((1,), (1,)), ((), ()))


def _group_sumsq(x, bd):
    x2 = x * x
    hi = x2.astype(BF16)
    lo = (x2 - hi.astype(F32)).astype(BF16)
    return (jnp.dot(hi, bd, preferred_element_type=F32)
            + jnp.dot(lo, bd, preferred_element_type=F32))


def _group_rms_norm(x, bd, group, gain):
    return x * lax.rsqrt(_group_sumsq(x, bd) * (1.0 / group) + EPS) * gain


def _gelu(h):
    return 0.5 * h * (1.0 + lax.erf(h * np.float32(np.sqrt(0.5))))


def _in_proj_kernel(x_ref, g_ref, w_ref, z_ref):
    x = x_ref[...]
    xn = x * lax.rsqrt(jnp.mean(x * x, axis=-1, keepdims=True) + EPS) * g_ref[...]
    z = jnp.dot(xn.astype(BF16), w_ref[...], preferred_element_type=F32)
    z_ref[...] = z.astype(z_ref.dtype)


def _in_proj(x2d, g, w_bf16):
    t = x2d.shape[0]
    tm = min(IN_PROJ_ROWS, t)
    return pl.pallas_call(
        _in_proj_kernel,
        grid=(t // tm,),
        in_specs=[pl.BlockSpec((tm, D_MODEL), lambda i: (i, 0)),
                  pl.BlockSpec((1, D_MODEL), lambda i: (0, 0)),
                  pl.BlockSpec((D_MODEL, IN_PROJ_WIDTH), lambda i: (0, 0))],
        out_specs=pl.BlockSpec((tm, IN_PROJ_WIDTH), lambda i: (i, 0)),
        out_shape=jax.ShapeDtypeStruct((t, IN_PROJ_WIDTH), BF16),
        compiler_params=pltpu.CompilerParams(dimension_semantics=("parallel",),
                                             vmem_limit_bytes=VMEM_LIMIT),
        name="in_proj",
    )(x2d, g, w_bf16)


def _local_mix_kernel(sinks_ref, a_ref, a_halo_ref, bq_ref, bkv_ref, bkv_prev_ref, ch_ref, ch_halo_ref,
                      cb_ref, cc_ref, cc_halo_ref, dq_ref, dk_ref,
                      wpool_ref, bd64_ref, bd32_ref, gq_swa_ref, gk_swa_ref, conv_w_ref, g_out_ref,
                      gq_diff_ref, gk_diff_ref,
                      y_ref, qdn_ref, kdn_ref):
    n = pl.program_id(1)
    not_first = (n > 0).astype(F32)
    bd64 = bd64_ref[...]
    rows = lax.broadcasted_iota(jnp.int32, (ROW_BLOCK, GROUP_WIDTH), 0)
    cols = lax.broadcasted_iota(jnp.int32, (ROW_BLOCK, GROUP_WIDTH), 1)

    p = a_ref[...].astype(F32)
    p_ext = jnp.concatenate([a_halo_ref[...].astype(F32) * not_first, p], axis=0)
    s2 = p_ext + pltpu.roll(p_ext, 1, axis=0)
    s4 = s2 + pltpu.roll(s2, 2, axis=0)
    s8 = s4 + pltpu.roll(s4, 4, axis=0)
    s16 = s8 + pltpu.roll(s8, 8, axis=0)
    wsel = jnp.where(cols < 64, 2, jnp.where(cols < 128, 4, jnp.where(cols < 192, 8, 16)))
    ssel = jnp.where(cols < 64, s2[HALO:], jnp.where(cols < 128, s4[HALO:],
                                                      jnp.where(cols < 192, s8[HALO:], s16[HALO:])))
    cnt = jnp.minimum(n * ROW_BLOCK + rows + 1, wsel).astype(F32)
    d = ssel / cnt - p
    y_a = jnp.dot(d.astype(BF16), wpool_ref[...], preferred_element_type=F32)

    q = _group_rms_norm(bq_ref[...].astype(F32), bd64, HEAD_DIM, gq_swa_ref[...]) * (HEAD_DIM ** -0.5)
    qn = q.astype(BF16)
    bd64k = bd64[:128, :128]
    kv_cur = bkv_ref[...]
    kv_prev = bkv_prev_ref[...]
    k_cur = _group_rms_norm(kv_cur[:, :128].astype(F32), bd64k, HEAD_DIM, gk_swa_ref[...])
    k_prev = _group_rms_norm(kv_prev[:, :128].astype(F32), bd64k, HEAD_DIM, gk_swa_ref[...])
    kn = jnp.concatenate([k_prev, k_cur], axis=0).astype(BF16)
    v = jnp.concatenate([kv_prev[:, 128:], kv_cur[:, 128:]], axis=0)
    qi = lax.broadcasted_iota(jnp.int32, (ROW_BLOCK, 2 * ROW_BLOCK), 0)
    kj = lax.broadcasted_iota(jnp.int32, (ROW_BLOCK, 2 * ROW_BLOCK), 1)
    dist = qi + ROW_BLOCK - kj
    valid = (dist >= 0) & (dist < SWA_WINDOW) & ((n - 1) * ROW_BLOCK + kj >= 0)
    distf = dist.astype(F32)
    heads = []
    for h in range(SWA_Q_HEADS):
        hk = h // SWA_GQA
        sc = lax.dot_general(qn[:, h * 64:(h + 1) * 64], kn[:, hk * 64:(hk + 1) * 64], NT_DIMS,
                             preferred_element_type=F32)
        sc = jnp.where(valid, sc - ALIBI_SLOPES[h] * distf, NEG)
        sink = sinks_ref[h]
        mx = jnp.maximum(jnp.max(sc, axis=-1, keepdims=True), sink)
        e = jnp.exp(sc - mx)
        den = jnp.sum(e, axis=-1, keepdims=True) + jnp.exp(sink - mx)
        o = jnp.dot(e.astype(BF16), v[:, hk * 64:(hk + 1) * 64], preferred_element_type=F32)
        heads.append(o / den)
    y_b = jnp.concatenate(heads, axis=-1)

    zc = cc_ref[...].astype(F32) * ch_ref[...].astype(F32)
    zc_halo = cc_halo_ref[...].astype(F32) * ch_halo_ref[...].astype(F32) * not_first
    zc_ext = jnp.concatenate([zc_halo, zc], axis=0)
    cw = conv_w_ref[...]
    conv = (cw[0:1] * pltpu.roll(zc_ext, 2, axis=0)[HALO:] + cw[1:2] * pltpu.roll(zc_ext, 1, axis=0)[HALO:]
            + cw[2:3] * zc)
    y_c = cb_ref[...].astype(F32) * conv

    g_out = g_out_ref[...]
    for gi, y in enumerate((y_a, y_b, y_c)):
        lo, hi = gi * GROUP_WIDTH, (gi + 1) * GROUP_WIDTH
        y_ref[:, lo:hi] = _group_rms_norm(y, bd64, HEAD_DIM, g_out[:, lo:hi]).astype(y_ref.dtype)

    bd32 = bd32_ref[...]
    qd = (_group_rms_norm(dq_ref[...].astype(F32), bd32, DIFF_QK_DIM, gq_diff_ref[...])
          * (DIFF_QK_DIM ** -0.5 * LOG2_E))
    kd = _group_rms_norm(dk_ref[...].astype(F32), bd32, DIFF_QK_DIM, gk_diff_ref[...])
    qdn_ref[...] = qd.astype(qdn_ref.dtype)
    kdn_ref[...] = kd.astype(kdn_ref.dtype)


def _local_mix(z, batch, seq, sinks, wpool_bd, bd64, bd32, gq_swa, gk_swa, conv_w, g_out, gq_diff, gk_diff):
    t = batch * seq
    nb = seq // ROW_BLOCK
    halo_per_block = ROW_BLOCK // HALO

    def cur(col):
        return pl.BlockSpec((ROW_BLOCK, GROUP_WIDTH), lambda b, n: (b * nb + n, col))

    def prev(col):
        return pl.BlockSpec((ROW_BLOCK, GROUP_WIDTH), lambda b, n: (b * nb + jnp.maximum(n - 1, 0), col))

    def halo(col):
        return pl.BlockSpec(
            (HALO, GROUP_WIDTH),
            lambda b, n: (jnp.maximum((b * nb + n) * halo_per_block - 1, 0), col))

    def whole(arr):
        return pl.BlockSpec(arr.shape, lambda b, n: (0,) * arr.ndim)

    params = (wpool_bd, bd64, bd32, gq_swa, gk_swa, conv_w, g_out, gq_diff, gk_diff)
    in_specs = ([pl.BlockSpec(memory_space=pltpu.SMEM),
                 cur(0), halo(0), cur(1), cur(2), prev(2), cur(3), halo(3), cur(4), cur(5), halo(5),
                 cur(6), cur(7)] + [whole(a) for a in params])
    out_specs = [pl.BlockSpec((ROW_BLOCK, 3 * GROUP_WIDTH), lambda b, n: (b * nb + n, 0)),
                 pl.BlockSpec((ROW_BLOCK, GROUP_WIDTH), lambda b, n: (b * nb + n, 0)),
                 pl.BlockSpec((ROW_BLOCK, GROUP_WIDTH), lambda b, n: (b * nb + n, 0))]
    out_shape = [jax.ShapeDtypeStruct((t, 3 * GROUP_WIDTH), BF16),
                 jax.ShapeDtypeStruct((t, GROUP_WIDTH), BF16),
                 jax.ShapeDtypeStruct((t, GROUP_WIDTH), BF16)]
    return pl.pallas_call(
        _local_mix_kernel,
        grid=(batch, nb),
        in_specs=in_specs,
        out_specs=out_specs,
        out_shape=out_shape,
        compiler_params=pltpu.CompilerParams(dimension_semantics=("parallel", "parallel"),
                                             vmem_limit_bytes=VMEM_LIMIT),
        name="local_mix",
    )(sinks, *([z] * 12), *params)


def _diff_attn_kernel(q_ref, k_ref, v_ref, lamv_ref, bd64_ref, g_ref, y_ref,
                      qs_ref, vt_ref, p_ref, m_ref, den_ref, alpha_ref, acc_ref, *, one_minus_lam_init, lam_init):
    qi = pl.program_id(1)
    qb = DIFF_Q_BLOCK
    lamv = lamv_ref[...]
    lam = (jnp.exp(jnp.sum(lamv[0:1] * lamv[1:2], axis=-1, keepdims=True))
           - jnp.exp(jnp.sum(lamv[2:3] * lamv[3:4], axis=-1, keepdims=True)) + lam_init)
    n_maps = 2 * DIFF_HEADS

    @pl.when(qi == 0)
    def _():
        for c in range(vt_ref.shape[0]):
            vt_ref[c] = v_ref[c * qb:(c + 1) * qb, :].astype(F32).T.astype(BF16)

    q = q_ref[...]
    lane_group = lax.broadcasted_iota(jnp.int32, (qb, GROUP_WIDTH), 1) // DIFF_QK_DIM
    for g in range(n_maps):
        qs_ref[g * qb:(g + 1) * qb, :] = jnp.where(lane_group == g, q, jnp.zeros_like(q))
    m_ref[...] = jnp.full(m_ref.shape, NEG, F32)
    den_ref[...] = jnp.zeros(den_ref.shape, F32)
    acc_ref[...] = jnp.zeros(acc_ref.shape, F32)
    rel = (lax.broadcasted_iota(jnp.int32, (qb, qb), 1) - lax.broadcasted_iota(jnp.int32, (qb, qb), 0))
    relf = rel.astype(F32)
    slopes = [ALIBI_SLOPES[SWA_Q_HEADS + h] * LOG2_E for h in range(DIFF_HEADS)]
    bias = [slopes[h] * relf for h in range(DIFF_HEADS)]

    def kv_step(c, diagonal):
        start = pl.multiple_of(c * qb, qb)
        sc_all = lax.dot_general(k_ref[pl.ds(start, qb), :], qs_ref[...], NT_DIMS,
                                 preferred_element_type=F32)
        off = ((qi - c) * qb).astype(F32)
        for g in range(n_maps):
            cols = slice(g * qb, (g + 1) * qb)
            sc = sc_all[:, cols] - bias[g // 2]
            if diagonal:
                sc = jnp.where(rel >= 0, sc, NEG)
            shift = slopes[g // 2] * off
            m_old = m_ref[:, cols]
            m_new = jnp.maximum(m_old, jnp.max(sc, axis=0, keepdims=True) - shift)
            alpha = jnp.exp2(m_old - m_new)
            e = jnp.exp2(sc - (m_new + shift))
            den_ref[:, cols] = alpha * den_ref[:, cols] + jnp.sum(e, axis=0, keepdims=True)
            m_ref[:, cols] = m_new
            alpha_ref[:, cols] = alpha
            p_ref[:, cols] = e.astype(BF16)
        pv = jnp.dot(vt_ref[c], p_ref[...], preferred_element_type=F32)
        for g in range(n_maps):
            cols = slice(g * qb, (g + 1) * qb)
            rows = slice((g // 2) * HEAD_DIM, (g // 2 + 1) * HEAD_DIM)
            acc_ref[:, cols] = alpha_ref[:, cols] * acc_ref[:, cols] + pv[rows, cols]

    def off_diagonal(c, carry):
        kv_step(c, False)
        return carry

    lax.fori_loop(0, qi, off_diagonal, 0)
    kv_step(qi, True)

    heads = []
    for h in range(DIFF_HEADS):
        c1 = slice(2 * h * qb, (2 * h + 1) * qb)
        c2 = slice((2 * h + 1) * qb, (2 * h + 2) * qb)
        heads.append(acc_ref[:, c1] / den_ref[:, c1] - lam * (acc_ref[:, c2] / den_ref[:, c2]))
    o = jnp.concatenate(heads, axis=0).T
    y = _group_rms_norm(o, bd64_ref[...], HEAD_DIM, g_ref[...]) * one_minus_lam_init
    y_ref[...] = y.astype(y_ref.dtype)


def _diff_attn(qdn, kdn, z, batch, seq, lamv, bd64, g_d, lam_init):
    t = batch * seq
    nq = seq // DIFF_Q_BLOCK
    v_col = 8
    stacked = 2 * DIFF_HEADS * DIFF_Q_BLOCK
    kernel = functools.partial(_diff_attn_kernel, one_minus_lam_init=1.0 - lam_init, lam_init=lam_init)
    return pl.pallas_call(
        kernel,
        grid=(batch, nq),
        in_specs=[pl.BlockSpec((DIFF_Q_BLOCK, GROUP_WIDTH), lambda b, i: (b * nq + i, 0)),
                  pl.BlockSpec((seq, GROUP_WIDTH), lambda b, i: (b, 0)),
                  pl.BlockSpec((seq, GROUP_WIDTH), lambda b, i: (b, v_col)),
                  pl.BlockSpec(lamv.shape, lambda b, i: (0, 0)),
                  pl.BlockSpec(bd64.shape, lambda b, i: (0, 0)),
                  pl.BlockSpec(g_d.shape, lambda b, i: (0, 0))],
        out_specs=pl.BlockSpec((DIFF_Q_BLOCK, GROUP_WIDTH), lambda b, i: (b * nq + i, 0)),
        out_shape=jax.ShapeDtypeStruct((t, GROUP_WIDTH), BF16),
        scratch_shapes=[pltpu.VMEM((stacked, GROUP_WIDTH), BF16),
                        pltpu.VMEM((seq // DIFF_Q_BLOCK, GROUP_WIDTH, DIFF_Q_BLOCK), BF16),
                        pltpu.VMEM((DIFF_Q_BLOCK, stacked), BF16),
                        pltpu.VMEM((1, stacked), F32),
                        pltpu.VMEM((1, stacked), F32),
                        pltpu.VMEM((1, stacked), F32),
                        pltpu.VMEM((HEAD_DIM, stacked), F32)],
        compiler_params=pltpu.CompilerParams(dimension_semantics=("parallel", "arbitrary"),
                                             vmem_limit_bytes=VMEM_LIMIT),
        name="diff_attn",
    )(qdn, kdn, z, lamv, bd64, g_d)


def _top_values(s, count):
    vals = []
    cur = s
    for r in range(count):
        m = jnp.max(cur, axis=0, keepdims=True)
        vals.append(m)
        if r + 1 < count:
            cur = jnp.where(cur >= m, NEG, cur)
    return vals


def _odd_even_merge_sort_pairs(n):
    pairs = []

    def merge(lo, length, r):
        step = 2 * r
        if step < length:
            merge(lo, length, step)
            merge(lo + r, length, step)
            pairs.extend((i, i + r) for i in range(lo + r, lo + length - r, step))
        else:
            pairs.append((lo, lo + r))

    def sort(lo, length):
        if length > 1:
            sort(lo, length // 2)
            sort(lo + length // 2, length // 2)
            merge(lo, length, 1)

    sort(0, n)
    return pairs


_SORT16 = _odd_even_merge_sort_pairs(PEER_TOPK)


def _top16_sorted(s):
    n = PEER_TOPK
    a = [s[8 * j:8 * j + 8] for j in range(n)]
    for i, j in _SORT16:
        a[i], a[j] = jnp.maximum(a[i], a[j]), jnp.minimum(a[i], a[j])
    for shift in (4, 2, 1):
        b = [pltpu.roll(x, shift, axis=0) for x in a]
        a = [jnp.maximum(a[j], b[n - 1 - j]) for j in range(n)]
        d = n // 2
        while d:
            for i in range(n):
                if not i & d:
                    a[i], a[i + d] = jnp.maximum(a[i], a[i + d]), jnp.minimum(a[i], a[i + d])
            d //= 2
    return a


def _sublane_pick(arrays):
    row = lax.broadcasted_iota(jnp.int32, arrays[0].shape, 0)
    out = arrays[0]
    for s in range(1, 8):
        out = jnp.where(row == s, arrays[s], out)
    return out


def _dup_bf16_words(x):
    bits = pltpu.bitcast(x.astype(BF16).astype(F32), jnp.uint32)
    return bits | (bits >> 16)


def _stack_rows(rows_list):
    n = len(rows_list)
    lanes = rows_list[0].shape[-1]
    ri = lax.broadcasted_iota(jnp.int32, (n, lanes), 0)
    out = jnp.zeros((n, lanes), F32)
    for r, row in enumerate(rows_list):
        out = jnp.where(ri == r, row, out)
    return out


def _out_router_kernel(yabc_ref, yd_ref, x_ref, wo1_ref, wo2_ref, g2_ref, wq_ref, keys_ref,
                       x1_ref, xn2_ref, n1_ref, w1_ref, r2_ref, w2_ref):
    acc = (jnp.dot(yabc_ref[...], wo1_ref[...], preferred_element_type=F32)
           + jnp.dot(yd_ref[...], wo2_ref[...], preferred_element_type=F32))
    x1 = x_ref[...] + acc
    x1_ref[...] = x1
    xn2 = (x1 * lax.rsqrt(jnp.mean(x1 * x1, axis=-1, keepdims=True) + EPS) * g2_ref[...]).astype(BF16)
    xn2_ref[...] = pltpu.bitcast(xn2, jnp.uint32)
    q = jnp.dot(xn2, wq_ref[...], preferred_element_type=F32)
    tm = q.shape[0]
    for h in range(PEER_HEADS):
        scores = []
        for c in range(2):
            lo = (2 * h + c) * PEER_HALF
            qhc = q[:, lo:lo + PEER_HALF].astype(BF16)
            scores.append(lax.dot_general(keys_ref[c], qhc, NT_DIMS, preferred_element_type=F32))
        for p in range(tm // ROUTER_LANES):
            ls = slice(p * ROUTER_LANES, (p + 1) * ROUTER_LANES)
            _route_head(scores[0][:, ls], scores[1][:, ls],
                        n1_ref.at[h, :, ls], w1_ref.at[h, :, ls], r2_ref.at[h, :, ls], w2_ref.at[h, :, ls])


def _route_head(s1, s2, n1_ref, w1_ref, r2_ref, w2_ref):
    lanes = s1.shape[1]
    row8 = lax.broadcasted_iota(jnp.int32, (8, lanes), 0)
    t1 = _top16_sorted(s1)
    t2 = _top16_sorted(s2)
    v1_lo, v1_hi, v2_hi = _sublane_pick(t1[:8]), _sublane_pick(t1[8:]), _sublane_pick(t2[8:])
    groups = [v1_lo + t2[0], v1_hi + t2[0]]
    for b in range(1, 8):
        groups.append(jnp.where(row8 < PEER_TOPK // (b + 1), v1_lo + t2[b], NEG))
    groups.append(t1[0] + v2_hi)
    best = _top_values(jnp.concatenate(groups, axis=0), PEER_TOPK)
    tau = jnp.broadcast_to(best[PEER_TOPK - 1], (8, lanes))
    zsum = jnp.sum(jnp.exp(_stack_rows(best) - best[0]), axis=0, keepdims=True)
    inv_z = jnp.broadcast_to(1.0 / zsum, (8, lanes))
    for j in range(PEER_KEYS // 16):
        words = slice(8 * j, 8 * j + 8)
        rank2, w2 = [], []
        for half in range(2):
            ks = slice(16 * j + 8 * half, 16 * j + 8 * half + 8)
            s1k, s2k = s1[ks], s2[ks]
            n1 = jnp.zeros((8, lanes), F32)
            rk = jnp.full((8, lanes), float(PEER_TOPK), F32)
            for b in reversed(range(PEER_TOPK)):
                n1 = n1 + jnp.where(s1k + t2[b] >= tau, 1.0, 0.0)
                rk = jnp.where(s2k >= t2[b], float(b), rk)
            n1_ref[ks, :] = _dup_bf16_words(n1)
            w1_ref[ks, :] = _dup_bf16_words(jnp.exp(s1k - t1[0]) * inv_z)
            rank2.append(rk)
            w2.append(jnp.exp(s2k - t2[0]))
        r2_ref[words, :] = pltpu.bitcast(jnp.concatenate(rank2, axis=0).astype(BF16), jnp.uint32)
        w2_ref[words, :] = pltpu.bitcast(jnp.concatenate(w2, axis=0).astype(BF16), jnp.uint32)


def _out_router(yabc, yd, x2d, wo1, wo2, g2, wq, keys):
    t = x2d.shape[0]
    tm = min(ROUTER_ROWS, t)

    def whole(arr):
        return pl.BlockSpec(arr.shape, lambda i: (0,) * arr.ndim)

    row_spec = pl.BlockSpec((PEER_HEADS, PEER_KEYS, tm), lambda i: (0, 0, i))
    row_shape = jax.ShapeDtypeStruct((PEER_HEADS, PEER_KEYS, t), jnp.uint32)
    tile_spec = pl.BlockSpec((PEER_HEADS, PEER_KEYS // 2, tm), lambda i: (0, 0, i))
    tile_shape = jax.ShapeDtypeStruct((PEER_HEADS, PEER_KEYS // 2, t), jnp.uint32)
    return pl.pallas_call(
        _out_router_kernel,
        grid=(t // tm,),
        in_specs=[pl.BlockSpec((tm, 3 * GROUP_WIDTH), lambda i: (i, 0)),
                  pl.BlockSpec((tm, GROUP_WIDTH), lambda i: (i, 0)),
                  pl.BlockSpec((tm, D_MODEL), lambda i: (i, 0)),
                  whole(wo1), whole(wo2), whole(g2), whole(wq), whole(keys)],
        out_specs=[pl.BlockSpec((tm, D_MODEL), lambda i: (i, 0)),
                   pl.BlockSpec((tm // 2, D_MODEL), lambda i: (i, 0)),
                   row_spec, row_spec, tile_spec, tile_spec],
        out_shape=[jax.ShapeDtypeStruct((t, D_MODEL), F32),
                   jax.ShapeDtypeStruct((t // 2, D_MODEL), jnp.uint32),
                   row_shape, row_shape, tile_shape, tile_shape],
        compiler_params=pltpu.CompilerParams(dimension_semantics=("parallel",),
                                             vmem_limit_bytes=VMEM_LIMIT),
        name="out_router",
    )(yabc, yd, x2d, wo1, wo2, g2, wq, keys)


def _gate_activations(h_ref, a_ref, n1_ref, w1_ref, r2_ref, w2_ref, first_row, rows):
    tb = h_ref.shape[1]
    zero = jnp.zeros((), BF16)
    half = PEER_KEYS // 2
    for lc in range(tb // 128):
        ls = slice(lc * 128, (lc + 1) * 128)
        n1_rows = [n1_ref[hd, pl.ds(first_row, rows), ls] for hd in range(PEER_HEADS)]
        w1_rows = [w1_ref[hd, pl.ds(first_row, rows), ls] for hd in range(PEER_HEADS)]
        for r in range(rows):
            w = None
            for hd in range(PEER_HEADS):
                n1b = pltpu.bitcast(jnp.broadcast_to(n1_rows[hd][r:r + 1], (half, 128)), BF16)
                w1b = pltpu.bitcast(jnp.broadcast_to(w1_rows[hd][r:r + 1], (half, 128)), BF16)
                rank2 = pltpu.bitcast(r2_ref[hd, :, ls], BF16)
                w2 = pltpu.bitcast(w2_ref[hd, :, ls], BF16)
                term = jnp.where(rank2 < n1b, w2 * w1b, zero)
                w = term if w is None else w + term
            g = _gelu(h_ref[r * PEER_KEYS:(r + 1) * PEER_KEYS, ls]).astype(BF16)
            a_ref[r * half:(r + 1) * half, ls] = pltpu.bitcast(w * g, jnp.uint32)


def _peer_kernel(x_ref, xn_ref, u_ref, vt_ref, n1_ref, w1_ref, r2_ref, w2_ref, o_ref,
                 acc_ref, h_ref, a_ref, *, rows):
    j = pl.program_id(1)

    @pl.when(j == 0)
    def _():
        acc_ref[...] = jnp.zeros_like(acc_ref)

    xn = pltpu.bitcast(xn_ref[...], BF16)
    for m in range(0, u_ref.shape[0], PEER_H_ROWS):
        h_ref[m:m + PEER_H_ROWS, :] = lax.dot_general(u_ref[m:m + PEER_H_ROWS, :], xn, NT_DIMS,
                                                      preferred_element_type=F32)
    first_row = pl.multiple_of(j * rows, rows)
    _gate_activations(h_ref, a_ref, n1_ref, w1_ref, r2_ref, w2_ref, first_row, rows)
    acc_ref[...] += jnp.dot(vt_ref[...], pltpu.bitcast(a_ref[...], BF16), preferred_element_type=F32)

    @pl.when(j == pl.num_programs(1) - 1)
    def _():
        o_ref[...] = x_ref[...] + acc_ref[...].T


def _peer(x1, xn2, u_bf16, vt_bf16, n1, w1, r2, w2):
    t = x1.shape[0]
    tb = min(PEER_TOKENS, t)
    ec = PEER_EXPERT_CHUNK
    rows = ec // PEER_KEYS
    assert rows % 8 == 0 and tb % 128 == 0 and t % tb == 0 and PEER_EXPERTS % ec == 0
    row_spec = pl.BlockSpec((PEER_HEADS, PEER_KEYS, tb), lambda i, j: (0, 0, i))
    tile_spec = pl.BlockSpec((PEER_HEADS, PEER_KEYS // 2, tb), lambda i, j: (0, 0, i))
    kernel = functools.partial(_peer_kernel, rows=rows)
    return pl.pallas_call(
        kernel,
        grid=(t // tb, PEER_EXPERTS // ec),
        in_specs=[pl.BlockSpec((tb, D_MODEL), lambda i, j: (i, 0)),
                  pl.BlockSpec((tb // 2, D_MODEL), lambda i, j: (i, 0)),
                  pl.BlockSpec((ec, D_MODEL), lambda i, j: (j, 0)),
                  pl.BlockSpec((D_MODEL, ec), lambda i, j: (0, j)),
                  row_spec, row_spec, tile_spec, tile_spec],
        out_specs=pl.BlockSpec((tb, D_MODEL), lambda i, j: (i, 0)),
        out_shape=jax.ShapeDtypeStruct((t, D_MODEL), F32),
        scratch_shapes=[pltpu.VMEM((D_MODEL, tb), F32),
                        pltpu.VMEM((ec, tb), F32),
                        pltpu.VMEM((ec // 2, tb), jnp.uint32)],
        compiler_params=pltpu.CompilerParams(dimension_semantics=("parallel", "arbitrary"),
                                             vmem_limit_bytes=VMEM_LIMIT),
        name="peer",
    )(x1, xn2, u_bf16, vt_bf16, n1, w1, r2, w2)


def _block_diag_ones(width, group):
    idx = np.arange(width) // group
    return jnp.asarray((idx[:, None] == idx[None, :]).astype(np.float32), BF16)


def _layer(x2d, batch, seq, layer_idx, norm1_g, w_in, w_pool, swa_q_norm, swa_k_norm, swa_sinks, conv_w,
           diff_q_norm, diff_k_norm, lam_q1, lam_k1, lam_q2, lam_k2, out_norm_g, w_out, norm2_g, w_query,
           sub_keys, peer_u, peer_v):
    lam_init = 0.8 - 0.6 * float(np.exp(-0.3 * layer_idx))
    bd64 = _block_diag_ones(GROUP_WIDTH, HEAD_DIM)
    bd32 = _block_diag_ones(GROUP_WIDTH, DIFF_QK_DIM)
    wpool_bd = jax.scipy.linalg.block_diag(*[w_pool[g] for g in range(len(POOL_WINDOWS))]).astype(BF16)

    z = _in_proj(x2d, norm1_g.reshape(1, D_MODEL), w_in.astype(BF16))
    yabc, qdn, kdn = _local_mix(
        z, batch, seq, swa_sinks.astype(F32), wpool_bd, bd64, bd32,
        jnp.tile(swa_q_norm, SWA_Q_HEADS).reshape(1, GROUP_WIDTH),
        jnp.tile(swa_k_norm, 2).reshape(1, 128),
        conv_w, out_norm_g[:3 * GROUP_WIDTH].reshape(1, 3 * GROUP_WIDTH),
        jnp.tile(diff_q_norm, 2 * DIFF_HEADS).reshape(1, GROUP_WIDTH),
        jnp.tile(diff_k_norm, 2 * DIFF_HEADS).reshape(1, GROUP_WIDTH))
    lamv = jnp.stack([lam_q1, lam_k1, lam_q2, lam_k2]).astype(F32)
    yd = _diff_attn(qdn, kdn, z, batch, seq, lamv, bd64,
                    out_norm_g[3 * GROUP_WIDTH:].reshape(1, GROUP_WIDTH), lam_init)
    w_out_bf16 = w_out.astype(BF16)
    x1, xn2, n1, w1, r2, w2 = _out_router(
        yabc, yd, x2d, w_out_bf16[:3 * GROUP_WIDTH], w_out_bf16[3 * GROUP_WIDTH:],
        norm2_g.reshape(1, D_MODEL), w_query.astype(BF16), sub_keys.astype(BF16))
    return _peer(x1, xn2, peer_u.astype(BF16), peer_v.T.astype(BF16), n1, w1, r2, w2)


def kernel(x, norm1_g, w_in, w_pool, swa_q_norm, swa_k_norm, swa_sinks, conv_w, diff_q_norm, diff_k_norm,
           lam_q1, lam_k1, lam_q2, lam_k2, out_norm_g, w_out, norm2_g, w_query, sub_keys, peer_u, peer_v):
    batch, seq, d = x.shape
    x2d = x.reshape(batch * seq, d)
    depth = norm1_g.shape[0]
    for l in range(depth):
        x2d = _layer(x2d, batch, seq, l, norm1_g[l], w_in[l], w_pool[l], swa_q_norm[l], swa_k_norm[l],
                     swa_sinks[l], conv_w[l], diff_q_norm[l], diff_k_norm[l], lam_q1[l], lam_k1[l],
                     lam_q2[l], lam_k2[l], out_norm_g[l], w_out[l], norm2_g[l], w_query[l], sub_keys[l],
                     peer_u[l], peer_v[l])
    return x2d.reshape(batch, seq, d)
```

```python
import functools

import jax
import jax.numpy as jnp
import numpy as np
from jax import lax
from jax.experimental import pallas as pl
from jax.experimental.pallas import tpu as pltpu

F32 = jnp.float32
BF16 = jnp.bfloat16

D_MODEL = 1024
EPS = 1e-6
HEAD_DIM = 64
GROUP_WIDTH = 256
POOL_WINDOWS = (2, 4, 8, 16)
SWA_Q_HEADS = 4
SWA_GQA = 2
SWA_WINDOW = 128
DIFF_HEADS = 4
DIFF_QK_DIM = 32
IN_PROJ_WIDTH = 2304
PEER_HEADS = 8
PEER_KEYS = 128
PEER_EXPERTS = PEER_KEYS * PEER_KEYS
PEER_HALF = 128
PEER_TOPK = 16
ALIBI_SLOPES = tuple(2.0 ** (-(i + 1)) for i in range(SWA_Q_HEADS + DIFF_HEADS))

NEG = -1e30
LOG2_E = 1.4426950408889634
HALO = 16
ROW_BLOCK = 128
DIFF_Q_BLOCK = 256
IN_PROJ_ROWS = 1024
ROUTER_ROWS = 512
ROUTER_LANES = 256
PEER_TOKENS = 512
PEER_EXPERT_CHUNK = 2048
PEER_H_ROWS = 1024
VMEM_LIMIT = 56 * 1024 * 1024

NT_DIMS = (((1,), (1,)), ((), ()))


def _group_sumsq(x, bd):
    x2 = x * x
    hi = x2.astype(BF16)
    lo = (x2 - hi.astype(F32)).astype(BF16)
    return (jnp.dot(hi, bd, preferred_element_type=F32)
            + jnp.dot(lo, bd, preferred_element_type=F32))


def _group_rms_norm(x, bd, group, gain):
    return x * lax.rsqrt(_group_sumsq(x, bd) * (1.0 / group) + EPS) * gain


def _gelu(h):
    return 0.5 * h * (1.0 + lax.erf(h * np.float32(np.sqrt(0.5))))


def _in_proj_kernel(x_ref, g_ref, w_ref, z_ref):
    x = x_ref[...]
    xn = x * lax.rsqrt(jnp.mean(x * x, axis=-1, keepdims=True) + EPS) * g_ref[...]
    z = jnp.dot(xn.astype(BF16), w_ref[...], preferred_element_type=F32)
    z_ref[...] = z.astype(z_ref.dtype)


def _in_proj(x2d, g, w_bf16):
    t = x2d.shape[0]
    tm = min(IN_PROJ_ROWS, t)
    return pl.pallas_call(
        _in_proj_kernel,
        grid=(t // tm,),
        in_specs=[pl.BlockSpec((tm, D_MODEL), lambda i: (i, 0)),
                  pl.BlockSpec((1, D_MODEL), lambda i: (0, 0)),
                  pl.BlockSpec((D_MODEL, IN_PROJ_WIDTH), lambda i: (0, 0))],
        out_specs=pl.BlockSpec((tm, IN_PROJ_WIDTH), lambda i: (i, 0)),
        out_shape=jax.ShapeDtypeStruct((t, IN_PROJ_WIDTH), BF16),
        compiler_params=pltpu.CompilerParams(dimension_semantics=("parallel",),
                                             vmem_limit_bytes=VMEM_LIMIT),
        name="in_proj",
    )(x2d, g, w_bf16)


def _local_mix_kernel(sinks_ref, a_ref, a_halo_ref, bq_ref, bkv_ref, bkv_prev_ref, ch_ref, ch_halo_ref,
                      cb_ref, cc_ref, cc_halo_ref, dq_ref, dk_ref,
                      wpool_ref, bd64_ref, bd32_ref, gq_swa_ref, gk_swa_ref, conv_w_ref, g_out_ref,
                      gq_diff_ref, gk_diff_ref,
                      y_ref, qdn_ref, kdn_ref):
    n = pl.program_id(1)
    not_first = (n > 0).astype(F32)
    bd64 = bd64_ref[...]
    rows = lax.broadcasted_iota(jnp.int32, (ROW_BLOCK, GROUP_WIDTH), 0)
    cols = lax.broadcasted_iota(jnp.int32, (ROW_BLOCK, GROUP_WIDTH), 1)

    p = a_ref[...].astype(F32)
    p_ext = jnp.concatenate([a_halo_ref[...].astype(F32) * not_first, p], axis=0)
    s2 = p_ext + pltpu.roll(p_ext, 1, axis=0)
    s4 = s2 + pltpu.roll(s2, 2, axis=0)
    s8 = s4 + pltpu.roll(s4, 4, axis=0)
    s16 = s8 + pltpu.roll(s8, 8, axis=0)
    wsel = jnp.where(cols < 64, 2, jnp.where(cols < 128, 4, jnp.where(cols < 192, 8, 16)))
    ssel = jnp.where(cols < 64, s2[HALO:], jnp.where(cols < 128, s4[HALO:],
                                                      jnp.where(cols < 192, s8[HALO:], s16[HALO:])))
    cnt = jnp.minimum(n * ROW_BLOCK + rows + 1, wsel).astype(F32)
    d = ssel / cnt - p
    y_a = jnp.dot(d.astype(BF16), wpool_ref[...], preferred_element_type=F32)

    q = _group_rms_norm(bq_ref[...].astype(F32), bd64, HEAD_DIM, gq_swa_ref[...]) * (HEAD_DIM ** -0.5)
    qn = q.astype(BF16)
    bd64k = bd64[:128, :128]
    kv_cur = bkv_ref[...]
    kv_prev = bkv_prev_ref[...]
    k_cur = _group_rms_norm(kv_cur[:, :128].astype(F32), bd64k, HEAD_DIM, gk_swa_ref[...])
    k_prev = _group_rms_norm(kv_prev[:, :128].astype(F32), bd64k, HEAD_DIM, gk_swa_ref[...])
    kn = jnp.concatenate([k_prev, k_cur], axis=0).astype(BF16)
    v = jnp.concatenate([kv_prev[:, 128:], kv_cur[:, 128:]], axis=0)
    qi = lax.broadcasted_iota(jnp.int32, (ROW_BLOCK, 2 * ROW_BLOCK), 0)
    kj = lax.broadcasted_iota(jnp.int32, (ROW_BLOCK, 2 * ROW_BLOCK), 1)
    dist = qi + ROW_BLOCK - kj
    valid = (dist >= 0) & (dist < SWA_WINDOW) & ((n - 1) * ROW_BLOCK + kj >= 0)
    distf = dist.astype(F32)
    heads = []
    for h in range(SWA_Q_HEADS):
        hk = h // SWA_GQA
        sc = lax.dot_general(qn[:, h * 64:(h + 1) * 64], kn[:, hk * 64:(hk + 1) * 64], NT_DIMS,
                             preferred_element_type=F32)
        sc = jnp.where(valid, sc - ALIBI_SLOPES[h] * distf, NEG)
        sink = sinks_ref[h]
        mx = jnp.maximum(jnp.max(sc, axis=-1, keepdims=True), sink)
        e = jnp.exp(sc - mx)
        den = jnp.sum(e, axis=-1, keepdims=True) + jnp.exp(sink - mx)
        o = jnp.dot(e.astype(BF16), v[:, hk * 64:(hk + 1) * 64], preferred_element_type=F32)
        heads.append(o / den)
    y_b = jnp.concatenate(heads, axis=-1)

    zc = cc_ref[...].astype(F32) * ch_ref[...].astype(F32)
    zc_halo = cc_halo_ref[...].astype(F32) * ch_halo_ref[...].astype(F32) * not_first
    zc_ext = jnp.concatenate([zc_halo, zc], axis=0)
    cw = conv_w_ref[...]
    conv = (cw[0:1] * pltpu.roll(zc_ext, 2, axis=0)[HALO:] + cw[1:2] * pltpu.roll(zc_ext, 1, axis=0)[HALO:]
            + cw[2:3] * zc)
    y_c = cb_ref[...].astype(F32) * conv

    g_out = g_out_ref[...]
    for gi, y in enumerate((y_a, y_b, y_c)):
        lo, hi = gi * GROUP_WIDTH, (gi + 1) * GROUP_WIDTH
        y_ref[:, lo:hi] = _group_rms_norm(y, bd64, HEAD_DIM, g_out[:, lo:hi]).astype(y_ref.dtype)

    bd32 = bd32_ref[...]
    qd = (_group_rms_norm(dq_ref[...].astype(F32), bd32, DIFF_QK_DIM, gq_diff_ref[...])
          * (DIFF_QK_DIM ** -0.5 * LOG2_E))
    kd = _group_rms_norm(dk_ref[...].astype(F32), bd32, DIFF_QK_DIM, gk_diff_ref[...])
    qdn_ref[...] = qd.astype(qdn_ref.dtype)
    kdn_ref[...] = kd.astype(kdn_ref.dtype)


def _local_mix(z, batch, seq, sinks, wpool_bd, bd64, bd32, gq_swa, gk_swa, conv_w, g_out, gq_diff, gk_diff):
    t = batch * seq
    nb = seq // ROW_BLOCK
    halo_per_block = ROW_BLOCK // HALO

    def cur(col):
        return pl.BlockSpec((ROW_BLOCK, GROUP_WIDTH), lambda b, n: (b * nb + n, col))

    def prev(col):
        return pl.BlockSpec((ROW_BLOCK, GROUP_WIDTH), lambda b, n: (b * nb + jnp.maximum(n - 1, 0), col))

    def halo(col):
        return pl.BlockSpec(
            (HALO, GROUP_WIDTH),
            lambda b, n: (jnp.maximum((b * nb + n) * halo_per_block - 1, 0), col))

    def whole(arr):
        return pl.BlockSpec(arr.shape, lambda b, n: (0,) * arr.ndim)

    params = (wpool_bd, bd64, bd32, gq_swa, gk_swa, conv_w, g_out, gq_diff, gk_diff)
    in_specs = ([pl.BlockSpec(memory_space=pltpu.SMEM),
                 cur(0), halo(0), cur(1), cur(2), prev(2), cur(3), halo(3), cur(4), cur(5), halo(5),
                 cur(6), cur(7)] + [whole(a) for a in params])
    out_specs = [pl.BlockSpec((ROW_BLOCK, 3 * GROUP_WIDTH), lambda b, n: (b * nb + n, 0)),
                 pl.BlockSpec((ROW_BLOCK, GROUP_WIDTH), lambda b, n: (b * nb + n, 0)),
                 pl.BlockSpec((ROW_BLOCK, GROUP_WIDTH), lambda b, n: (b * nb + n, 0))]
    out_shape = [jax.ShapeDtypeStruct((t, 3 * GROUP_WIDTH), BF16),
                 jax.ShapeDtypeStruct((t, GROUP_WIDTH), BF16),
                 jax.ShapeDtypeStruct((t, GROUP_WIDTH), BF16)]
    return pl.pallas_call(
        _local_mix_kernel,
        grid=(batch, nb),
        in_specs=in_specs,
        out_specs=out_specs,
        out_shape=out_shape,
        compiler_params=pltpu.CompilerParams(dimension_semantics=("parallel", "parallel"),
                                             vmem_limit_bytes=VMEM_LIMIT),
        name="local_mix",
    )(sinks, *([z] * 12), *params)


def _diff_attn_kernel(q_ref, k_ref, v_ref, lamv_ref, bd64_ref, g_ref, y_ref,
                      qs_ref, vt_ref, m_ref, den_ref, acc_ref, *, one_minus_lam_init, lam_init):
    qi = pl.program_id(1)
    qb = DIFF_Q_BLOCK
    lamv = lamv_ref[...]
    lam = (jnp.exp(jnp.sum(lamv[0:1] * lamv[1:2], axis=-1, keepdims=True))
           - jnp.exp(jnp.sum(lamv[2:3] * lamv[3:4], axis=-1, keepdims=True)) + lam_init)
    n_maps = 2 * DIFF_HEADS

    @pl.when(qi == 0)
    def _():
        for c in range(vt_ref.shape[0]):
            vt_ref[c] = v_ref[c * qb:(c + 1) * qb, :].astype(F32).T.astype(BF16)

    q = q_ref[...]
    lane_group = lax.broadcasted_iota(jnp.int32, (qb, GROUP_WIDTH), 1) // DIFF_QK_DIM
    for g in range(n_maps):
        qs_ref[g * qb:(g + 1) * qb, :] = jnp.where(lane_group == g, q, jnp.zeros_like(q))
    m_ref[...] = jnp.full(m_ref.shape, NEG, F32)
    den_ref[...] = jnp.zeros(den_ref.shape, F32)
    acc_ref[...] = jnp.zeros(acc_ref.shape, F32)
    rel = (lax.broadcasted_iota(jnp.int32, (qb, qb), 1) - lax.broadcasted_iota(jnp.int32, (qb, qb), 0))
    relf = rel.astype(F32)
    slopes = [ALIBI_SLOPES[SWA_Q_HEADS + h] * LOG2_E for h in range(DIFF_HEADS)]
    bias = [slopes[h] * relf for h in range(DIFF_HEADS)]

    def kv_step(c, diagonal):
        start = pl.multiple_of(c * qb, qb)
        sc_all = lax.dot_general(k_ref[pl.ds(start, qb), :], qs_ref[...], NT_DIMS,
                                 preferred_element_type=F32)
        off = ((qi - c) * qb).astype(F32)
        for g in range(n_maps):
            cols = slice(g * qb, (g + 1) * qb)
            sc = sc_all[:, cols] - bias[g // 2]
            if diagonal:
                sc = jnp.where(rel >= 0, sc, NEG)
            shift = slopes[g // 2] * off
            m_old = m_ref[:, cols]
            m_new = jnp.maximum(m_old, jnp.max(sc, axis=0, keepdims=True) - shift)
            alpha = jnp.exp2(m_old - m_new)
            e = jnp.exp2(sc - (m_new + shift))
            den_ref[:, cols] = alpha * den_ref[:, cols] + jnp.sum(e, axis=0, keepdims=True)
            m_ref[:, cols] = m_new
            rows = slice((g // 2) * HEAD_DIM, (g // 2 + 1) * HEAD_DIM)
            pv = jnp.dot(vt_ref[c, rows, :], e.astype(BF16), preferred_element_type=F32)
            acc_ref[:, cols] = alpha * acc_ref[:, cols] + pv

    def off_diagonal(c, carry):
        kv_step(c, False)
        return carry

    lax.fori_loop(0, qi, off_diagonal, 0)
    kv_step(qi, True)

    heads = []
    for h in range(DIFF_HEADS):
        c1 = slice(2 * h * qb, (2 * h + 1) * qb)
        c2 = slice((2 * h + 1) * qb, (2 * h + 2) * qb)
        heads.append(acc_ref[:, c1] / den_ref[:, c1] - lam * (acc_ref[:, c2] / den_ref[:, c2]))
    o = jnp.concatenate(heads, axis=0).T
    y = _group_rms_norm(o, bd64_ref[...], HEAD_DIM, g_ref[...]) * one_minus_lam_init
    y_ref[...] = y.astype(y_ref.dtype)


def _diff_attn(qdn, kdn, z, batch, seq, lamv, bd64, g_d, lam_init):
    t = batch * seq
    nq = seq // DIFF_Q_BLOCK
    v_col = 8
    stacked = 2 * DIFF_HEADS * DIFF_Q_BLOCK
    kernel = functools.partial(_diff_attn_kernel, one_minus_lam_init=1.0 - lam_init, lam_init=lam_init)
    return pl.pallas_call(
        kernel,
        grid=(batch, nq),
        in_specs=[pl.BlockSpec((DIFF_Q_BLOCK, GROUP_WIDTH), lambda b, i: (b * nq + i, 0)),
                  pl.BlockSpec((seq, GROUP_WIDTH), lambda b, i: (b, 0)),
                  pl.BlockSpec((seq, GROUP_WIDTH), lambda b, i: (b, v_col)),
                  pl.BlockSpec(lamv.shape, lambda b, i: (0, 0)),
                  pl.BlockSpec(bd64.shape, lambda b, i: (0, 0)),
                  pl.BlockSpec(g_d.shape, lambda b, i: (0, 0))],
        out_specs=pl.BlockSpec((DIFF_Q_BLOCK, GROUP_WIDTH), lambda b, i: (b * nq + i, 0)),
        out_shape=jax.ShapeDtypeStruct((t, GROUP_WIDTH), BF16),
        scratch_shapes=[pltpu.VMEM((stacked, GROUP_WIDTH), BF16),
                        pltpu.VMEM((seq // DIFF_Q_BLOCK, GROUP_WIDTH, DIFF_Q_BLOCK), BF16),
                        pltpu.VMEM((1, stacked), F32),
                        pltpu.VMEM((1, stacked), F32),
                        pltpu.VMEM((HEAD_DIM, stacked), F32)],
        compiler_params=pltpu.CompilerParams(dimension_semantics=("parallel", "arbitrary"),
                                             vmem_limit_bytes=VMEM_LIMIT),
        name="diff_attn",
    )(qdn, kdn, z, lamv, bd64, g_d)


def _top_values(s, count):
    vals = []
    cur = s
    for r in range(count):
        m = jnp.max(cur, axis=0, keepdims=True)
        vals.append(m)
        if r + 1 < count:
            cur = jnp.where(cur >= m, NEG, cur)
    return vals


def _odd_even_merge_sort_pairs(n):
    pairs = []

    def merge(lo, length, r):
        step = 2 * r
        if step < length:
            merge(lo, length, step)
            merge(lo + r, length, step)
            pairs.extend((i, i + r) for i in range(lo + r, lo + length - r, step))
        else:
            pairs.append((lo, lo + r))

    def sort(lo, length):
        if length > 1:
            sort(lo, length // 2)
            sort(lo + length // 2, length // 2)
            merge(lo, length, 1)

    sort(0, n)
    return pairs


_SORT16 = _odd_even_merge_sort_pairs(PEER_TOPK)


def _top16_sorted(s):
    n = PEER_TOPK
    a = [s[8 * j:8 * j + 8] for j in range(n)]
    for i, j in _SORT16:
        a[i], a[j] = jnp.maximum(a[i], a[j]), jnp.minimum(a[i], a[j])
    for shift in (4, 2, 1):
        b = [pltpu.roll(x, shift, axis=0) for x in a]
        a = [jnp.maximum(a[j], b[n - 1 - j]) for j in range(n)]
        d = n // 2
        while d:
            for i in range(n):
                if not i & d:
                    a[i], a[i + d] = jnp.maximum(a[i], a[i + d]), jnp.minimum(a[i], a[i + d])
            d //= 2
    return a


def _sublane_pick(arrays):
    row = lax.broadcasted_iota(jnp.int32, arrays[0].shape, 0)
    out = arrays[0]
    for s in range(1, 8):
        out = jnp.where(row == s, arrays[s], out)
    return out


def _dup_bf16_words(x):
    bits = pltpu.bitcast(x.astype(BF16).astype(F32), jnp.uint32)
    return bits | (bits >> 16)


def _stack_rows(rows_list):
    n = len(rows_list)
    lanes = rows_list[0].shape[-1]
    ri = lax.broadcasted_iota(jnp.int32, (n, lanes), 0)
    out = jnp.zeros((n, lanes), F32)
    for r, row in enumerate(rows_list):
        out = jnp.where(ri == r, row, out)
    return out


def _out_router_kernel(yabc_ref, yd_ref, x_ref, wo1_ref, wo2_ref, g2_ref, wq_ref, keys_ref,
                       x1_ref, xn2_ref, n1_ref, w1_ref, r2_ref, w2_ref):
    acc = (jnp.dot(yabc_ref[...], wo1_ref[...], preferred_element_type=F32)
           + jnp.dot(yd_ref[...], wo2_ref[...], preferred_element_type=F32))
    x1 = x_ref[...] + acc
    x1_ref[...] = x1
    xn2 = (x1 * lax.rsqrt(jnp.mean(x1 * x1, axis=-1, keepdims=True) + EPS) * g2_ref[...]).astype(BF16)
    xn2_ref[...] = pltpu.bitcast(xn2, jnp.uint32)
    q = jnp.dot(xn2, wq_ref[...], preferred_element_type=F32)
    tm = q.shape[0]
    for h in range(PEER_HEADS):
        scores = []
        for c in range(2):
            lo = (2 * h + c) * PEER_HALF
            qhc = q[:, lo:lo + PEER_HALF].astype(BF16)
            scores.append(lax.dot_general(keys_ref[c], qhc, NT_DIMS, preferred_element_type=F32))
        for p in range(tm // ROUTER_LANES):
            ls = slice(p * ROUTER_LANES, (p + 1) * ROUTER_LANES)
            _route_head(scores[0][:, ls], scores[1][:, ls],
                        n1_ref.at[h, :, ls], w1_ref.at[h, :, ls], r2_ref.at[h, :, ls], w2_ref.at[h, :, ls])


def _route_head(s1, s2, n1_ref, w1_ref, r2_ref, w2_ref):
    lanes = s1.shape[1]
    row8 = lax.broadcasted_iota(jnp.int32, (8, lanes), 0)
    t1 = _top16_sorted(s1)
    t2 = _top16_sorted(s2)
    v1_lo, v1_hi, v2_hi = _sublane_pick(t1[:8]), _sublane_pick(t1[8:]), _sublane_pick(t2[8:])
    groups = [v1_lo + t2[0], v1_hi + t2[0]]
    for b in range(1, 8):
        groups.append(jnp.where(row8 < PEER_TOPK // (b + 1), v1_lo + t2[b], NEG))
    groups.append(t1[0] + v2_hi)
    best = _top_values(jnp.concatenate(groups, axis=0), PEER_TOPK)
    tau = jnp.broadcast_to(best[PEER_TOPK - 1], (8, lanes))
    zsum = jnp.sum(jnp.exp(_stack_rows(best) - best[0]), axis=0, keepdims=True)
    inv_z = jnp.broadcast_to(1.0 / zsum, (8, lanes))
    for j in range(PEER_KEYS // 16):
        words = slice(8 * j, 8 * j + 8)
        rank2, w2 = [], []
        for half in range(2):
            ks = slice(16 * j + 8 * half, 16 * j + 8 * half + 8)
            s1k, s2k = s1[ks], s2[ks]
            n1 = jnp.zeros((8, lanes), F32)
            rk = jnp.full((8, lanes), float(PEER_TOPK), F32)
            for b in reversed(range(PEER_TOPK)):
                n1 = n1 + jnp.where(s1k + t2[b] >= tau, 1.0, 0.0)
                rk = jnp.where(s2k >= t2[b], float(b), rk)
            n1_ref[ks, :] = _dup_bf16_words(n1)
            w1_ref[ks, :] = _dup_bf16_words(jnp.exp(s1k - t1[0]) * inv_z)
            rank2.append(rk)
            w2.append(jnp.exp(s2k - t2[0]))
        r2_ref[words, :] = pltpu.bitcast(jnp.concatenate(rank2, axis=0).astype(BF16), jnp.uint32)
        w2_ref[words, :] = pltpu.bitcast(jnp.concatenate(w2, axis=0).astype(BF16), jnp.uint32)


def _out_router(yabc, yd, x2d, wo1, wo2, g2, wq, keys):
    t = x2d.shape[0]
    tm = min(ROUTER_ROWS, t)

    def whole(arr):
        return pl.BlockSpec(arr.shape, lambda i: (0,) * arr.ndim)

    row_spec = pl.BlockSpec((PEER_HEADS, PEER_KEYS, tm), lambda i: (0, 0, i))
    row_shape = jax.ShapeDtypeStruct((PEER_HEADS, PEER_KEYS, t), jnp.uint32)
    tile_spec = pl.BlockSpec((PEER_HEADS, PEER_KEYS // 2, tm), lambda i: (0, 0, i))
    tile_shape = jax.ShapeDtypeStruct((PEER_HEADS, PEER_KEYS // 2, t), jnp.uint32)
    return pl.pallas_call(
        _out_router_kernel,
        grid=(t // tm,),
        in_specs=[pl.BlockSpec((tm, 3 * GROUP_WIDTH), lambda i: (i, 0)),
                  pl.BlockSpec((tm, GROUP_WIDTH), lambda i: (i, 0)),
                  pl.BlockSpec((tm, D_MODEL), lambda i: (i, 0)),
                  whole(wo1), whole(wo2), whole(g2), whole(wq), whole(keys)],
        out_specs=[pl.BlockSpec((tm, D_MODEL), lambda i: (i, 0)),
                   pl.BlockSpec((tm // 2, D_MODEL), lambda i: (i, 0)),
                   row_spec, row_spec, tile_spec, tile_spec],
        out_shape=[jax.ShapeDtypeStruct((t, D_MODEL), F32),
                   jax.ShapeDtypeStruct((t // 2, D_MODEL), jnp.uint32),
                   row_shape, row_shape, tile_shape, tile_shape],
        compiler_params=pltpu.CompilerParams(dimension_semantics=("parallel",),
                                             vmem_limit_bytes=VMEM_LIMIT),
        name="out_router",
    )(yabc, yd, x2d, wo1, wo2, g2, wq, keys)


def _gate_activations(h_ref, a_ref, n1_ref, w1_ref, r2_ref, w2_ref, first_row, rows):
    tb = h_ref.shape[1]
    zero = jnp.zeros((), BF16)
    half = PEER_KEYS // 2
    for lc in range(tb // 128):
        ls = slice(lc * 128, (lc + 1) * 128)
        n1_rows = [n1_ref[hd, pl.ds(first_row, rows), ls] for hd in range(PEER_HEADS)]
        w1_rows = [w1_ref[hd, pl.ds(first_row, rows), ls] for hd in range(PEER_HEADS)]
        for r in range(rows):
            w = None
            for hd in range(PEER_HEADS):
                n1b = pltpu.bitcast(jnp.broadcast_to(n1_rows[hd][r:r + 1], (half, 128)), BF16)
                w1b = pltpu.bitcast(jnp.broadcast_to(w1_rows[hd][r:r + 1], (half, 128)), BF16)
                rank2 = pltpu.bitcast(r2_ref[hd, :, ls], BF16)
                w2 = pltpu.bitcast(w2_ref[hd, :, ls], BF16)
                term = jnp.where(rank2 < n1b, w2 * w1b, zero)
                w = term if w is None else w + term
            g = _gelu(h_ref[r * PEER_KEYS:(r + 1) * PEER_KEYS, ls]).astype(BF16)
            a_ref[r * half:(r + 1) * half, ls] = pltpu.bitcast(w * g, jnp.uint32)


def _peer_kernel(x_ref, xn_ref, u_ref, vt_ref, n1_ref, w1_ref, r2_ref, w2_ref, o_ref,
                 acc_ref, h_ref, a_ref, *, rows):
    j = pl.program_id(1)

    @pl.when(j == 0)
    def _():
        acc_ref[...] = jnp.zeros_like(acc_ref)

    xn = pltpu.bitcast(xn_ref[...], BF16)
    for m in range(0, u_ref.shape[0], PEER_H_ROWS):
        h_ref[m:m + PEER_H_ROWS, :] = lax.dot_general(u_ref[m:m + PEER_H_ROWS, :], xn, NT_DIMS,
                                                      preferred_element_type=F32)
    first_row = pl.multiple_of(j * rows, rows)
    _gate_activations(h_ref, a_ref, n1_ref, w1_ref, r2_ref, w2_ref, first_row, rows)
    acc_ref[...] += jnp.dot(vt_ref[...], pltpu.bitcast(a_ref[...], BF16), preferred_element_type=F32)

    @pl.when(j == pl.num_programs(1) - 1)
    def _():
        o_ref[...] = x_ref[...] + acc_ref[...].T


def _peer(x1, xn2, u_bf16, vt_bf16, n1, w1, r2, w2):
    t = x1.shape[0]
    tb = min(PEER_TOKENS, t)
    ec = PEER_EXPERT_CHUNK
    rows = ec // PEER_KEYS
    assert rows % 8 == 0 and tb % 128 == 0 and t % tb == 0 and PEER_EXPERTS % ec == 0
    row_spec = pl.BlockSpec((PEER_HEADS, PEER_KEYS, tb), lambda i, j: (0, 0, i))
    tile_spec = pl.BlockSpec((PEER_HEADS, PEER_KEYS // 2, tb), lambda i, j: (0, 0, i))
    kernel = functools.partial(_peer_kernel, rows=rows)
    return pl.pallas_call(
        kernel,
        grid=(t // tb, PEER_EXPERTS // ec),
        in_specs=[pl.BlockSpec((tb, D_MODEL), lambda i, j: (i, 0)),
                  pl.BlockSpec((tb // 2, D_MODEL), lambda i, j: (i, 0)),
                  pl.BlockSpec((ec, D_MODEL), lambda i, j: (j, 0)),
                  pl.BlockSpec((D_MODEL, ec), lambda i, j: (0, j)),
                  row_spec, row_spec, tile_spec, tile_spec],
        out_specs=pl.BlockSpec((tb, D_MODEL), lambda i, j: (i, 0)),
        out_shape=jax.ShapeDtypeStruct((t, D_MODEL), F32),
        scratch_shapes=[pltpu.VMEM((D_MODEL, tb), F32),
                        pltpu.VMEM((ec, tb), F32),
                        pltpu.VMEM((ec // 2, tb), jnp.uint32)],
        compiler_params=pltpu.CompilerParams(dimension_semantics=("parallel", "arbitrary"),
                                             vmem_limit_bytes=VMEM_LIMIT),
        name="peer",
    )(x1, xn2, u_bf16, vt_bf16, n1, w1, r2, w2)


def _block_diag_ones(width, group):
    idx = np.arange(width) // group
    return jnp.asarray((idx[:, None] == idx[None, :]).astype(np.float32), BF16)


def _layer(x2d, batch, seq, layer_idx, norm1_g, w_in, w_pool, swa_q_norm, swa_k_norm, swa_sinks, conv_w,
           diff_q_norm, diff_k_norm, lam_q1, lam_k1, lam_q2, lam_k2, out_norm_g, w_out, norm2_g, w_query,
           sub_keys, peer_u, peer_v):
    lam_init = 0.8 - 0.6 * float(np.exp(-0.3 * layer_idx))
    bd64 = _block_diag_ones(GROUP_WIDTH, HEAD_DIM)
    bd32 = _block_diag_ones(GROUP_WIDTH, DIFF_QK_DIM)
    wpool_bd = jax.scipy.linalg.block_diag(*[w_pool[g] for g in range(len(POOL_WINDOWS))]).astype(BF16)

    z = _in_proj(x2d, norm1_g.reshape(1, D_MODEL), w_in.astype(BF16))
    yabc, qdn, kdn = _local_mix(
        z, batch, seq, swa_sinks.astype(F32), wpool_bd, bd64, bd32,
        jnp.tile(swa_q_norm, SWA_Q_HEADS).reshape(1, GROUP_WIDTH),
        jnp.tile(swa_k_norm, 2).reshape(1, 128),
        conv_w, out_norm_g[:3 * GROUP_WIDTH].reshape(1, 3 * GROUP_WIDTH),
        jnp.tile(diff_q_norm, 2 * DIFF_HEADS).reshape(1, GROUP_WIDTH),
        jnp.tile(diff_k_norm, 2 * DIFF_HEADS).reshape(1, GROUP_WIDTH))
    lamv = jnp.stack([lam_q1, lam_k1, lam_q2, lam_k2]).astype(F32)
    yd = _diff_attn(qdn, kdn, z, batch, seq, lamv, bd64,
                    out_norm_g[3 * GROUP_WIDTH:].reshape(1, GROUP_WIDTH), lam_init)
    w_out_bf16 = w_out.astype(BF16)
    x1, xn2, n1, w1, r2, w2 = _out_router(
        yabc, yd, x2d, w_out_bf16[:3 * GROUP_WIDTH], w_out_bf16[3 * GROUP_WIDTH:],
        norm2_g.reshape(1, D_MODEL), w_query.astype(BF16), sub_keys.astype(BF16))
    return _peer(x1, xn2, peer_u.astype(BF16), peer_v.T.astype(BF16), n1, w1, r2, w2)


def kernel(x, norm1_g, w_in, w_pool, swa_q_norm, swa_k_norm, swa_sinks, conv_w, diff_q_norm, diff_k_norm,
           lam_q1, lam_k1, lam_q2, lam_k2, out_norm_g, w_out, norm2_g, w_query, sub_keys, peer_u, peer_v):
    batch, seq, d = x.shape
    x2d = x.reshape(batch * seq, d)
    depth = norm1_g.shape[0]
    for l in range(depth):
        x2d = _layer(x2d, batch, seq, l, norm1_g[l], w_in[l], w_pool[l], swa_q_norm[l], swa_k_norm[l],
                     swa_sinks[l], conv_w[l], diff_q_norm[l], diff_k_norm[l], lam_q1[l], lam_k1[l],
                     lam_q2[l], lam_k2[l], out_norm_g[l], w_out[l], norm2_g[l], w_query[l], sub_keys[l],
                     peer_u[l], peer_v[l])
    return x2d.reshape(batch, seq, d)
```

```python
import functools

import jax
import jax.numpy as jnp
import numpy as np
from jax import lax
from jax.experimental import pallas as pl
from jax.experimental.pallas import tpu as pltpu

F32 = jnp.float32
BF16 = jnp.bfloat16

D_MODEL = 1024
EPS = 1e-6
HEAD_DIM = 64
GROUP_WIDTH = 256
POOL_WINDOWS = (2, 4, 8, 16)
SWA_Q_HEADS = 4
SWA_GQA = 2
SWA_WINDOW = 128
DIFF_HEADS = 4
DIFF_QK_DIM = 32
IN_PROJ_WIDTH = 2304
PEER_HEADS = 8
PEER_KEYS = 128
PEER_EXPERTS = PEER_KEYS * PEER_KEYS
PEER_HALF = 128
PEER_TOPK = 16
ALIBI_SLOPES = tuple(2.0 ** (-(i + 1)) for i in range(SWA_Q_HEADS + DIFF_HEADS))

NEG = -1e30
LOG2_E = 1.4426950408889634
HALO = 16
ROW_BLOCK = 128
DIFF_Q_BLOCK = 256
IN_PROJ_ROWS = 1024
ROUTER_ROWS = 512
ROUTER_LANES = 256
PEER_TOKENS = 512
PEER_EXPERT_CHUNK = 2048
PEER_H_ROWS = 1024
VMEM_LIMIT = 56 * 1024 * 1024

NT_DIMS = (((1,), (1,)), ((), ()))


def _group_sumsq(x, bd):
    x2 = x * x
    hi = x2.astype(BF16)
    lo = (x2 - hi.astype(F32)).astype(BF16)
    return (jnp.dot(hi, bd, preferred_element_type=F32)
            + jnp.dot(lo, bd, preferred_element_type=F32))


def _group_rms_norm(x, bd, group, gain):
    return x * lax.rsqrt(_group_sumsq(x, bd) * (1.0 / group) + EPS) * gain


def _gelu(h):
    return 0.5 * h * (1.0 + lax.erf(h * np.float32(np.sqrt(0.5))))


def _in_proj_kernel(x_ref, g_ref, w_ref, z_ref):
    x = x_ref[...]
    xn = x * lax.rsqrt(jnp.mean(x * x, axis=-1, keepdims=True) + EPS) * g_ref[...]
    z = jnp.dot(xn.astype(BF16), w_ref[...], preferred_element_type=F32)
    z_ref[...] = z.astype(z_ref.dtype)


def _in_proj(x2d, g, w_bf16):
    t = x2d.shape[0]
    tm = min(IN_PROJ_ROWS, t)
    return pl.pallas_call(
        _in_proj_kernel,
        grid=(t // tm,),
        in_specs=[pl.BlockSpec((tm, D_MODEL), lambda i: (i, 0)),
                  pl.BlockSpec((1, D_MODEL), lambda i: (0, 0)),
                  pl.BlockSpec((D_MODEL, IN_PROJ_WIDTH), lambda i: (0, 0))],
        out_specs=pl.BlockSpec((tm, IN_PROJ_WIDTH), lambda i: (i, 0)),
        out_shape=jax.ShapeDtypeStruct((t, IN_PROJ_WIDTH), BF16),
        compiler_params=pltpu.CompilerParams(dimension_semantics=("parallel",),
                                             vmem_limit_bytes=VMEM_LIMIT),
        name="in_proj",
    )(x2d, g, w_bf16)


def _local_mix_kernel(sinks_ref, a_ref, a_halo_ref, bq_ref, bkv_ref, bkv_prev_ref, ch_ref, ch_halo_ref,
                      cb_ref, cc_ref, cc_halo_ref, dq_ref, dk_ref,
                      wpool_ref, bd64_ref, bd32_ref, gq_swa_ref, gk_swa_ref, conv_w_ref, g_out_ref,
                      gq_diff_ref, gk_diff_ref,
                      y_ref, qdn_ref, kdn_ref):
    n = pl.program_id(1)
    not_first = (n > 0).astype(F32)
    bd64 = bd64_ref[...]
    rows = lax.broadcasted_iota(jnp.int32, (ROW_BLOCK, GROUP_WIDTH), 0)
    cols = lax.broadcasted_iota(jnp.int32, (ROW_BLOCK, GROUP_WIDTH), 1)

    p = a_ref[...].astype(F32)
    p_ext = jnp.concatenate([a_halo_ref[...].astype(F32) * not_first, p], axis=0)
    s2 = p_ext + pltpu.roll(p_ext, 1, axis=0)
    s4 = s2 + pltpu.roll(s2, 2, axis=0)
    s8 = s4 + pltpu.roll(s4, 4, axis=0)
    s16 = s8 + pltpu.roll(s8, 8, axis=0)
    wsel = jnp.where(cols < 64, 2, jnp.where(cols < 128, 4, jnp.where(cols < 192, 8, 16)))
    ssel = jnp.where(cols < 64, s2[HALO:], jnp.where(cols < 128, s4[HALO:],
                                                      jnp.where(cols < 192, s8[HALO:], s16[HALO:])))
    cnt = jnp.minimum(n * ROW_BLOCK + rows + 1, wsel).astype(F32)
    d = ssel / cnt - p
    y_a = jnp.dot(d.astype(BF16), wpool_ref[...], preferred_element_type=F32)

    q = _group_rms_norm(bq_ref[...].astype(F32), bd64, HEAD_DIM, gq_swa_ref[...]) * (HEAD_DIM ** -0.5)
    qn = q.astype(BF16)
    bd64k = bd64[:128, :128]
    kv_cur = bkv_ref[...]
    kv_prev = bkv_prev_ref[...]
    k_cur = _group_rms_norm(kv_cur[:, :128].astype(F32), bd64k, HEAD_DIM, gk_swa_ref[...])
    k_prev = _group_rms_norm(kv_prev[:, :128].astype(F32), bd64k, HEAD_DIM, gk_swa_ref[...])
    kn = jnp.concatenate([k_prev, k_cur], axis=0).astype(BF16)
    v = jnp.concatenate([kv_prev[:, 128:], kv_cur[:, 128:]], axis=0)
    qi = lax.broadcasted_iota(jnp.int32, (ROW_BLOCK, 2 * ROW_BLOCK), 0)
    kj = lax.broadcasted_iota(jnp.int32, (ROW_BLOCK, 2 * ROW_BLOCK), 1)
    dist = qi + ROW_BLOCK - kj
    valid = (dist >= 0) & (dist < SWA_WINDOW) & ((n - 1) * ROW_BLOCK + kj >= 0)
    distf = dist.astype(F32)
    heads = []
    for h in range(SWA_Q_HEADS):
        hk = h // SWA_GQA
        sc = lax.dot_general(qn[:, h * 64:(h + 1) * 64], kn[:, hk * 64:(hk + 1) * 64], NT_DIMS,
                             preferred_element_type=F32)
        sc = jnp.where(valid, sc - ALIBI_SLOPES[h] * distf, NEG)
        sink = sinks_ref[h]
        mx = jnp.maximum(jnp.max(sc, axis=-1, keepdims=True), sink)
        e = jnp.exp(sc - mx)
        den = jnp.sum(e, axis=-1, keepdims=True) + jnp.exp(sink - mx)
        o = jnp.dot(e.astype(BF16), v[:, hk * 64:(hk + 1) * 64], preferred_element_type=F32)
        heads.append(o / den)
    y_b = jnp.concatenate(heads, axis=-1)

    zc = cc_ref[...].astype(F32) * ch_ref[...].astype(F32)
    zc_halo = cc_halo_ref[...].astype(F32) * ch_halo_ref[...].astype(F32) * not_first
    zc_ext = jnp.concatenate([zc_halo, zc], axis=0)
    cw = conv_w_ref[...]
    conv = (cw[0:1] * pltpu.roll(zc_ext, 2, axis=0)[HALO:] + cw[1:2] * pltpu.roll(zc_ext, 1, axis=0)[HALO:]
            + cw[2:3] * zc)
    y_c = cb_ref[...].astype(F32) * conv

    g_out = g_out_ref[...]
    for gi, y in enumerate((y_a, y_b, y_c)):
        lo, hi = gi * GROUP_WIDTH, (gi + 1) * GROUP_WIDTH
        y_ref[:, lo:hi] = _group_rms_norm(y, bd64, HEAD_DIM, g_out[:, lo:hi]).astype(y_ref.dtype)

    bd32 = bd32_ref[...]
    qd = (_group_rms_norm(dq_ref[...].astype(F32), bd32, DIFF_QK_DIM, gq_diff_ref[...])
          * (DIFF_QK_DIM ** -0.5 * LOG2_E))
    kd = _group_rms_norm(dk_ref[...].astype(F32), bd32, DIFF_QK_DIM, gk_diff_ref[...])
    qdn_ref[...] = qd.astype(qdn_ref.dtype)
    kdn_ref[...] = kd.astype(kdn_ref.dtype)


def _local_mix(z, batch, seq, sinks, wpool_bd, bd64, bd32, gq_swa, gk_swa, conv_w, g_out, gq_diff, gk_diff):
    t = batch * seq
    nb = seq // ROW_BLOCK
    halo_per_block = ROW_BLOCK // HALO

    def cur(col):
        return pl.BlockSpec((ROW_BLOCK, GROUP_WIDTH), lambda b, n: (b * nb + n, col))

    def prev(col):
        return pl.BlockSpec((ROW_BLOCK, GROUP_WIDTH), lambda b, n: (b * nb + jnp.maximum(n - 1, 0), col))

    def halo(col):
        return pl.BlockSpec(
            (HALO, GROUP_WIDTH),
            lambda b, n: (jnp.maximum((b * nb + n) * halo_per_block - 1, 0), col))

    def whole(arr):
        return pl.BlockSpec(arr.shape, lambda b, n: (0,) * arr.ndim)

    params = (wpool_bd, bd64, bd32, gq_swa, gk_swa, conv_w, g_out, gq_diff, gk_diff)
    in_specs = ([pl.BlockSpec(memory_space=pltpu.SMEM),
                 cur(0), halo(0), cur(1), cur(2), prev(2), cur(3), halo(3), cur(4), cur(5), halo(5),
                 cur(6), cur(7)] + [whole(a) for a in params])
    out_specs = [pl.BlockSpec((ROW_BLOCK, 3 * GROUP_WIDTH), lambda b, n: (b * nb + n, 0)),
                 pl.BlockSpec((ROW_BLOCK, GROUP_WIDTH), lambda b, n: (b * nb + n, 0)),
                 pl.BlockSpec((ROW_BLOCK, GROUP_WIDTH), lambda b, n: (b * nb + n, 0))]
    out_shape = [jax.ShapeDtypeStruct((t, 3 * GROUP_WIDTH), BF16),
                 jax.ShapeDtypeStruct((t, GROUP_WIDTH), BF16),
                 jax.ShapeDtypeStruct((t, GROUP_WIDTH), BF16)]
    return pl.pallas_call(
        _local_mix_kernel,
        grid=(batch, nb),
        in_specs=in_specs,
        out_specs=out_specs,
        out_shape=out_shape,
        compiler_params=pltpu.CompilerParams(dimension_semantics=("parallel", "parallel"),
                                             vmem_limit_bytes=VMEM_LIMIT),
        name="local_mix",
    )(sinks, *([z] * 12), *params)


def _diff_attn_kernel(q_ref, k_ref, v_ref, lamv_ref, bd64_ref, g_ref, y_ref,
                      qs_ref, vt_ref, m_ref, den_ref, acc_ref, *, one_minus_lam_init, lam_init):
    qi = pl.program_id(1)
    qb = DIFF_Q_BLOCK
    lamv = lamv_ref[...]
    lam = (jnp.exp(jnp.sum(lamv[0:1] * lamv[1:2], axis=-1, keepdims=True))
           - jnp.exp(jnp.sum(lamv[2:3] * lamv[3:4], axis=-1, keepdims=True)) + lam_init)
    n_maps = 2 * DIFF_HEADS

    @pl.when(qi == 0)
    def _():
        for c in range(vt_ref.shape[0]):
            vt_ref[c] = v_ref[c * qb:(c + 1) * qb, :].astype(F32).T.astype(BF16)

    q = q_ref[...]
    lane_group = lax.broadcasted_iota(jnp.int32, (qb, GROUP_WIDTH), 1) // DIFF_QK_DIM
    for g in range(n_maps):
        qs_ref[g * qb:(g + 1) * qb, :] = jnp.where(lane_group == g, q, jnp.zeros_like(q))
    m_ref[...] = jnp.full(m_ref.shape, NEG, F32)
    den_ref[...] = jnp.zeros(den_ref.shape, F32)
    acc_ref[...] = jnp.zeros(acc_ref.shape, F32)
    rel = (lax.broadcasted_iota(jnp.int32, (qb, qb), 1) - lax.broadcasted_iota(jnp.int32, (qb, qb), 0))
    relf = rel.astype(F32)
    slopes = [ALIBI_SLOPES[SWA_Q_HEADS + h] * LOG2_E for h in range(DIFF_HEADS)]
    bias = [slopes[h] * relf for h in range(DIFF_HEADS)]

    def kv_step(c, diagonal):
        start = pl.multiple_of(c * qb, qb)
        sc_all = lax.dot_general(k_ref[pl.ds(start, qb), :], qs_ref[...], NT_DIMS,
                                 preferred_element_type=F32)
        off = ((qi - c) * qb).astype(F32)
        for g in range(n_maps):
            cols = slice(g * qb, (g + 1) * qb)
            sc = sc_all[:, cols] - bias[g // 2]
            if diagonal:
                sc = jnp.where(rel >= 0, sc, NEG)
            shift = slopes[g // 2] * off
            m_old = m_ref[:, cols]
            m_new = jnp.maximum(m_old, jnp.max(sc, axis=0, keepdims=True) - shift)
            alpha = jnp.exp2(m_old - m_new)
            e = jnp.exp2(sc - (m_new + shift))
            den_ref[:, cols] = alpha * den_ref[:, cols] + jnp.sum(e, axis=0, keepdims=True)
            m_ref[:, cols] = m_new
            rows = slice((g // 2) * HEAD_DIM, (g // 2 + 1) * HEAD_DIM)
            pv = jnp.dot(vt_ref[c, rows, :], e.astype(BF16), preferred_element_type=F32)
            acc_ref[:, cols] = alpha * acc_ref[:, cols] + pv

    def off_diagonal(c, carry):
        kv_step(c, False)
        return carry

    lax.fori_loop(0, qi, off_diagonal, 0)
    kv_step(qi, True)

    heads = []
    for h in range(DIFF_HEADS):
        c1 = slice(2 * h * qb, (2 * h + 1) * qb)
        c2 = slice((2 * h + 1) * qb, (2 * h + 2) * qb)
        heads.append(acc_ref[:, c1] / den_ref[:, c1] - lam * (acc_ref[:, c2] / den_ref[:, c2]))
    o = jnp.concatenate(heads, axis=0).T
    y = _group_rms_norm(o, bd64_ref[...], HEAD_DIM, g_ref[...]) * one_minus_lam_init
    y_ref[...] = y.astype(y_ref.dtype)


def _diff_attn(qdn, kdn, z, batch, seq, lamv, bd64, g_d, lam_init):
    t = batch * seq
    nq = seq // DIFF_Q_BLOCK
    v_col = 8
    stacked = 2 * DIFF_HEADS * DIFF_Q_BLOCK
    kernel = functools.partial(_diff_attn_kernel, one_minus_lam_init=1.0 - lam_init, lam_init=lam_init)
    return pl.pallas_call(
        kernel,
        grid=(batch, nq),
        in_specs=[pl.BlockSpec((DIFF_Q_BLOCK, GROUP_WIDTH), lambda b, i: (b * nq + i, 0)),
                  pl.BlockSpec((seq, GROUP_WIDTH), lambda b, i: (b, 0)),
                  pl.BlockSpec((seq, GROUP_WIDTH), lambda b, i: (b, v_col)),
                  pl.BlockSpec(lamv.shape, lambda b, i: (0, 0)),
                  pl.BlockSpec(bd64.shape, lambda b, i: (0, 0)),
                  pl.BlockSpec(g_d.shape, lambda b, i: (0, 0))],
        out_specs=pl.BlockSpec((DIFF_Q_BLOCK, GROUP_WIDTH), lambda b, i: (b * nq + i, 0)),
        out_shape=jax.ShapeDtypeStruct((t, GROUP_WIDTH), BF16),
        scratch_shapes=[pltpu.VMEM((stacked, GROUP_WIDTH), BF16),
                        pltpu.VMEM((seq // DIFF_Q_BLOCK, GROUP_WIDTH, DIFF_Q_BLOCK), BF16),
                        pltpu.VMEM((1, stacked), F32),
                        pltpu.VMEM((1, stacked), F32),
                        pltpu.VMEM((HEAD_DIM, stacked), F32)],
        compiler_params=pltpu.CompilerParams(dimension_semantics=("parallel", "arbitrary"),
                                             vmem_limit_bytes=VMEM_LIMIT),
        name="diff_attn",
    )(qdn, kdn, z, lamv, bd64, g_d)


def _odd_even_merge_sort_pairs(n):
    pairs = []

    def merge(lo, length, r):
        step = 2 * r
        if step < length:
            merge(lo, length, step)
            merge(lo + r, length, step)
            pairs.extend((i, i + r) for i in range(lo + r, lo + length - r, step))
        else:
            pairs.append((lo, lo + r))

    def sort(lo, length):
        if length > 1:
            sort(lo, length // 2)
            sort(lo + length // 2, length // 2)
            merge(lo, length, 1)

    sort(0, n)
    return pairs


_SORT16 = _odd_even_merge_sort_pairs(PEER_TOPK)


def _top16_sorted(slabs):
    n = PEER_TOPK
    a = list(slabs) + [jnp.full(slabs[0].shape, NEG, F32)] * (n - len(slabs))
    for i, j in _SORT16:
        a[i], a[j] = jnp.maximum(a[i], a[j]), jnp.minimum(a[i], a[j])
    for shift in (4, 2, 1):
        b = [pltpu.roll(x, shift, axis=0) for x in a]
        a = [jnp.maximum(a[j], b[n - 1 - j]) for j in range(n)]
        d = n // 2
        while d:
            for i in range(n):
                if not i & d:
                    a[i], a[i + d] = jnp.maximum(a[i], a[i + d]), jnp.minimum(a[i], a[i + d])
            d //= 2
    return a


def _sublane_pick(arrays):
    row = lax.broadcasted_iota(jnp.int32, arrays[0].shape, 0)
    out = arrays[0]
    for s in range(1, 8):
        out = jnp.where(row == s, arrays[s], out)
    return out


def _dup_bf16_words(x):
    bits = pltpu.bitcast(x.astype(BF16).astype(F32), jnp.uint32)
    return bits | (bits >> 16)


def _out_router_kernel(yabc_ref, yd_ref, x_ref, wo1_ref, wo2_ref, g2_ref, wq_ref, keys_ref,
                       x1_ref, xn2_ref, n1_ref, w1_ref, r2_ref, w2_ref):
    acc = (jnp.dot(yabc_ref[...], wo1_ref[...], preferred_element_type=F32)
           + jnp.dot(yd_ref[...], wo2_ref[...], preferred_element_type=F32))
    x1 = x_ref[...] + acc
    x1_ref[...] = x1
    xn2 = (x1 * lax.rsqrt(jnp.mean(x1 * x1, axis=-1, keepdims=True) + EPS) * g2_ref[...]).astype(BF16)
    xn2_ref[...] = pltpu.bitcast(xn2, jnp.uint32)
    q = jnp.dot(xn2, wq_ref[...], preferred_element_type=F32)
    tm = q.shape[0]
    for h in range(PEER_HEADS):
        scores = []
        for c in range(2):
            lo = (2 * h + c) * PEER_HALF
            qhc = q[:, lo:lo + PEER_HALF].astype(BF16)
            scores.append(lax.dot_general(keys_ref[c], qhc, NT_DIMS, preferred_element_type=F32))
        for p in range(tm // ROUTER_LANES):
            ls = slice(p * ROUTER_LANES, (p + 1) * ROUTER_LANES)
            _route_head(scores[0][:, ls], scores[1][:, ls],
                        n1_ref.at[h, :, ls], w1_ref.at[h, :, ls], r2_ref.at[h, :, ls], w2_ref.at[h, :, ls])


def _route_head(s1, s2, n1_ref, w1_ref, r2_ref, w2_ref):
    lanes = s1.shape[1]
    row8 = lax.broadcasted_iota(jnp.int32, (8, lanes), 0)
    slabs = PEER_KEYS // 8
    t1 = _top16_sorted([s1[8 * j:8 * j + 8] for j in range(slabs)])
    t2 = _top16_sorted([s2[8 * j:8 * j + 8] for j in range(slabs)])
    v1_lo, v1_hi, v2_hi = _sublane_pick(t1[:8]), _sublane_pick(t1[8:]), _sublane_pick(t2[8:])
    groups = [v1_lo + t2[0], v1_hi + t2[0]]
    for b in range(1, 8):
        groups.append(jnp.where(row8 < PEER_TOPK // (b + 1), v1_lo + t2[b], NEG))
    groups.append(t1[0] + v2_hi)
    best = _top16_sorted(groups)
    tau = best[PEER_TOPK - 1]
    zsum = jnp.ones_like(tau)
    for r in range(1, PEER_TOPK):
        zsum = zsum + jnp.exp(best[r] - best[0])
    inv_z = 1.0 / zsum
    for j in range(PEER_KEYS // 16):
        words = slice(8 * j, 8 * j + 8)
        rank2, w2 = [], []
        for half in range(2):
            ks = slice(16 * j + 8 * half, 16 * j + 8 * half + 8)
            s1k, s2k = s1[ks], s2[ks]
            n1 = jnp.zeros((8, lanes), F32)
            rk = jnp.full((8, lanes), float(PEER_TOPK), F32)
            for b in reversed(range(PEER_TOPK)):
                n1 = n1 + jnp.where(s1k + t2[b] >= tau, 1.0, 0.0)
                rk = jnp.where(s2k >= t2[b], float(b), rk)
            n1_ref[ks, :] = _dup_bf16_words(n1)
            w1_ref[ks, :] = _dup_bf16_words(jnp.exp(s1k - t1[0]) * inv_z)
            rank2.append(rk)
            w2.append(jnp.exp(s2k - t2[0]))
        r2_ref[words, :] = pltpu.bitcast(jnp.concatenate(rank2, axis=0).astype(BF16), jnp.uint32)
        w2_ref[words, :] = pltpu.bitcast(jnp.concatenate(w2, axis=0).astype(BF16), jnp.uint32)


def _out_router(yabc, yd, x2d, wo1, wo2, g2, wq, keys):
    t = x2d.shape[0]
    tm = min(ROUTER_ROWS, t)

    def whole(arr):
        return pl.BlockSpec(arr.shape, lambda i: (0,) * arr.ndim)

    row_spec = pl.BlockSpec((PEER_HEADS, PEER_KEYS, tm), lambda i: (0, 0, i))
    row_shape = jax.ShapeDtypeStruct((PEER_HEADS, PEER_KEYS, t), jnp.uint32)
    tile_spec = pl.BlockSpec((PEER_HEADS, PEER_KEYS // 2, tm), lambda i: (0, 0, i))
    tile_shape = jax.ShapeDtypeStruct((PEER_HEADS, PEER_KEYS // 2, t), jnp.uint32)
    return pl.pallas_call(
        _out_router_kernel,
        grid=(t // tm,),
        in_specs=[pl.BlockSpec((tm, 3 * GROUP_WIDTH), lambda i: (i, 0)),
                  pl.BlockSpec((tm, GROUP_WIDTH), lambda i: (i, 0)),
                  pl.BlockSpec((tm, D_MODEL), lambda i: (i, 0)),
                  whole(wo1), whole(wo2), whole(g2), whole(wq), whole(keys)],
        out_specs=[pl.BlockSpec((tm, D_MODEL), lambda i: (i, 0)),
                   pl.BlockSpec((tm // 2, D_MODEL), lambda i: (i, 0)),
                   row_spec, row_spec, tile_spec, tile_spec],
        out_shape=[jax.ShapeDtypeStruct((t, D_MODEL), F32),
                   jax.ShapeDtypeStruct((t // 2, D_MODEL), jnp.uint32),
                   row_shape, row_shape, tile_shape, tile_shape],
        compiler_params=pltpu.CompilerParams(dimension_semantics=("parallel",),
                                             vmem_limit_bytes=VMEM_LIMIT),
        name="out_router",
    )(yabc, yd, x2d, wo1, wo2, g2, wq, keys)


def _gate_activations(h_ref, a_ref, n1_ref, w1_ref, r2_ref, w2_ref, first_row, rows):
    tb = h_ref.shape[1]
    zero = jnp.zeros((), BF16)
    half = PEER_KEYS // 2
    for lc in range(tb // 128):
        ls = slice(lc * 128, (lc + 1) * 128)
        n1_rows = [n1_ref[hd, pl.ds(first_row, rows), ls] for hd in range(PEER_HEADS)]
        w1_rows = [w1_ref[hd, pl.ds(first_row, rows), ls] for hd in range(PEER_HEADS)]
        for r in range(rows):
            w = None
            for hd in range(PEER_HEADS):
                n1b = pltpu.bitcast(jnp.broadcast_to(n1_rows[hd][r:r + 1], (half, 128)), BF16)
                w1b = pltpu.bitcast(jnp.broadcast_to(w1_rows[hd][r:r + 1], (half, 128)), BF16)
                rank2 = pltpu.bitcast(r2_ref[hd, :, ls], BF16)
                w2 = pltpu.bitcast(w2_ref[hd, :, ls], BF16)
                term = jnp.where(rank2 < n1b, w2 * w1b, zero)
                w = term if w is None else w + term
            g = _gelu(h_ref[r * PEER_KEYS:(r + 1) * PEER_KEYS, ls]).astype(BF16)
            a_ref[r * half:(r + 1) * half, ls] = pltpu.bitcast(w * g, jnp.uint32)


def _peer_kernel(x_ref, xn_ref, u_ref, vt_ref, n1_ref, w1_ref, r2_ref, w2_ref, o_ref,
                 acc_ref, h_ref, a_ref, *, rows):
    j = pl.program_id(1)

    @pl.when(j == 0)
    def _():
        acc_ref[...] = jnp.zeros_like(acc_ref)

    xn = pltpu.bitcast(xn_ref[...], BF16)
    for m in range(0, u_ref.shape[0], PEER_H_ROWS):
        h_ref[m:m + PEER_H_ROWS, :] = lax.dot_general(u_ref[m:m + PEER_H_ROWS, :], xn, NT_DIMS,
                                                      preferred_element_type=F32)
    first_row = pl.multiple_of(j * rows, rows)
    _gate_activations(h_ref, a_ref, n1_ref, w1_ref, r2_ref, w2_ref, first_row, rows)
    acc_ref[...] += jnp.dot(vt_ref[...], pltpu.bitcast(a_ref[...], BF16), preferred_element_type=F32)

    @pl.when(j == pl.num_programs(1) - 1)
    def _():
        o_ref[...] = x_ref[...] + acc_ref[...].T


def _peer(x1, xn2, u_bf16, vt_bf16, n1, w1, r2, w2):
    t = x1.shape[0]
    tb = min(PEER_TOKENS, t)
    ec = PEER_EXPERT_CHUNK
    rows = ec // PEER_KEYS
    assert rows % 8 == 0 and tb % 128 == 0 and t % tb == 0 and PEER_EXPERTS % ec == 0
    row_spec = pl.BlockSpec((PEER_HEADS, PEER_KEYS, tb), lambda i, j: (0, 0, i))
    tile_spec = pl.BlockSpec((PEER_HEADS, PEER_KEYS // 2, tb), lambda i, j: (0, 0, i))
    kernel = functools.partial(_peer_kernel, rows=rows)
    return pl.pallas_call(
        kernel,
        grid=(t // tb, PEER_EXPERTS // ec),
        in_specs=[pl.BlockSpec((tb, D_MODEL), lambda i, j: (i, 0)),
                  pl.BlockSpec((tb // 2, D_MODEL), lambda i, j: (i, 0)),
                  pl.BlockSpec((ec, D_MODEL), lambda i, j: (j, 0)),
                  pl.BlockSpec((D_MODEL, ec), lambda i, j: (0, j)),
                  row_spec, row_spec, tile_spec, tile_spec],
        out_specs=pl.BlockSpec((tb, D_MODEL), lambda i, j: (i, 0)),
        out_shape=jax.ShapeDtypeStruct((t, D_MODEL), F32),
        scratch_shapes=[pltpu.VMEM((D_MODEL, tb), F32),
                        pltpu.VMEM((ec, tb), F32),
                        pltpu.VMEM((ec // 2, tb), jnp.uint32)],
        compiler_params=pltpu.CompilerParams(dimension_semantics=("parallel", "arbitrary"),
                                             vmem_limit_bytes=VMEM_LIMIT),
        name="peer",
    )(x1, xn2, u_bf16, vt_bf16, n1, w1, r2, w2)


def _block_diag_ones(width, group):
    idx = np.arange(width) // group
    return jnp.asarray((idx[:, None] == idx[None, :]).astype(np.float32), BF16)


def _layer(x2d, batch, seq, layer_idx, norm1_g, w_in, w_pool, swa_q_norm, swa_k_norm, swa_sinks, conv_w,
           diff_q_norm, diff_k_norm, lam_q1, lam_k1, lam_q2, lam_k2, out_norm_g, w_out, norm2_g, w_query,
           sub_keys, peer_u, peer_v):
    lam_init = 0.8 - 0.6 * float(np.exp(-0.3 * layer_idx))
    bd64 = _block_diag_ones(GROUP_WIDTH, HEAD_DIM)
    bd32 = _block_diag_ones(GROUP_WIDTH, DIFF_QK_DIM)
    wpool_bd = jax.scipy.linalg.block_diag(*[w_pool[g] for g in range(len(POOL_WINDOWS))]).astype(BF16)

    z = _in_proj(x2d, norm1_g.reshape(1, D_MODEL), w_in.astype(BF16))
    yabc, qdn, kdn = _local_mix(
        z, batch, seq, swa_sinks.astype(F32), wpool_bd, bd64, bd32,
        jnp.tile(swa_q_norm, SWA_Q_HEADS).reshape(1, GROUP_WIDTH),
        jnp.tile(swa_k_norm, 2).reshape(1, 128),
        conv_w, out_norm_g[:3 * GROUP_WIDTH].reshape(1, 3 * GROUP_WIDTH),
        jnp.tile(diff_q_norm, 2 * DIFF_HEADS).reshape(1, GROUP_WIDTH),
        jnp.tile(diff_k_norm, 2 * DIFF_HEADS).reshape(1, GROUP_WIDTH))
    lamv = jnp.stack([lam_q1, lam_k1, lam_q2, lam_k2]).astype(F32)
    yd = _diff_attn(qdn, kdn, z, batch, seq, lamv, bd64,
                    out_norm_g[3 * GROUP_WIDTH:].reshape(1, GROUP_WIDTH), lam_init)
    w_out_bf16 = w_out.astype(BF16)
    x1, xn2, n1, w1, r2, w2 = _out_router(
        yabc, yd, x2d, w_out_bf16[:3 * GROUP_WIDTH], w_out_bf16[3 * GROUP_WIDTH:],
        norm2_g.reshape(1, D_MODEL), w_query.astype(BF16), sub_keys.astype(BF16))
    return _peer(x1, xn2, peer_u.astype(BF16), peer_v.T.astype(BF16), n1, w1, r2, w2)


def kernel(x, norm1_g, w_in, w_pool, swa_q_norm, swa_k_norm, swa_sinks, conv_w, diff_q_norm, diff_k_norm,
           lam_q1, lam_k1, lam_q2, lam_k2, out_norm_g, w_out, norm2_g, w_query, sub_keys, peer_u, peer_v):
    batch, seq, d = x.shape
    x2d = x.reshape(batch * seq, d)
    depth = norm1_g.shape[0]
    for l in range(depth):
        x2d = _layer(x2d, batch, seq, l, norm1_g[l], w_in[l], w_pool[l], swa_q_norm[l], swa_k_norm[l],
                     swa_sinks[l], conv_w[l], diff_q_norm[l], diff_k_norm[l], lam_q1[l], lam_k1[l],
                     lam_q2[l], lam_k2[l], out_norm_g[l], w_out[l], norm2_g[l], w_query[l], sub_keys[l],
                     peer_u[l], peer_v[l])
    return x2d.reshape(batch, seq, d)
```

```python
import functools

import jax
import jax.numpy as jnp
import numpy as np
from jax import lax
from jax.experimental import pallas as pl
from jax.experimental.pallas import tpu as pltpu

F32 = jnp.float32
BF16 = jnp.bfloat16

D_MODEL = 1024
EPS = 1e-6
HEAD_DIM = 64
GROUP_WIDTH = 256
POOL_WINDOWS = (2, 4, 8, 16)
SWA_Q_HEADS = 4
SWA_GQA = 2
SWA_WINDOW = 128
DIFF_HEADS = 4
DIFF_QK_DIM = 32
IN_PROJ_WIDTH = 2304
PEER_HEADS = 8
PEER_KEYS = 128
PEER_EXPERTS = PEER_KEYS * PEER_KEYS
PEER_HALF = 128
PEER_TOPK = 16
ALIBI_SLOPES = tuple(2.0 ** (-(i + 1)) for i in range(SWA_Q_HEADS + DIFF_HEADS))

NEG = -1e30
LOG2_E = 1.4426950408889634
HALO = 16
ROW_BLOCK = 128
DIFF_Q_BLOCK = 256
IN_PROJ_ROWS = 1024
ROUTER_ROWS = 512
ROUTER_LANES = 256
PEER_TOKENS = 512
PEER_EXPERT_CHUNK = 2048
PEER_H_ROWS = 1024
VMEM_LIMIT = 56 * 1024 * 1024

NT_DIMS = (((1,), (1,)), ((), ()))


def _group_sumsq(x, bd):
    x2 = x * x
    hi = x2.astype(BF16)
    lo = (x2 - hi.astype(F32)).astype(BF16)
    return (jnp.dot(hi, bd, preferred_element_type=F32)
            + jnp.dot(lo, bd, preferred_element_type=F32))


def _group_rms_norm(x, bd, group, gain):
    return x * lax.rsqrt(_group_sumsq(x, bd) * (1.0 / group) + EPS) * gain


def _gelu(h):
    return 0.5 * h * (1.0 + lax.erf(h * np.float32(np.sqrt(0.5))))


def _in_proj_kernel(x_ref, g_ref, w_ref, z_ref):
    x = x_ref[...]
    xn = x * lax.rsqrt(jnp.mean(x * x, axis=-1, keepdims=True) + EPS) * g_ref[...]
    z = jnp.dot(xn.astype(BF16), w_ref[...], preferred_element_type=F32)
    z_ref[...] = z.astype(z_ref.dtype)


def _in_proj(x2d, g, w_bf16):
    t = x2d.shape[0]
    tm = min(IN_PROJ_ROWS, t)
    return pl.pallas_call(
        _in_proj_kernel,
        grid=(t // tm,),
        in_specs=[pl.BlockSpec((tm, D_MODEL), lambda i: (i, 0)),
                  pl.BlockSpec((1, D_MODEL), lambda i: (0, 0)),
                  pl.BlockSpec((D_MODEL, IN_PROJ_WIDTH), lambda i: (0, 0))],
        out_specs=pl.BlockSpec((tm, IN_PROJ_WIDTH), lambda i: (i, 0)),
        out_shape=jax.ShapeDtypeStruct((t, IN_PROJ_WIDTH), BF16),
        compiler_params=pltpu.CompilerParams(dimension_semantics=("parallel",),
                                             vmem_limit_bytes=VMEM_LIMIT),
        name="in_proj",
    )(x2d, g, w_bf16)


def _local_mix_kernel(sinks_ref, a_ref, a_halo_ref, bq_ref, bkv_ref, bkv_prev_ref, ch_ref, ch_halo_ref,
                      cb_ref, cc_ref, cc_halo_ref, dq_ref, dk_ref,
                      wpool_ref, bd64_ref, bd32_ref, gq_swa_ref, gk_swa_ref, conv_w_ref, g_out_ref,
                      gq_diff_ref, gk_diff_ref,
                      y_ref, qdn_ref, kdn_ref):
    n = pl.program_id(1)
    not_first = (n > 0).astype(F32)
    bd64 = bd64_ref[...]
    rows = lax.broadcasted_iota(jnp.int32, (ROW_BLOCK, GROUP_WIDTH), 0)
    cols = lax.broadcasted_iota(jnp.int32, (ROW_BLOCK, GROUP_WIDTH), 1)

    p = a_ref[...].astype(F32)
    p_ext = jnp.concatenate([a_halo_ref[...].astype(F32) * not_first, p], axis=0)
    s2 = p_ext + pltpu.roll(p_ext, 1, axis=0)
    s4 = s2 + pltpu.roll(s2, 2, axis=0)
    s8 = s4 + pltpu.roll(s4, 4, axis=0)
    s16 = s8 + pltpu.roll(s8, 8, axis=0)
    wsel = jnp.where(cols < 64, 2, jnp.where(cols < 128, 4, jnp.where(cols < 192, 8, 16)))
    ssel = jnp.where(cols < 64, s2[HALO:], jnp.where(cols < 128, s4[HALO:],
                                                      jnp.where(cols < 192, s8[HALO:], s16[HALO:])))
    cnt = jnp.minimum(n * ROW_BLOCK + rows + 1, wsel).astype(F32)
    d = ssel / cnt - p
    y_a = jnp.dot(d.astype(BF16), wpool_ref[...], preferred_element_type=F32)

    q = _group_rms_norm(bq_ref[...].astype(F32), bd64, HEAD_DIM, gq_swa_ref[...]) * (HEAD_DIM ** -0.5)
    qn = q.astype(BF16)
    bd64k = bd64[:128, :128]
    kv_cur = bkv_ref[...]
    kv_prev = bkv_prev_ref[...]
    k_cur = _group_rms_norm(kv_cur[:, :128].astype(F32), bd64k, HEAD_DIM, gk_swa_ref[...])
    k_prev = _group_rms_norm(kv_prev[:, :128].astype(F32), bd64k, HEAD_DIM, gk_swa_ref[...])
    kn = jnp.concatenate([k_prev, k_cur], axis=0).astype(BF16)
    v = jnp.concatenate([kv_prev[:, 128:], kv_cur[:, 128:]], axis=0)
    qi = lax.broadcasted_iota(jnp.int32, (ROW_BLOCK, 2 * ROW_BLOCK), 0)
    kj = lax.broadcasted_iota(jnp.int32, (ROW_BLOCK, 2 * ROW_BLOCK), 1)
    dist = qi + ROW_BLOCK - kj
    valid = (dist >= 0) & (dist < SWA_WINDOW) & ((n - 1) * ROW_BLOCK + kj >= 0)
    distf = dist.astype(F32)
    heads = []
    for h in range(SWA_Q_HEADS):
        hk = h // SWA_GQA
        sc = lax.dot_general(qn[:, h * 64:(h + 1) * 64], kn[:, hk * 64:(hk + 1) * 64], NT_DIMS,
                             preferred_element_type=F32)
        sc = jnp.where(valid, sc - ALIBI_SLOPES[h] * distf, NEG)
        sink = sinks_ref[h]
        mx = jnp.maximum(jnp.max(sc, axis=-1, keepdims=True), sink)
        e = jnp.exp(sc - mx)
        den = jnp.sum(e, axis=-1, keepdims=True) + jnp.exp(sink - mx)
        o = jnp.dot(e.astype(BF16), v[:, hk * 64:(hk + 1) * 64], preferred_element_type=F32)
        heads.append(o / den)
    y_b = jnp.concatenate(heads, axis=-1)

    zc = cc_ref[...].astype(F32) * ch_ref[...].astype(F32)
    zc_halo = cc_halo_ref[...].astype(F32) * ch_halo_ref[...].astype(F32) * not_first
    zc_ext = jnp.concatenate([zc_halo, zc], axis=0)
    cw = conv_w_ref[...]
    conv = (cw[0:1] * pltpu.roll(zc_ext, 2, axis=0)[HALO:] + cw[1:2] * pltpu.roll(zc_ext, 1, axis=0)[HALO:]
            + cw[2:3] * zc)
    y_c = cb_ref[...].astype(F32) * conv

    g_out = g_out_ref[...]
    for gi, y in enumerate((y_a, y_b, y_c)):
        lo, hi = gi * GROUP_WIDTH, (gi + 1) * GROUP_WIDTH
        y_ref[:, lo:hi] = _group_rms_norm(y, bd64, HEAD_DIM, g_out[:, lo:hi]).astype(y_ref.dtype)

    bd32 = bd32_ref[...]
    qd = (_group_rms_norm(dq_ref[...].astype(F32), bd32, DIFF_QK_DIM, gq_diff_ref[...])
          * (DIFF_QK_DIM ** -0.5 * LOG2_E))
    kd = _group_rms_norm(dk_ref[...].astype(F32), bd32, DIFF_QK_DIM, gk_diff_ref[...])
    qdn_ref[...] = qd.astype(qdn_ref.dtype)
    kdn_ref[...] = kd.astype(kdn_ref.dtype)


def _local_mix(z, batch, seq, sinks, wpool_bd, bd64, bd32, gq_swa, gk_swa, conv_w, g_out, gq_diff, gk_diff):
    t = batch * seq
    nb = seq // ROW_BLOCK
    halo_per_block = ROW_BLOCK // HALO

    def cur(col):
        return pl.BlockSpec((ROW_BLOCK, GROUP_WIDTH), lambda b, n: (b * nb + n, col))

    def prev(col):
        return pl.BlockSpec((ROW_BLOCK, GROUP_WIDTH), lambda b, n: (b * nb + jnp.maximum(n - 1, 0), col))

    def halo(col):
        return pl.BlockSpec(
            (HALO, GROUP_WIDTH),
            lambda b, n: (jnp.maximum((b * nb + n) * halo_per_block - 1, 0), col))

    def whole(arr):
        return pl.BlockSpec(arr.shape, lambda b, n: (0,) * arr.ndim)

    params = (wpool_bd, bd64, bd32, gq_swa, gk_swa, conv_w, g_out, gq_diff, gk_diff)
    in_specs = ([pl.BlockSpec(memory_space=pltpu.SMEM),
                 cur(0), halo(0), cur(1), cur(2), prev(2), cur(3), halo(3), cur(4), cur(5), halo(5),
                 cur(6), cur(7)] + [whole(a) for a in params])
    out_specs = [pl.BlockSpec((ROW_BLOCK, 3 * GROUP_WIDTH), lambda b, n: (b * nb + n, 0)),
                 pl.BlockSpec((ROW_BLOCK, GROUP_WIDTH), lambda b, n: (b * nb + n, 0)),
                 pl.BlockSpec((ROW_BLOCK, GROUP_WIDTH), lambda b, n: (b * nb + n, 0))]
    out_shape = [jax.ShapeDtypeStruct((t, 3 * GROUP_WIDTH), BF16),
                 jax.ShapeDtypeStruct((t, GROUP_WIDTH), BF16),
                 jax.ShapeDtypeStruct((t, GROUP_WIDTH), BF16)]
    return pl.pallas_call(
        _local_mix_kernel,
        grid=(batch, nb),
        in_specs=in_specs,
        out_specs=out_specs,
        out_shape=out_shape,
        compiler_params=pltpu.CompilerParams(dimension_semantics=("parallel", "parallel"),
                                             vmem_limit_bytes=VMEM_LIMIT),
        name="local_mix",
    )(sinks, *([z] * 12), *params)


def _diff_attn_kernel(q_ref, k_ref, v_ref, lamv_ref, bd64_ref, g_ref, y_ref,
                      qs_ref, vt_ref, m_ref, den_ref, acc_ref, *, one_minus_lam_init, lam_init):
    qi = pl.program_id(1)
    qb = DIFF_Q_BLOCK
    lamv = lamv_ref[...]
    lam = (jnp.exp(jnp.sum(lamv[0:1] * lamv[1:2], axis=-1, keepdims=True))
           - jnp.exp(jnp.sum(lamv[2:3] * lamv[3:4], axis=-1, keepdims=True)) + lam_init)
    n_maps = 2 * DIFF_HEADS

    @pl.when(qi == 0)
    def _():
        for c in range(vt_ref.shape[0]):
            vt_ref[c] = v_ref[c * qb:(c + 1) * qb, :].astype(F32).T.astype(BF16)

    q = q_ref[...]
    lane_group = lax.broadcasted_iota(jnp.int32, (qb, GROUP_WIDTH), 1) // DIFF_QK_DIM
    for g in range(n_maps):
        qs_ref[g * qb:(g + 1) * qb, :] = jnp.where(lane_group == g, q, jnp.zeros_like(q))
    m_ref[...] = jnp.full(m_ref.shape, NEG, F32)
    den_ref[...] = jnp.zeros(den_ref.shape, F32)
    acc_ref[...] = jnp.zeros(acc_ref.shape, F32)
    rel = (lax.broadcasted_iota(jnp.int32, (qb, qb), 1) - lax.broadcasted_iota(jnp.int32, (qb, qb), 0))
    relf = rel.astype(F32)
    slopes = [ALIBI_SLOPES[SWA_Q_HEADS + h] * LOG2_E for h in range(DIFF_HEADS)]
    bias = [slopes[h] * relf for h in range(DIFF_HEADS)]

    def kv_step(c, diagonal):
        start = pl.multiple_of(c * qb, qb)
        sc_all = lax.dot_general(k_ref[pl.ds(start, qb), :], qs_ref[...], NT_DIMS,
                                 preferred_element_type=F32)
        off = ((qi - c) * qb).astype(F32)
        for g in range(n_maps):
            cols = slice(g * qb, (g + 1) * qb)
            sc = sc_all[:, cols] - bias[g // 2]
            if diagonal:
                sc = jnp.where(rel >= 0, sc, NEG)
            shift = slopes[g // 2] * off
            m_old = m_ref[:, cols]
            m_new = jnp.maximum(m_old, jnp.max(sc, axis=0, keepdims=True) - shift)
            alpha = jnp.exp2(m_old - m_new)
            e = jnp.exp2(sc - (m_new + shift))
            den_ref[:, cols] = alpha * den_ref[:, cols] + jnp.sum(e, axis=0, keepdims=True)
            m_ref[:, cols] = m_new
            rows = slice((g // 2) * HEAD_DIM, (g // 2 + 1) * HEAD_DIM)
            pv = jnp.dot(vt_ref[c, rows, :], e.astype(BF16), preferred_element_type=F32)
            acc_ref[:, cols] = alpha * acc_ref[:, cols] + pv

    def off_diagonal(c, carry):
        kv_step(c, False)
        return carry

    lax.fori_loop(0, qi, off_diagonal, 0)
    kv_step(qi, True)

    heads = []
    for h in range(DIFF_HEADS):
        c1 = slice(2 * h * qb, (2 * h + 1) * qb)
        c2 = slice((2 * h + 1) * qb, (2 * h + 2) * qb)
        heads.append(acc_ref[:, c1] / den_ref[:, c1] - lam * (acc_ref[:, c2] / den_ref[:, c2]))
    o = jnp.concatenate(heads, axis=0).T
    y = _group_rms_norm(o, bd64_ref[...], HEAD_DIM, g_ref[...]) * one_minus_lam_init
    y_ref[...] = y.astype(y_ref.dtype)


def _diff_attn(qdn, kdn, z, batch, seq, lamv, bd64, g_d, lam_init):
    t = batch * seq
    nq = seq // DIFF_Q_BLOCK
    v_col = 8
    stacked = 2 * DIFF_HEADS * DIFF_Q_BLOCK
    kernel = functools.partial(_diff_attn_kernel, one_minus_lam_init=1.0 - lam_init, lam_init=lam_init)
    return pl.pallas_call(
        kernel,
        grid=(batch, nq),
        in_specs=[pl.BlockSpec((DIFF_Q_BLOCK, GROUP_WIDTH), lambda b, i: (b * nq + i, 0)),
                  pl.BlockSpec((seq, GROUP_WIDTH), lambda b, i: (b, 0)),
                  pl.BlockSpec((seq, GROUP_WIDTH), lambda b, i: (b, v_col)),
                  pl.BlockSpec(lamv.shape, lambda b, i: (0, 0)),
                  pl.BlockSpec(bd64.shape, lambda b, i: (0, 0)),
                  pl.BlockSpec(g_d.shape, lambda b, i: (0, 0))],
        out_specs=pl.BlockSpec((DIFF_Q_BLOCK, GROUP_WIDTH), lambda b, i: (b * nq + i, 0)),
        out_shape=jax.ShapeDtypeStruct((t, GROUP_WIDTH), BF16),
        scratch_shapes=[pltpu.VMEM((stacked, GROUP_WIDTH), BF16),
                        pltpu.VMEM((seq // DIFF_Q_BLOCK, GROUP_WIDTH, DIFF_Q_BLOCK), BF16),
                        pltpu.VMEM((1, stacked), F32),
                        pltpu.VMEM((1, stacked), F32),
                        pltpu.VMEM((HEAD_DIM, stacked), F32)],
        compiler_params=pltpu.CompilerParams(dimension_semantics=("parallel", "arbitrary"),
                                             vmem_limit_bytes=VMEM_LIMIT),
        name="diff_attn",
    )(qdn, kdn, z, lamv, bd64, g_d)


def _odd_even_merge_sort_pairs(n):
    pairs = []

    def merge(lo, length, r):
        step = 2 * r
        if step < length:
            merge(lo, length, step)
            merge(lo + r, length, step)
            pairs.extend((i, i + r) for i in range(lo + r, lo + length - r, step))
        else:
            pairs.append((lo, lo + r))

    def sort(lo, length):
        if length > 1:
            sort(lo, length // 2)
            sort(lo + length // 2, length // 2)
            merge(lo, length, 1)

    sort(0, n)
    return pairs


_SORT16 = _odd_even_merge_sort_pairs(PEER_TOPK)


def _top16_sorted(slabs):
    n = PEER_TOPK
    a = list(slabs) + [jnp.full(slabs[0].shape, NEG, F32)] * (n - len(slabs))
    for i, j in _SORT16:
        a[i], a[j] = jnp.maximum(a[i], a[j]), jnp.minimum(a[i], a[j])
    for shift in (4, 2, 1):
        b = [pltpu.roll(x, shift, axis=0) for x in a]
        a = [jnp.maximum(a[j], b[n - 1 - j]) for j in range(n)]
        d = n // 2
        while d:
            for i in range(n):
                if not i & d:
                    a[i], a[i + d] = jnp.maximum(a[i], a[i + d]), jnp.minimum(a[i], a[i + d])
            d //= 2
    return a


def _sublane_pick(arrays):
    row = lax.broadcasted_iota(jnp.int32, arrays[0].shape, 0)
    out = arrays[0]
    for s in range(1, 8):
        out = jnp.where(row == s, arrays[s], out)
    return out


def _dup_bf16_words(x):
    bits = pltpu.bitcast(x.astype(BF16).astype(F32), jnp.uint32)
    return bits | (bits >> 16)


def _out_router_kernel(yabc_ref, yd_ref, x_ref, wo1_ref, wo2_ref, g2_ref, wq_ref, keys_ref,
                       x1_ref, xn2_ref, n1_ref, w1_ref, r2_ref, w2_ref):
    acc = (jnp.dot(yabc_ref[...], wo1_ref[...], preferred_element_type=F32)
           + jnp.dot(yd_ref[...], wo2_ref[...], preferred_element_type=F32))
    x1 = x_ref[...] + acc
    x1_ref[...] = x1
    xn2 = (x1 * lax.rsqrt(jnp.mean(x1 * x1, axis=-1, keepdims=True) + EPS) * g2_ref[...]).astype(BF16)
    xn2_ref[...] = pltpu.bitcast(xn2, jnp.uint32)
    q = jnp.dot(xn2, wq_ref[...], preferred_element_type=F32)
    tm = q.shape[0]
    for h in range(PEER_HEADS):
        scores = []
        for c in range(2):
            lo = (2 * h + c) * PEER_HALF
            qhc = q[:, lo:lo + PEER_HALF].astype(BF16)
            scores.append(lax.dot_general(keys_ref[c], qhc, NT_DIMS, preferred_element_type=F32))
        for p in range(tm // ROUTER_LANES):
            ls = slice(p * ROUTER_LANES, (p + 1) * ROUTER_LANES)
            _route_head(scores[0][:, ls], scores[1][:, ls],
                        n1_ref.at[h, :, ls], w1_ref.at[h, :, ls], r2_ref.at[h, :, ls], w2_ref.at[h, :, ls])


def _route_head(s1, s2, n1_ref, w1_ref, r2_ref, w2_ref):
    lanes = s1.shape[1]
    row8 = lax.broadcasted_iota(jnp.int32, (8, lanes), 0)
    slabs = PEER_KEYS // 8
    t1 = _top16_sorted([s1[8 * j:8 * j + 8] for j in range(slabs)])
    t2 = _top16_sorted([s2[8 * j:8 * j + 8] for j in range(slabs)])
    v1_lo, v1_hi, v2_hi = _sublane_pick(t1[:8]), _sublane_pick(t1[8:]), _sublane_pick(t2[8:])
    groups = [v1_lo + t2[0], v1_hi + t2[0]]
    for b in range(1, 8):
        groups.append(jnp.where(row8 < PEER_TOPK // (b + 1), v1_lo + t2[b], NEG))
    groups.append(t1[0] + v2_hi)
    best = _top16_sorted(groups)
    tau = best[PEER_TOPK - 1]
    zsum = jnp.ones_like(tau)
    for r in range(1, PEER_TOPK):
        zsum = zsum + jnp.exp(best[r] - best[0])
    inv_z = 1.0 / zsum
    n1_top_hi = jnp.zeros_like(tau)
    for b in range(PEER_TOPK // 2, PEER_TOPK):
        n1_top_hi = n1_top_hi + jnp.where(t1[0] + t2[b] >= tau, 1.0, 0.0)
    for j in range(PEER_KEYS // 16):
        words = slice(8 * j, 8 * j + 8)
        rank2, w2 = [], []
        for half in range(2):
            ks = slice(16 * j + 8 * half, 16 * j + 8 * half + 8)
            s1k, s2k = s1[ks], s2[ks]
            n1 = jnp.where(s1k >= t1[0], n1_top_hi, 0.0)
            rk = jnp.full((8, lanes), float(PEER_TOPK), F32)
            for b in reversed(range(PEER_TOPK)):
                if b < PEER_TOPK // 2:
                    n1 = n1 + jnp.where(s1k + t2[b] >= tau, 1.0, 0.0)
                rk = jnp.where(s2k >= t2[b], float(b), rk)
            n1_ref[ks, :] = _dup_bf16_words(n1)
            w1_ref[ks, :] = _dup_bf16_words(jnp.exp(s1k - t1[0]) * inv_z)
            rank2.append(rk)
            w2.append(jnp.exp(s2k - t2[0]))
        r2_ref[words, :] = pltpu.bitcast(jnp.concatenate(rank2, axis=0).astype(BF16), jnp.uint32)
        w2_ref[words, :] = pltpu.bitcast(jnp.concatenate(w2, axis=0).astype(BF16), jnp.uint32)


def _out_router(yabc, yd, x2d, wo1, wo2, g2, wq, keys):
    t = x2d.shape[0]
    tm = min(ROUTER_ROWS, t)

    def whole(arr):
        return pl.BlockSpec(arr.shape, lambda i: (0,) * arr.ndim)

    row_spec = pl.BlockSpec((PEER_HEADS, PEER_KEYS, tm), lambda i: (0, 0, i))
    row_shape = jax.ShapeDtypeStruct((PEER_HEADS, PEER_KEYS, t), jnp.uint32)
    tile_spec = pl.BlockSpec((PEER_HEADS, PEER_KEYS // 2, tm), lambda i: (0, 0, i))
    tile_shape = jax.ShapeDtypeStruct((PEER_HEADS, PEER_KEYS // 2, t), jnp.uint32)
    return pl.pallas_call(
        _out_router_kernel,
        grid=(t // tm,),
        in_specs=[pl.BlockSpec((tm, 3 * GROUP_WIDTH), lambda i: (i, 0)),
                  pl.BlockSpec((tm, GROUP_WIDTH), lambda i: (i, 0)),
                  pl.BlockSpec((tm, D_MODEL), lambda i: (i, 0)),
                  whole(wo1), whole(wo2), whole(g2), whole(wq), whole(keys)],
        out_specs=[pl.BlockSpec((tm, D_MODEL), lambda i: (i, 0)),
                   pl.BlockSpec((tm // 2, D_MODEL), lambda i: (i, 0)),
                   row_spec, row_spec, tile_spec, tile_spec],
        out_shape=[jax.ShapeDtypeStruct((t, D_MODEL), F32),
                   jax.ShapeDtypeStruct((t // 2, D_MODEL), jnp.uint32),
                   row_shape, row_shape, tile_shape, tile_shape],
        compiler_params=pltpu.CompilerParams(dimension_semantics=("parallel",),
                                             vmem_limit_bytes=VMEM_LIMIT),
        name="out_router",
    )(yabc, yd, x2d, wo1, wo2, g2, wq, keys)


def _gate_activations(h_ref, a_ref, n1_ref, w1_ref, r2_ref, w2_ref, first_row, rows):
    tb = h_ref.shape[1]
    zero = jnp.zeros((), BF16)
    half = PEER_KEYS // 2
    for lc in range(tb // 128):
        ls = slice(lc * 128, (lc + 1) * 128)
        n1_rows = [n1_ref[hd, pl.ds(first_row, rows), ls] for hd in range(PEER_HEADS)]
        w1_rows = [w1_ref[hd, pl.ds(first_row, rows), ls] for hd in range(PEER_HEADS)]
        for r in range(rows):
            w = None
            for hd in range(PEER_HEADS):
                n1b = pltpu.bitcast(jnp.broadcast_to(n1_rows[hd][r:r + 1], (half, 128)), BF16)
                w1b = pltpu.bitcast(jnp.broadcast_to(w1_rows[hd][r:r + 1], (half, 128)), BF16)
                rank2 = pltpu.bitcast(r2_ref[hd, :, ls], BF16)
                w2 = pltpu.bitcast(w2_ref[hd, :, ls], BF16)
                term = jnp.where(rank2 < n1b, w2 * w1b, zero)
                w = term if w is None else w + term
            g = _gelu(h_ref[r * PEER_KEYS:(r + 1) * PEER_KEYS, ls]).astype(BF16)
            a_ref[r * half:(r + 1) * half, ls] = pltpu.bitcast(w * g, jnp.uint32)


def _peer_kernel(x_ref, xn_ref, u_ref, vt_ref, n1_ref, w1_ref, r2_ref, w2_ref, o_ref,
                 acc_ref, h_ref, a_ref, *, rows):
    j = pl.program_id(1)

    @pl.when(j == 0)
    def _():
        acc_ref[...] = jnp.zeros_like(acc_ref)

    xn = pltpu.bitcast(xn_ref[...], BF16)
    for m in range(0, u_ref.shape[0], PEER_H_ROWS):
        h_ref[m:m + PEER_H_ROWS, :] = lax.dot_general(u_ref[m:m + PEER_H_ROWS, :], xn, NT_DIMS,
                                                      preferred_element_type=F32)
    first_row = pl.multiple_of(j * rows, rows)
    _gate_activations(h_ref, a_ref, n1_ref, w1_ref, r2_ref, w2_ref, first_row, rows)
    acc_ref[...] += jnp.dot(vt_ref[...], pltpu.bitcast(a_ref[...], BF16), preferred_element_type=F32)

    @pl.when(j == pl.num_programs(1) - 1)
    def _():
        o_ref[...] = x_ref[...] + acc_ref[...].T


def _peer(x1, xn2, u_bf16, vt_bf16, n1, w1, r2, w2):
    t = x1.shape[0]
    tb = min(PEER_TOKENS, t)
    ec = PEER_EXPERT_CHUNK
    rows = ec // PEER_KEYS
    assert rows % 8 == 0 and tb % 128 == 0 and t % tb == 0 and PEER_EXPERTS % ec == 0
    row_spec = pl.BlockSpec((PEER_HEADS, PEER_KEYS, tb), lambda i, j: (0, 0, i))
    tile_spec = pl.BlockSpec((PEER_HEADS, PEER_KEYS // 2, tb), lambda i, j: (0, 0, i))
    kernel = functools.partial(_peer_kernel, rows=rows)
    return pl.pallas_call(
        kernel,
        grid=(t // tb, PEER_EXPERTS // ec),
        in_specs=[pl.BlockSpec((tb, D_MODEL), lambda i, j: (i, 0)),
                  pl.BlockSpec((tb // 2, D_MODEL), lambda i, j: (i, 0)),
                  pl.BlockSpec((ec, D_MODEL), lambda i, j: (j, 0)),
                  pl.BlockSpec((D_MODEL, ec), lambda i, j: (0, j)),
                  row_spec, row_spec, tile_spec, tile_spec],
        out_specs=pl.BlockSpec((tb, D_MODEL), lambda i, j: (i, 0)),
        out_shape=jax.ShapeDtypeStruct((t, D_MODEL), F32),
        scratch_shapes=[pltpu.VMEM((D_MODEL, tb), F32),
                        pltpu.VMEM((ec, tb), F32),
                        pltpu.VMEM((ec // 2, tb), jnp.uint32)],
        compiler_params=pltpu.CompilerParams(dimension_semantics=("parallel", "arbitrary"),
                                             vmem_limit_bytes=VMEM_LIMIT),
        name="peer",
    )(x1, xn2, u_bf16, vt_bf16, n1, w1, r2, w2)


def _block_diag_ones(width, group):
    idx = np.arange(width) // group
    return jnp.asarray((idx[:, None] == idx[None, :]).astype(np.float32), BF16)


def _layer(x2d, batch, seq, layer_idx, norm1_g, w_in, w_pool, swa_q_norm, swa_k_norm, swa_sinks, conv_w,
           diff_q_norm, diff_k_norm, lam_q1, lam_k1, lam_q2, lam_k2, out_norm_g, w_out, norm2_g, w_query,
           sub_keys, peer_u, peer_v):
    lam_init = 0.8 - 0.6 * float(np.exp(-0.3 * layer_idx))
    bd64 = _block_diag_ones(GROUP_WIDTH, HEAD_DIM)
    bd32 = _block_diag_ones(GROUP_WIDTH, DIFF_QK_DIM)
    wpool_bd = jax.scipy.linalg.block_diag(*[w_pool[g] for g in range(len(POOL_WINDOWS))]).astype(BF16)

    z = _in_proj(x2d, norm1_g.reshape(1, D_MODEL), w_in.astype(BF16))
    yabc, qdn, kdn = _local_mix(
        z, batch, seq, swa_sinks.astype(F32), wpool_bd, bd64, bd32,
        jnp.tile(swa_q_norm, SWA_Q_HEADS).reshape(1, GROUP_WIDTH),
        jnp.tile(swa_k_norm, 2).reshape(1, 128),
        conv_w, out_norm_g[:3 * GROUP_WIDTH].reshape(1, 3 * GROUP_WIDTH),
        jnp.tile(diff_q_norm, 2 * DIFF_HEADS).reshape(1, GROUP_WIDTH),
        jnp.tile(diff_k_norm, 2 * DIFF_HEADS).reshape(1, GROUP_WIDTH))
    lamv = jnp.stack([lam_q1, lam_k1, lam_q2, lam_k2]).astype(F32)
    yd = _diff_attn(qdn, kdn, z, batch, seq, lamv, bd64,
                    out_norm_g[3 * GROUP_WIDTH:].reshape(1, GROUP_WIDTH), lam_init)
    w_out_bf16 = w_out.astype(BF16)
    x1, xn2, n1, w1, r2, w2 = _out_router(
        yabc, yd, x2d, w_out_bf16[:3 * GROUP_WIDTH], w_out_bf16[3 * GROUP_WIDTH:],
        norm2_g.reshape(1, D_MODEL), w_query.astype(BF16), sub_keys.astype(BF16))
    return _peer(x1, xn2, peer_u.astype(BF16), peer_v.T.astype(BF16), n1, w1, r2, w2)


def kernel(x, norm1_g, w_in, w_pool, swa_q_norm, swa_k_norm, swa_sinks, conv_w, diff_q_norm, diff_k_norm,
           lam_q1, lam_k1, lam_q2, lam_k2, out_norm_g, w_out, norm2_g, w_query, sub_keys, peer_u, peer_v):
    batch, seq, d = x.shape
    x2d = x.reshape(batch * seq, d)
    depth = norm1_g.shape[0]
    for l in range(depth):
        x2d = _layer(x2d, batch, seq, l, norm1_g[l], w_in[l], w_pool[l], swa_q_norm[l], swa_k_norm[l],
                     swa_sinks[l], conv_w[l], diff_q_norm[l], diff_k_norm[l], lam_q1[l], lam_k1[l],
                     lam_q2[l], lam_k2[l], out_norm_g[l], w_out[l], norm2_g[l], w_query[l], sub_keys[l],
                     peer_u[l], peer_v[l])
    return x2d.reshape(batch, seq, d)
```

```python
import functools

import jax
import jax.numpy as jnp
import numpy as np
from jax import lax
from jax.experimental import pallas as pl
from jax.experimental.pallas import tpu as pltpu

F32 = jnp.float32
BF16 = jnp.bfloat16

D_MODEL = 1024
EPS = 1e-6
HEAD_DIM = 64
GROUP_WIDTH = 256
POOL_WINDOWS = (2, 4, 8, 16)
SWA_Q_HEADS = 4
SWA_GQA = 2
SWA_WINDOW = 128
DIFF_HEADS = 4
DIFF_QK_DIM = 32
IN_PROJ_WIDTH = 2304
PEER_HEADS = 8
PEER_KEYS = 128
PEER_EXPERTS = PEER_KEYS * PEER_KEYS
PEER_HALF = 128
PEER_TOPK = 16
ALIBI_SLOPES = tuple(2.0 ** (-(i + 1)) for i in range(SWA_Q_HEADS + DIFF_HEADS))

LANES = 128
SUBLANES = 8
BF16_ROWS = 16

NEG = -1e30
LOG2_E = 1.4426950408889634
HALO = 16
ROW_BLOCK = 128
DIFF_Q_BLOCK = 256
IN_PROJ_ROWS = 1024
ROUTER_ROWS = 512
ROUTER_LANES = 256
PEER_TOKENS = 512
PEER_EXPERT_CHUNK = 2048
PEER_H_ROWS = 1024
VMEM_LIMIT = 56 * 1024 * 1024

NT_DIMS = (((1,), (1,)), ((), ()))


def _group_sumsq(x, bd):
    x2 = x * x
    hi = x2.astype(BF16)
    lo = (x2 - hi.astype(F32)).astype(BF16)
    return (jnp.dot(hi, bd, preferred_element_type=F32)
            + jnp.dot(lo, bd, preferred_element_type=F32))


def _group_rms_norm(x, bd, group, gain):
    return x * lax.rsqrt(_group_sumsq(x, bd) * (1.0 / group) + EPS) * gain


def _gelu(h):
    return 0.5 * h * (1.0 + lax.erf(h * np.float32(np.sqrt(0.5))))


def _in_proj_kernel(x_ref, g_ref, w_ref, z_ref):
    x = x_ref[...]
    xn = x * lax.rsqrt(jnp.mean(x * x, axis=-1, keepdims=True) + EPS) * g_ref[...]
    z = jnp.dot(xn.astype(BF16), w_ref[...], preferred_element_type=F32)
    z_ref[...] = z.astype(z_ref.dtype)


def _in_proj(x2d, g, w_bf16):
    t = x2d.shape[0]
    tm = min(IN_PROJ_ROWS, t)
    return pl.pallas_call(
        _in_proj_kernel,
        grid=(t // tm,),
        in_specs=[pl.BlockSpec((tm, D_MODEL), lambda i: (i, 0)),
                  pl.BlockSpec((1, D_MODEL), lambda i: (0, 0)),
                  pl.BlockSpec((D_MODEL, IN_PROJ_WIDTH), lambda i: (0, 0))],
        out_specs=pl.BlockSpec((tm, IN_PROJ_WIDTH), lambda i: (i, 0)),
        out_shape=jax.ShapeDtypeStruct((t, IN_PROJ_WIDTH), BF16),
        compiler_params=pltpu.CompilerParams(dimension_semantics=("parallel",),
                                             vmem_limit_bytes=VMEM_LIMIT),
        name="in_proj",
    )(x2d, g, w_bf16)


def _local_mix_kernel(sinks_ref, a_ref, a_halo_ref, bq_ref, bkv_ref, bkv_prev_ref, ch_ref, ch_halo_ref,
                      cb_ref, cc_ref, cc_halo_ref, dq_ref, dk_ref,
                      wpool_ref, bd64_ref, bd32_ref, gq_swa_ref, gk_swa_ref, conv_w_ref, g_out_ref,
                      gq_diff_ref, gk_diff_ref,
                      y_ref, qdn_ref, kdn_ref):
    n = pl.program_id(1)
    not_first = (n > 0).astype(F32)
    bd64 = bd64_ref[...]
    rows = lax.broadcasted_iota(jnp.int32, (ROW_BLOCK, GROUP_WIDTH), 0)
    cols = lax.broadcasted_iota(jnp.int32, (ROW_BLOCK, GROUP_WIDTH), 1)

    p = a_ref[...].astype(F32)
    p_ext = jnp.concatenate([a_halo_ref[...].astype(F32) * not_first, p], axis=0)
    s2 = p_ext + pltpu.roll(p_ext, 1, axis=0)
    s4 = s2 + pltpu.roll(s2, 2, axis=0)
    s8 = s4 + pltpu.roll(s4, 4, axis=0)
    s16 = s8 + pltpu.roll(s8, 8, axis=0)
    wsel = jnp.where(cols < 64, 2, jnp.where(cols < 128, 4, jnp.where(cols < 192, 8, 16)))
    ssel = jnp.where(cols < 64, s2[HALO:], jnp.where(cols < 128, s4[HALO:],
                                                      jnp.where(cols < 192, s8[HALO:], s16[HALO:])))
    cnt = jnp.minimum(n * ROW_BLOCK + rows + 1, wsel).astype(F32)
    d = ssel / cnt - p
    y_a = jnp.dot(d.astype(BF16), wpool_ref[...], preferred_element_type=F32)

    q = _group_rms_norm(bq_ref[...].astype(F32), bd64, HEAD_DIM, gq_swa_ref[...]) * (HEAD_DIM ** -0.5)
    qn = q.astype(BF16)
    bd64k = bd64[:128, :128]
    kv_cur = bkv_ref[...]
    kv_prev = bkv_prev_ref[...]
    k_cur = _group_rms_norm(kv_cur[:, :128].astype(F32), bd64k, HEAD_DIM, gk_swa_ref[...])
    k_prev = _group_rms_norm(kv_prev[:, :128].astype(F32), bd64k, HEAD_DIM, gk_swa_ref[...])
    kn = jnp.concatenate([k_prev, k_cur], axis=0).astype(BF16)
    v = jnp.concatenate([kv_prev[:, 128:], kv_cur[:, 128:]], axis=0)
    qi = lax.broadcasted_iota(jnp.int32, (ROW_BLOCK, 2 * ROW_BLOCK), 0)
    kj = lax.broadcasted_iota(jnp.int32, (ROW_BLOCK, 2 * ROW_BLOCK), 1)
    dist = qi + ROW_BLOCK - kj
    valid = (dist >= 0) & (dist < SWA_WINDOW) & ((n - 1) * ROW_BLOCK + kj >= 0)
    distf = dist.astype(F32)
    heads = []
    for h in range(SWA_Q_HEADS):
        hk = h // SWA_GQA
        sc = lax.dot_general(qn[:, h * 64:(h + 1) * 64], kn[:, hk * 64:(hk + 1) * 64], NT_DIMS,
                             preferred_element_type=F32)
        sc = jnp.where(valid, sc - ALIBI_SLOPES[h] * distf, NEG)
        sink = sinks_ref[h]
        mx = jnp.maximum(jnp.max(sc, axis=-1, keepdims=True), sink)
        e = jnp.exp(sc - mx)
        den = jnp.sum(e, axis=-1, keepdims=True) + jnp.exp(sink - mx)
        o = jnp.dot(e.astype(BF16), v[:, hk * 64:(hk + 1) * 64], preferred_element_type=F32)
        heads.append(o / den)
    y_b = jnp.concatenate(heads, axis=-1)

    zc = cc_ref[...].astype(F32) * ch_ref[...].astype(F32)
    zc_halo = cc_halo_ref[...].astype(F32) * ch_halo_ref[...].astype(F32) * not_first
    zc_ext = jnp.concatenate([zc_halo, zc], axis=0)
    cw = conv_w_ref[...]
    conv = (cw[0:1] * pltpu.roll(zc_ext, 2, axis=0)[HALO:] + cw[1:2] * pltpu.roll(zc_ext, 1, axis=0)[HALO:]
            + cw[2:3] * zc)
    y_c = cb_ref[...].astype(F32) * conv

    g_out = g_out_ref[...]
    for gi, y in enumerate((y_a, y_b, y_c)):
        lo, hi = gi * GROUP_WIDTH, (gi + 1) * GROUP_WIDTH
        y_ref[:, lo:hi] = _group_rms_norm(y, bd64, HEAD_DIM, g_out[:, lo:hi]).astype(y_ref.dtype)

    bd32 = bd32_ref[...]
    qd = (_group_rms_norm(dq_ref[...].astype(F32), bd32, DIFF_QK_DIM, gq_diff_ref[...])
          * (DIFF_QK_DIM ** -0.5 * LOG2_E))
    kd = _group_rms_norm(dk_ref[...].astype(F32), bd32, DIFF_QK_DIM, gk_diff_ref[...])
    qdn_ref[...] = qd.astype(qdn_ref.dtype)
    kdn_ref[...] = kd.astype(kdn_ref.dtype)


def _local_mix(z, batch, seq, sinks, wpool_bd, bd64, bd32, gq_swa, gk_swa, conv_w, g_out, gq_diff, gk_diff):
    t = batch * seq
    nb = seq // ROW_BLOCK
    halo_per_block = ROW_BLOCK // HALO

    def cur(col):
        return pl.BlockSpec((ROW_BLOCK, GROUP_WIDTH), lambda b, n: (b * nb + n, col))

    def prev(col):
        return pl.BlockSpec((ROW_BLOCK, GROUP_WIDTH), lambda b, n: (b * nb + jnp.maximum(n - 1, 0), col))

    def halo(col):
        return pl.BlockSpec(
            (HALO, GROUP_WIDTH),
            lambda b, n: (jnp.maximum((b * nb + n) * halo_per_block - 1, 0), col))

    def whole(arr):
        return pl.BlockSpec(arr.shape, lambda b, n: (0,) * arr.ndim)

    params = (wpool_bd, bd64, bd32, gq_swa, gk_swa, conv_w, g_out, gq_diff, gk_diff)
    in_specs = ([pl.BlockSpec(memory_space=pltpu.SMEM),
                 cur(0), halo(0), cur(1), cur(2), prev(2), cur(3), halo(3), cur(4), cur(5), halo(5),
                 cur(6), cur(7)] + [whole(a) for a in params])
    out_specs = [pl.BlockSpec((ROW_BLOCK, 3 * GROUP_WIDTH), lambda b, n: (b * nb + n, 0)),
                 pl.BlockSpec((ROW_BLOCK, GROUP_WIDTH), lambda b, n: (b * nb + n, 0)),
                 pl.BlockSpec((ROW_BLOCK, GROUP_WIDTH), lambda b, n: (b * nb + n, 0))]
    out_shape = [jax.ShapeDtypeStruct((t, 3 * GROUP_WIDTH), BF16),
                 jax.ShapeDtypeStruct((t, GROUP_WIDTH), BF16),
                 jax.ShapeDtypeStruct((t, GROUP_WIDTH), BF16)]
    return pl.pallas_call(
        _local_mix_kernel,
        grid=(batch, nb),
        in_specs=in_specs,
        out_specs=out_specs,
        out_shape=out_shape,
        compiler_params=pltpu.CompilerParams(dimension_semantics=("parallel", "parallel"),
                                             vmem_limit_bytes=VMEM_LIMIT),
        name="local_mix",
    )(sinks, *([z] * 12), *params)


def _diff_attn_kernel(q_ref, k_ref, v_ref, lamv_ref, bd64_ref, g_ref, y_ref,
                      qs_ref, vt_ref, m_ref, den_ref, acc_ref, *, one_minus_lam_init, lam_init):
    qi = pl.program_id(1)
    qb = DIFF_Q_BLOCK
    lamv = lamv_ref[...]
    lam = (jnp.exp(jnp.sum(lamv[0:1] * lamv[1:2], axis=-1, keepdims=True))
           - jnp.exp(jnp.sum(lamv[2:3] * lamv[3:4], axis=-1, keepdims=True)) + lam_init)
    n_maps = 2 * DIFF_HEADS

    @pl.when(qi == 0)
    def _():
        for c in range(vt_ref.shape[0]):
            vt_ref[c] = v_ref[c * qb:(c + 1) * qb, :].astype(F32).T.astype(BF16)

    q = q_ref[...]
    lane_group = lax.broadcasted_iota(jnp.int32, (qb, GROUP_WIDTH), 1) // DIFF_QK_DIM
    for g in range(n_maps):
        qs_ref[g * qb:(g + 1) * qb, :] = jnp.where(lane_group == g, q, jnp.zeros_like(q))
    m_ref[...] = jnp.full(m_ref.shape, NEG, F32)
    den_ref[...] = jnp.zeros(den_ref.shape, F32)
    acc_ref[...] = jnp.zeros(acc_ref.shape, F32)
    rel = (lax.broadcasted_iota(jnp.int32, (qb, qb), 1) - lax.broadcasted_iota(jnp.int32, (qb, qb), 0))
    relf = rel.astype(F32)
    slopes = [ALIBI_SLOPES[SWA_Q_HEADS + h] * LOG2_E for h in range(DIFF_HEADS)]
    bias = [slopes[h] * relf for h in range(DIFF_HEADS)]

    def kv_step(c, diagonal):
        start = pl.multiple_of(c * qb, qb)
        sc_all = lax.dot_general(k_ref[pl.ds(start, qb), :], qs_ref[...], NT_DIMS,
                                 preferred_element_type=F32)
        off = ((qi - c) * qb).astype(F32)
        for g in range(n_maps):
            cols = slice(g * qb, (g + 1) * qb)
            sc = sc_all[:, cols] - bias[g // 2]
            if diagonal:
                sc = jnp.where(rel >= 0, sc, NEG)
            shift = slopes[g // 2] * off
            m_old = m_ref[:, cols]
            m_new = jnp.maximum(m_old, jnp.max(sc, axis=0, keepdims=True) - shift)
            alpha = jnp.exp2(m_old - m_new)
            e = jnp.exp2(sc - (m_new + shift))
            den_ref[:, cols] = alpha * den_ref[:, cols] + jnp.sum(e, axis=0, keepdims=True)
            m_ref[:, cols] = m_new
            rows = slice((g // 2) * HEAD_DIM, (g // 2 + 1) * HEAD_DIM)
            pv = jnp.dot(vt_ref[c, rows, :], e.astype(BF16), preferred_element_type=F32)
            acc_ref[:, cols] = alpha * acc_ref[:, cols] + pv

    def off_diagonal(c, carry):
        kv_step(c, False)
        return carry

    lax.fori_loop(0, qi, off_diagonal, 0)
    kv_step(qi, True)

    heads = []
    for h in range(DIFF_HEADS):
        c1 = slice(2 * h * qb, (2 * h + 1) * qb)
        c2 = slice((2 * h + 1) * qb, (2 * h + 2) * qb)
        heads.append(acc_ref[:, c1] / den_ref[:, c1] - lam * (acc_ref[:, c2] / den_ref[:, c2]))
    o = jnp.concatenate(heads, axis=0).T
    y = _group_rms_norm(o, bd64_ref[...], HEAD_DIM, g_ref[...]) * one_minus_lam_init
    y_ref[...] = y.astype(y_ref.dtype)


def _diff_attn(qdn, kdn, z, batch, seq, lamv, bd64, g_d, lam_init):
    t = batch * seq
    nq = seq // DIFF_Q_BLOCK
    v_col = 8
    stacked = 2 * DIFF_HEADS * DIFF_Q_BLOCK
    kernel = functools.partial(_diff_attn_kernel, one_minus_lam_init=1.0 - lam_init, lam_init=lam_init)
    return pl.pallas_call(
        kernel,
        grid=(batch, nq),
        in_specs=[pl.BlockSpec((DIFF_Q_BLOCK, GROUP_WIDTH), lambda b, i: (b * nq + i, 0)),
                  pl.BlockSpec((seq, GROUP_WIDTH), lambda b, i: (b, 0)),
                  pl.BlockSpec((seq, GROUP_WIDTH), lambda b, i: (b, v_col)),
                  pl.BlockSpec(lamv.shape, lambda b, i: (0, 0)),
                  pl.BlockSpec(bd64.shape, lambda b, i: (0, 0)),
                  pl.BlockSpec(g_d.shape, lambda b, i: (0, 0))],
        out_specs=pl.BlockSpec((DIFF_Q_BLOCK, GROUP_WIDTH), lambda b, i: (b * nq + i, 0)),
        out_shape=jax.ShapeDtypeStruct((t, GROUP_WIDTH), BF16),
        scratch_shapes=[pltpu.VMEM((stacked, GROUP_WIDTH), BF16),
                        pltpu.VMEM((seq // DIFF_Q_BLOCK, GROUP_WIDTH, DIFF_Q_BLOCK), BF16),
                        pltpu.VMEM((1, stacked), F32),
                        pltpu.VMEM((1, stacked), F32),
                        pltpu.VMEM((HEAD_DIM, stacked), F32)],
        compiler_params=pltpu.CompilerParams(dimension_semantics=("parallel", "arbitrary"),
                                             vmem_limit_bytes=VMEM_LIMIT),
        name="diff_attn",
    )(qdn, kdn, z, lamv, bd64, g_d)


def _odd_even_merge_sort_pairs(n):
    pairs = []

    def merge(lo, length, r):
        step = 2 * r
        if step < length:
            merge(lo, length, step)
            merge(lo + r, length, step)
            pairs.extend((i, i + r) for i in range(lo + r, lo + length - r, step))
        else:
            pairs.append((lo, lo + r))

    def sort(lo, length):
        if length > 1:
            sort(lo, length // 2)
            sort(lo + length // 2, length // 2)
            merge(lo, length, 1)

    sort(0, n)
    return pairs


_SORT16 = _odd_even_merge_sort_pairs(PEER_TOPK)


def _top16_sorted(slabs):
    n = PEER_TOPK
    assert n // 2 <= len(slabs) <= n
    a = list(slabs) + [None] * (n - len(slabs))

    def exchange(i, j):
        if a[j] is None:
            return
        if a[i] is None:
            a[i], a[j] = a[j], None
        else:
            a[i], a[j] = jnp.maximum(a[i], a[j]), jnp.minimum(a[i], a[j])

    for i, j in _SORT16:
        exchange(i, j)
    for shift in (SUBLANES // 2, SUBLANES // 4, SUBLANES // 8):
        b = [None if x is None else pltpu.roll(x, shift, axis=0) for x in a]
        for j in range(n):
            other = b[n - 1 - j]
            if a[j] is None:
                a[j] = other
            elif other is not None:
                a[j] = jnp.maximum(a[j], other)
        d = n // 2
        while d:
            for i in range(n):
                if not i & d:
                    exchange(i, i + d)
            d //= 2
    return a


def _sublane_pick(arrays):
    row = lax.broadcasted_iota(jnp.int32, arrays[0].shape, 0)
    out = arrays[0]
    for s in range(1, SUBLANES):
        out = jnp.where(row == s, arrays[s], out)
    return out


def _dup_bf16_words(x):
    bits = pltpu.bitcast(x.astype(BF16).astype(F32), jnp.uint32)
    return bits | (bits >> 16)


def _out_router_kernel(yabc_ref, yd_ref, x_ref, wo1_ref, wo2_ref, g2_ref, wq_ref, keys_ref,
                       x1_ref, xn2_ref, n1_ref, w1_ref, r2_ref, w2_ref):
    acc = (jnp.dot(yabc_ref[...], wo1_ref[...], preferred_element_type=F32)
           + jnp.dot(yd_ref[...], wo2_ref[...], preferred_element_type=F32))
    x1 = x_ref[...] + acc
    x1_ref[...] = x1
    xn2 = (x1 * lax.rsqrt(jnp.mean(x1 * x1, axis=-1, keepdims=True) + EPS) * g2_ref[...]).astype(BF16)
    xn2_ref[...] = pltpu.bitcast(xn2, jnp.uint32)
    q = jnp.dot(xn2, wq_ref[...], preferred_element_type=F32)
    tm = q.shape[0]
    for h in range(PEER_HEADS):
        scores = []
        for c in range(2):
            lo = (2 * h + c) * PEER_HALF
            qhc = q[:, lo:lo + PEER_HALF].astype(BF16)
            scores.append(lax.dot_general(keys_ref[c], qhc, NT_DIMS, preferred_element_type=F32))
        for p in range(tm // ROUTER_LANES):
            ls = slice(p * ROUTER_LANES, (p + 1) * ROUTER_LANES)
            _route_head(scores[0][:, ls], scores[1][:, ls],
                        n1_ref.at[h, :, ls], w1_ref.at[h, :, ls], r2_ref.at[h, :, ls], w2_ref.at[h, :, ls])


def _route_head(s1, s2, n1_ref, w1_ref, r2_ref, w2_ref):
    lanes = s1.shape[1]
    sl = SUBLANES
    row8 = lax.broadcasted_iota(jnp.int32, (sl, lanes), 0)
    slabs = PEER_KEYS // sl
    t1 = _top16_sorted([s1[sl * j:sl * j + sl] for j in range(slabs)])
    t2 = _top16_sorted([s2[sl * j:sl * j + sl] for j in range(slabs)])
    v1_lo, v1_hi, v2_hi = _sublane_pick(t1[:sl]), _sublane_pick(t1[sl:]), _sublane_pick(t2[sl:])
    groups = [v1_lo + t2[0], v1_hi + t2[0]]
    for b in range(1, sl):
        groups.append(jnp.where(row8 < PEER_TOPK // (b + 1), v1_lo + t2[b], NEG))
    groups.append(t1[0] + v2_hi)
    best = _top16_sorted(groups)
    tau = best[PEER_TOPK - 1]
    zsum = jnp.ones_like(tau)
    for r in range(1, PEER_TOPK):
        zsum = zsum + jnp.exp(best[r] - best[0])
    inv_z = 1.0 / zsum
    n1_top_hi = jnp.zeros_like(tau)
    for b in range(PEER_TOPK // 2, PEER_TOPK):
        n1_top_hi = n1_top_hi + jnp.where(t1[0] + t2[b] >= tau, 1.0, 0.0)
    for j in range(PEER_KEYS // BF16_ROWS):
        words = slice(sl * j, sl * j + sl)
        rank2, w2 = [], []
        for half in range(2):
            ks = slice(BF16_ROWS * j + sl * half, BF16_ROWS * j + sl * half + sl)
            s1k, s2k = s1[ks], s2[ks]
            n1 = jnp.where(s1k >= t1[0], n1_top_hi, 0.0)
            rk = jnp.full((sl, lanes), float(PEER_TOPK), F32)
            for b in reversed(range(PEER_TOPK)):
                if b < PEER_TOPK // 2:
                    n1 = n1 + jnp.where(s1k + t2[b] >= tau, 1.0, 0.0)
                rk = jnp.where(s2k >= t2[b], float(b), rk)
            n1_ref[ks, :] = _dup_bf16_words(n1)
            w1_ref[ks, :] = _dup_bf16_words(jnp.exp(s1k - t1[0]) * inv_z)
            rank2.append(rk)
            w2.append(jnp.exp(s2k - t2[0]))
        r2_ref[words, :] = pltpu.bitcast(jnp.concatenate(rank2, axis=0).astype(BF16), jnp.uint32)
        w2_ref[words, :] = pltpu.bitcast(jnp.concatenate(w2, axis=0).astype(BF16), jnp.uint32)


def _out_router(yabc, yd, x2d, wo1, wo2, g2, wq, keys):
    t = x2d.shape[0]
    tm = min(ROUTER_ROWS, t)

    def whole(arr):
        return pl.BlockSpec(arr.shape, lambda i: (0,) * arr.ndim)

    row_spec = pl.BlockSpec((PEER_HEADS, PEER_KEYS, tm), lambda i: (0, 0, i))
    row_shape = jax.ShapeDtypeStruct((PEER_HEADS, PEER_KEYS, t), jnp.uint32)
    tile_spec = pl.BlockSpec((PEER_HEADS, PEER_KEYS // 2, tm), lambda i: (0, 0, i))
    tile_shape = jax.ShapeDtypeStruct((PEER_HEADS, PEER_KEYS // 2, t), jnp.uint32)
    return pl.pallas_call(
        _out_router_kernel,
        grid=(t // tm,),
        in_specs=[pl.BlockSpec((tm, 3 * GROUP_WIDTH), lambda i: (i, 0)),
                  pl.BlockSpec((tm, GROUP_WIDTH), lambda i: (i, 0)),
                  pl.BlockSpec((tm, D_MODEL), lambda i: (i, 0)),
                  whole(wo1), whole(wo2), whole(g2), whole(wq), whole(keys)],
        out_specs=[pl.BlockSpec((tm, D_MODEL), lambda i: (i, 0)),
                   pl.BlockSpec((tm // 2, D_MODEL), lambda i: (i, 0)),
                   row_spec, row_spec, tile_spec, tile_spec],
        out_shape=[jax.ShapeDtypeStruct((t, D_MODEL), F32),
                   jax.ShapeDtypeStruct((t // 2, D_MODEL), jnp.uint32),
                   row_shape, row_shape, tile_shape, tile_shape],
        compiler_params=pltpu.CompilerParams(dimension_semantics=("parallel",),
                                             vmem_limit_bytes=VMEM_LIMIT),
        name="out_router",
    )(yabc, yd, x2d, wo1, wo2, g2, wq, keys)


def _gate_activations(h_ref, a_ref, n1_ref, w1_ref, r2_ref, w2_ref, first_row, rows):
    tb = h_ref.shape[1]
    zero = jnp.zeros((), BF16)
    half = PEER_KEYS // 2
    for lc in range(tb // LANES):
        ls = slice(lc * LANES, (lc + 1) * LANES)
        n1_rows = [n1_ref[hd, pl.ds(first_row, rows), ls] for hd in range(PEER_HEADS)]
        w1_rows = [w1_ref[hd, pl.ds(first_row, rows), ls] for hd in range(PEER_HEADS)]
        for r in range(rows):
            w = None
            for hd in range(PEER_HEADS):
                n1b = pltpu.bitcast(jnp.broadcast_to(n1_rows[hd][r:r + 1], (half, LANES)), BF16)
                w1b = pltpu.bitcast(jnp.broadcast_to(w1_rows[hd][r:r + 1], (half, LANES)), BF16)
                rank2 = pltpu.bitcast(r2_ref[hd, :, ls], BF16)
                w2 = pltpu.bitcast(w2_ref[hd, :, ls], BF16)
                term = jnp.where(rank2 < n1b, w2 * w1b, zero)
                w = term if w is None else w + term
            g = _gelu(h_ref[r * PEER_KEYS:(r + 1) * PEER_KEYS, ls]).astype(BF16)
            a_ref[r * half:(r + 1) * half, ls] = pltpu.bitcast(w * g, jnp.uint32)


def _peer_kernel(x_ref, xn_ref, u_ref, vt_ref, n1_ref, w1_ref, r2_ref, w2_ref, o_ref,
                 acc_ref, h_ref, a_ref, *, rows):
    j = pl.program_id(1)

    @pl.when(j == 0)
    def _():
        acc_ref[...] = jnp.zeros_like(acc_ref)

    xn = pltpu.bitcast(xn_ref[...], BF16)
    for m in range(0, u_ref.shape[0], PEER_H_ROWS):
        h_ref[m:m + PEER_H_ROWS, :] = lax.dot_general(u_ref[m:m + PEER_H_ROWS, :], xn, NT_DIMS,
                                                      preferred_element_type=F32)
    first_row = pl.multiple_of(j * rows, rows)
    _gate_activations(h_ref, a_ref, n1_ref, w1_ref, r2_ref, w2_ref, first_row, rows)
    acc_ref[...] += jnp.dot(vt_ref[...], pltpu.bitcast(a_ref[...], BF16), preferred_element_type=F32)

    @pl.when(j == pl.num_programs(1) - 1)
    def _():
        o_ref[...] = x_ref[...] + acc_ref[...].T


def _peer(x1, xn2, u_bf16, vt_bf16, n1, w1, r2, w2):
    t = x1.shape[0]
    tb = min(PEER_TOKENS, t)
    ec = PEER_EXPERT_CHUNK
    rows = ec // PEER_KEYS
    assert rows % SUBLANES == 0 and tb % LANES == 0 and t % tb == 0 and PEER_EXPERTS % ec == 0
    row_spec = pl.BlockSpec((PEER_HEADS, PEER_KEYS, tb), lambda i, j: (0, 0, i))
    tile_spec = pl.BlockSpec((PEER_HEADS, PEER_KEYS // 2, tb), lambda i, j: (0, 0, i))
    kernel = functools.partial(_peer_kernel, rows=rows)
    return pl.pallas_call(
        kernel,
        grid=(t // tb, PEER_EXPERTS // ec),
        in_specs=[pl.BlockSpec((tb, D_MODEL), lambda i, j: (i, 0)),
                  pl.BlockSpec((tb // 2, D_MODEL), lambda i, j: (i, 0)),
                  pl.BlockSpec((ec, D_MODEL), lambda i, j: (j, 0)),
                  pl.BlockSpec((D_MODEL, ec), lambda i, j: (0, j)),
                  row_spec, row_spec, tile_spec, tile_spec],
        out_specs=pl.BlockSpec((tb, D_MODEL), lambda i, j: (i, 0)),
        out_shape=jax.ShapeDtypeStruct((t, D_MODEL), F32),
        scratch_shapes=[pltpu.VMEM((D_MODEL, tb), F32),
                        pltpu.VMEM((ec, tb), F32),
                        pltpu.VMEM((ec // 2, tb), jnp.uint32)],
        compiler_params=pltpu.CompilerParams(dimension_semantics=("parallel", "arbitrary"),
                                             vmem_limit_bytes=VMEM_LIMIT),
        name="peer",
    )(x1, xn2, u_bf16, vt_bf16, n1, w1, r2, w2)


def _block_diag_ones(width, group):
    idx = np.arange(width) // group
    return jnp.asarray((idx[:, None] == idx[None, :]).astype(np.float32), BF16)


def _layer(x2d, batch, seq, layer_idx, norm1_g, w_in, w_pool, swa_q_norm, swa_k_norm, swa_sinks, conv_w,
           diff_q_norm, diff_k_norm, lam_q1, lam_k1, lam_q2, lam_k2, out_norm_g, w_out, norm2_g, w_query,
           sub_keys, peer_u, peer_v):
    lam_init = 0.8 - 0.6 * float(np.exp(-0.3 * layer_idx))
    bd64 = _block_diag_ones(GROUP_WIDTH, HEAD_DIM)
    bd32 = _block_diag_ones(GROUP_WIDTH, DIFF_QK_DIM)
    wpool_bd = jax.scipy.linalg.block_diag(*[w_pool[g] for g in range(len(POOL_WINDOWS))]).astype(BF16)

    z = _in_proj(x2d, norm1_g.reshape(1, D_MODEL), w_in.astype(BF16))
    yabc, qdn, kdn = _local_mix(
        z, batch, seq, swa_sinks.astype(F32), wpool_bd, bd64, bd32,
        jnp.tile(swa_q_norm, SWA_Q_HEADS).reshape(1, GROUP_WIDTH),
        jnp.tile(swa_k_norm, 2).reshape(1, 128),
        conv_w, out_norm_g[:3 * GROUP_WIDTH].reshape(1, 3 * GROUP_WIDTH),
        jnp.tile(diff_q_norm, 2 * DIFF_HEADS).reshape(1, GROUP_WIDTH),
        jnp.tile(diff_k_norm, 2 * DIFF_HEADS).reshape(1, GROUP_WIDTH))
    lamv = jnp.stack([lam_q1, lam_k1, lam_q2, lam_k2]).astype(F32)
    yd = _diff_attn(qdn, kdn, z, batch, seq, lamv, bd64,
                    out_norm_g[3 * GROUP_WIDTH:].reshape(1, GROUP_WIDTH), lam_init)
    w_out_bf16 = w_out.astype(BF16)
    x1, xn2, n1, w1, r2, w2 = _out_router(
        yabc, yd, x2d, w_out_bf16[:3 * GROUP_WIDTH], w_out_bf16[3 * GROUP_WIDTH:],
        norm2_g.reshape(1, D_MODEL), w_query.astype(BF16), sub_keys.astype(BF16))
    return _peer(x1, xn2, peer_u.astype(BF16), peer_v.T.astype(BF16), n1, w1, r2, w2)


def kernel(x, norm1_g, w_in, w_pool, swa_q_norm, swa_k_norm, swa_sinks, conv_w, diff_q_norm, diff_k_norm,
           lam_q1, lam_k1, lam_q2, lam_k2, out_norm_g, w_out, norm2_g, w_query, sub_keys, peer_u, peer_v):
    batch, seq, d = x.shape
    x2d = x.reshape(batch * seq, d)
    depth = norm1_g.shape[0]
    for l in range(depth):
        x2d = _layer(x2d, batch, seq, l, norm1_g[l], w_in[l], w_pool[l], swa_q_norm[l], swa_k_norm[l],
                     swa_sinks[l], conv_w[l], diff_q_norm[l], diff_k_norm[l], lam_q1[l], lam_k1[l],
                     lam_q2[l], lam_k2[l], out_norm_g[l], w_out[l], norm2_g[l], w_query[l], sub_keys[l],
                     peer_u[l], peer_v[l])
    return x2d.reshape(batch, seq, d)
```

```python
import functools

import jax
import jax.numpy as jnp
import numpy as np
from jax import lax
from jax.experimental import pallas as pl
from jax.experimental.pallas import tpu as pltpu

F32 = jnp.float32
BF16 = jnp.bfloat16

D_MODEL = 1024
EPS = 1e-6
HEAD_DIM = 64
GROUP_WIDTH = 256
POOL_WINDOWS = (2, 4, 8, 16)
SWA_Q_HEADS = 4
SWA_GQA = 2
SWA_WINDOW = 128
DIFF_HEADS = 4
DIFF_QK_DIM = 32
IN_PROJ_WIDTH = 2304
PEER_HEADS = 8
PEER_KEYS = 128
PEER_EXPERTS = PEER_KEYS * PEER_KEYS
PEER_HALF = 128
PEER_TOPK = 16
ALIBI_SLOPES = tuple(2.0 ** (-(i + 1)) for i in range(SWA_Q_HEADS + DIFF_HEADS))

LANES = 128
SUBLANES = 8
BF16_ROWS = 16

NEG = -1e30
LOG2_E = 1.4426950408889634
HALO = 16
ROW_BLOCK = 128
DIFF_Q_BLOCK = 256
IN_PROJ_ROWS = 1024
ROUTER_ROWS = 512
ROUTER_LANES = 256
PEER_TOKENS = 512
PEER_EXPERT_CHUNK = 2048
PEER_H_ROWS = 1024
VMEM_LIMIT = 56 * 1024 * 1024

NT_DIMS = (((1,), (1,)), ((), ()))


def _group_sumsq(x, bd):
    x2 = x * x
    hi = x2.astype(BF16)
    lo = (x2 - hi.astype(F32)).astype(BF16)
    return (jnp.dot(hi, bd, preferred_element_type=F32)
            + jnp.dot(lo, bd, preferred_element_type=F32))


def _group_rms_norm(x, bd, group, gain):
    return x * lax.rsqrt(_group_sumsq(x, bd) * (1.0 / group) + EPS) * gain


def _gelu(h):
    return 0.5 * h * (1.0 + lax.erf(h * np.float32(np.sqrt(0.5))))


def _in_proj_kernel(x_ref, g_ref, w_ref, z_ref):
    x = x_ref[...]
    xn = x * lax.rsqrt(jnp.mean(x * x, axis=-1, keepdims=True) + EPS) * g_ref[...]
    z = jnp.dot(xn.astype(BF16), w_ref[...], preferred_element_type=F32)
    z_ref[...] = z.astype(z_ref.dtype)


def _in_proj(x2d, g, w_bf16):
    t = x2d.shape[0]
    tm = min(IN_PROJ_ROWS, t)
    return pl.pallas_call(
        _in_proj_kernel,
        grid=(t // tm,),
        in_specs=[pl.BlockSpec((tm, D_MODEL), lambda i: (i, 0)),
                  pl.BlockSpec((1, D_MODEL), lambda i: (0, 0)),
                  pl.BlockSpec((D_MODEL, IN_PROJ_WIDTH), lambda i: (0, 0))],
        out_specs=pl.BlockSpec((tm, IN_PROJ_WIDTH), lambda i: (i, 0)),
        out_shape=jax.ShapeDtypeStruct((t, IN_PROJ_WIDTH), BF16),
        compiler_params=pltpu.CompilerParams(dimension_semantics=("parallel",),
                                             vmem_limit_bytes=VMEM_LIMIT),
        name="in_proj",
    )(x2d, g, w_bf16)


def _local_mix_kernel(sinks_ref, a_ref, a_halo_ref, bq_ref, bkv_ref, bkv_prev_ref, ch_ref, ch_halo_ref,
                      cb_ref, cc_ref, cc_halo_ref, dq_ref, dk_ref,
                      wpool_ref, bd64_ref, bd32_ref, gq_swa_ref, gk_swa_ref, conv_w_ref, g_out_ref,
                      gq_diff_ref, gk_diff_ref,
                      y_ref, qdn_ref, kdn_ref):
    n = pl.program_id(1)
    not_first = (n > 0).astype(F32)
    bd64 = bd64_ref[...]
    rows = lax.broadcasted_iota(jnp.int32, (ROW_BLOCK, GROUP_WIDTH), 0)
    cols = lax.broadcasted_iota(jnp.int32, (ROW_BLOCK, GROUP_WIDTH), 1)

    p = a_ref[...].astype(F32)
    p_ext = jnp.concatenate([a_halo_ref[...].astype(F32) * not_first, p], axis=0)
    s2 = p_ext + pltpu.roll(p_ext, 1, axis=0)
    s4 = s2 + pltpu.roll(s2, 2, axis=0)
    s8 = s4 + pltpu.roll(s4, 4, axis=0)
    s16 = s8 + pltpu.roll(s8, 8, axis=0)
    wsel = jnp.where(cols < 64, 2, jnp.where(cols < 128, 4, jnp.where(cols < 192, 8, 16)))
    ssel = jnp.where(cols < 64, s2[HALO:], jnp.where(cols < 128, s4[HALO:],
                                                      jnp.where(cols < 192, s8[HALO:], s16[HALO:])))
    cnt = jnp.minimum(n * ROW_BLOCK + rows + 1, wsel).astype(F32)
    d = ssel / cnt - p
    y_a = jnp.dot(d.astype(BF16), wpool_ref[...], preferred_element_type=F32)

    q = _group_rms_norm(bq_ref[...].astype(F32), bd64, HEAD_DIM, gq_swa_ref[...]) * (HEAD_DIM ** -0.5)
    qn = q.astype(BF16)
    bd64k = bd64[:128, :128]
    kv_cur = bkv_ref[...]
    kv_prev = bkv_prev_ref[...]
    k_cur = _group_rms_norm(kv_cur[:, :128].astype(F32), bd64k, HEAD_DIM, gk_swa_ref[...])
    k_prev = _group_rms_norm(kv_prev[:, :128].astype(F32), bd64k, HEAD_DIM, gk_swa_ref[...])
    kn = jnp.concatenate([k_prev, k_cur], axis=0).astype(BF16)
    v = jnp.concatenate([kv_prev[:, 128:], kv_cur[:, 128:]], axis=0)
    qi = lax.broadcasted_iota(jnp.int32, (ROW_BLOCK, 2 * ROW_BLOCK), 0)
    kj = lax.broadcasted_iota(jnp.int32, (ROW_BLOCK, 2 * ROW_BLOCK), 1)
    dist = qi + ROW_BLOCK - kj
    valid = (dist >= 0) & (dist < SWA_WINDOW) & ((n - 1) * ROW_BLOCK + kj >= 0)
    distf = dist.astype(F32)
    heads = []
    for h in range(SWA_Q_HEADS):
        hk = h // SWA_GQA
        sc = lax.dot_general(qn[:, h * 64:(h + 1) * 64], kn[:, hk * 64:(hk + 1) * 64], NT_DIMS,
                             preferred_element_type=F32)
        sc = jnp.where(valid, sc - ALIBI_SLOPES[h] * distf, NEG)
        sink = sinks_ref[h]
        mx = jnp.maximum(jnp.max(sc, axis=-1, keepdims=True), sink)
        e = jnp.exp(sc - mx)
        den = jnp.sum(e, axis=-1, keepdims=True) + jnp.exp(sink - mx)
        o = jnp.dot(e.astype(BF16), v[:, hk * 64:(hk + 1) * 64], preferred_element_type=F32)
        heads.append(o / den)
    y_b = jnp.concatenate(heads, axis=-1)

    zc = cc_ref[...].astype(F32) * ch_ref[...].astype(F32)
    zc_halo = cc_halo_ref[...].astype(F32) * ch_halo_ref[...].astype(F32) * not_first
    zc_ext = jnp.concatenate([zc_halo, zc], axis=0)
    cw = conv_w_ref[...]
    conv = (cw[0:1] * pltpu.roll(zc_ext, 2, axis=0)[HALO:] + cw[1:2] * pltpu.roll(zc_ext, 1, axis=0)[HALO:]
            + cw[2:3] * zc)
    y_c = cb_ref[...].astype(F32) * conv

    g_out = g_out_ref[...]
    for gi, y in enumerate((y_a, y_b, y_c)):
        lo, hi = gi * GROUP_WIDTH, (gi + 1) * GROUP_WIDTH
        y_ref[:, lo:hi] = _group_rms_norm(y, bd64, HEAD_DIM, g_out[:, lo:hi]).astype(y_ref.dtype)

    bd32 = bd32_ref[...]
    qd = (_group_rms_norm(dq_ref[...].astype(F32), bd32, DIFF_QK_DIM, gq_diff_ref[...])
          * (DIFF_QK_DIM ** -0.5 * LOG2_E))
    kd = _group_rms_norm(dk_ref[...].astype(F32), bd32, DIFF_QK_DIM, gk_diff_ref[...])
    qdn_ref[...] = qd.astype(qdn_ref.dtype)
    kdn_ref[...] = kd.astype(kdn_ref.dtype)


def _local_mix(z, batch, seq, sinks, wpool_bd, bd64, bd32, gq_swa, gk_swa, conv_w, g_out, gq_diff, gk_diff):
    t = batch * seq
    nb = seq // ROW_BLOCK
    halo_per_block = ROW_BLOCK // HALO

    def cur(col):
        return pl.BlockSpec((ROW_BLOCK, GROUP_WIDTH), lambda b, n: (b * nb + n, col))

    def prev(col):
        return pl.BlockSpec((ROW_BLOCK, GROUP_WIDTH), lambda b, n: (b * nb + jnp.maximum(n - 1, 0), col))

    def halo(col):
        return pl.BlockSpec(
            (HALO, GROUP_WIDTH),
            lambda b, n: (jnp.maximum((b * nb + n) * halo_per_block - 1, 0), col))

    def whole(arr):
        return pl.BlockSpec(arr.shape, lambda b, n: (0,) * arr.ndim)

    params = (wpool_bd, bd64, bd32, gq_swa, gk_swa, conv_w, g_out, gq_diff, gk_diff)
    in_specs = ([pl.BlockSpec(memory_space=pltpu.SMEM),
                 cur(0), halo(0), cur(1), cur(2), prev(2), cur(3), halo(3), cur(4), cur(5), halo(5),
                 cur(6), cur(7)] + [whole(a) for a in params])
    out_specs = [pl.BlockSpec((ROW_BLOCK, 3 * GROUP_WIDTH), lambda b, n: (b * nb + n, 0)),
                 pl.BlockSpec((ROW_BLOCK, GROUP_WIDTH), lambda b, n: (b * nb + n, 0)),
                 pl.BlockSpec((ROW_BLOCK, GROUP_WIDTH), lambda b, n: (b * nb + n, 0))]
    out_shape = [jax.ShapeDtypeStruct((t, 3 * GROUP_WIDTH), BF16),
                 jax.ShapeDtypeStruct((t, GROUP_WIDTH), BF16),
                 jax.ShapeDtypeStruct((t, GROUP_WIDTH), BF16)]
    return pl.pallas_call(
        _local_mix_kernel,
        grid=(batch, nb),
        in_specs=in_specs,
        out_specs=out_specs,
        out_shape=out_shape,
        compiler_params=pltpu.CompilerParams(dimension_semantics=("parallel", "parallel"),
                                             vmem_limit_bytes=VMEM_LIMIT),
        name="local_mix",
    )(sinks, *([z] * 12), *params)


def _diff_attn_kernel(q_ref, k_ref, v_ref, lamv_ref, bd64_ref, g_ref, y_ref,
                      qs_ref, vt_ref, m_ref, den_ref, acc_ref, *, one_minus_lam_init, lam_init):
    qi = pl.program_id(1)
    qb = DIFF_Q_BLOCK
    lamv = lamv_ref[...]
    lam = (jnp.exp(jnp.sum(lamv[0:1] * lamv[1:2], axis=-1, keepdims=True))
           - jnp.exp(jnp.sum(lamv[2:3] * lamv[3:4], axis=-1, keepdims=True)) + lam_init)
    n_maps = 2 * DIFF_HEADS

    @pl.when(qi == 0)
    def _():
        for c in range(vt_ref.shape[0]):
            vt_ref[c] = v_ref[c * qb:(c + 1) * qb, :].astype(F32).T.astype(BF16)

    q = q_ref[...]
    lane_group = lax.broadcasted_iota(jnp.int32, (qb, GROUP_WIDTH), 1) // DIFF_QK_DIM
    for g in range(n_maps):
        qs_ref[g * qb:(g + 1) * qb, :] = jnp.where(lane_group == g, q, jnp.zeros_like(q))
    m_ref[...] = jnp.full(m_ref.shape, NEG, F32)
    den_ref[...] = jnp.zeros(den_ref.shape, F32)
    acc_ref[...] = jnp.zeros(acc_ref.shape, F32)
    rel = (lax.broadcasted_iota(jnp.int32, (qb, qb), 1) - lax.broadcasted_iota(jnp.int32, (qb, qb), 0))
    relf = rel.astype(F32)
    slopes = [ALIBI_SLOPES[SWA_Q_HEADS + h] * LOG2_E for h in range(DIFF_HEADS)]
    bias = [slopes[h] * relf for h in range(DIFF_HEADS)]

    def kv_step(c, diagonal):
        start = pl.multiple_of(c * qb, qb)
        sc_all = lax.dot_general(k_ref[pl.ds(start, qb), :], qs_ref[...], NT_DIMS,
                                 preferred_element_type=F32)
        off = ((qi - c) * qb).astype(F32)
        for g in range(n_maps):
            cols = slice(g * qb, (g + 1) * qb)
            sc = sc_all[:, cols] - bias[g // 2]
            if diagonal:
                sc = jnp.where(rel >= 0, sc, NEG)
            shift = slopes[g // 2] * off
            m_old = m_ref[:, cols]
            m_new = jnp.maximum(m_old, jnp.max(sc, axis=0, keepdims=True) - shift)
            alpha = jnp.exp2(m_old - m_new)
            e = jnp.exp2(sc - (m_new + shift))
            den_ref[:, cols] = alpha * den_ref[:, cols] + jnp.sum(e, axis=0, keepdims=True)
            m_ref[:, cols] = m_new
            rows = slice((g // 2) * HEAD_DIM, (g // 2 + 1) * HEAD_DIM)
            pv = jnp.dot(vt_ref[c, rows, :], e.astype(BF16), preferred_element_type=F32)
            acc_ref[:, cols] = alpha * acc_ref[:, cols] + pv

    def off_diagonal_pair(p, carry):
        kv_step(2 * p, False)
        kv_step(2 * p + 1, False)
        return carry

    lax.fori_loop(0, qi // 2, off_diagonal_pair, 0)

    @pl.when(qi % 2 == 1)
    def _():
        kv_step(qi - 1, False)

    kv_step(qi, True)

    heads = []
    for h in range(DIFF_HEADS):
        c1 = slice(2 * h * qb, (2 * h + 1) * qb)
        c2 = slice((2 * h + 1) * qb, (2 * h + 2) * qb)
        heads.append(acc_ref[:, c1] / den_ref[:, c1] - lam * (acc_ref[:, c2] / den_ref[:, c2]))
    o = jnp.concatenate(heads, axis=0).T
    y = _group_rms_norm(o, bd64_ref[...], HEAD_DIM, g_ref[...]) * one_minus_lam_init
    y_ref[...] = y.astype(y_ref.dtype)


def _diff_attn(qdn, kdn, z, batch, seq, lamv, bd64, g_d, lam_init):
    t = batch * seq
    nq = seq // DIFF_Q_BLOCK
    v_col = 8
    stacked = 2 * DIFF_HEADS * DIFF_Q_BLOCK
    kernel = functools.partial(_diff_attn_kernel, one_minus_lam_init=1.0 - lam_init, lam_init=lam_init)
    return pl.pallas_call(
        kernel,
        grid=(batch, nq),
        in_specs=[pl.BlockSpec((DIFF_Q_BLOCK, GROUP_WIDTH), lambda b, i: (b * nq + i, 0)),
                  pl.BlockSpec((seq, GROUP_WIDTH), lambda b, i: (b, 0)),
                  pl.BlockSpec((seq, GROUP_WIDTH), lambda b, i: (b, v_col)),
                  pl.BlockSpec(lamv.shape, lambda b, i: (0, 0)),
                  pl.BlockSpec(bd64.shape, lambda b, i: (0, 0)),
                  pl.BlockSpec(g_d.shape, lambda b, i: (0, 0))],
        out_specs=pl.BlockSpec((DIFF_Q_BLOCK, GROUP_WIDTH), lambda b, i: (b * nq + i, 0)),
        out_shape=jax.ShapeDtypeStruct((t, GROUP_WIDTH), BF16),
        scratch_shapes=[pltpu.VMEM((stacked, GROUP_WIDTH), BF16),
                        pltpu.VMEM((seq // DIFF_Q_BLOCK, GROUP_WIDTH, DIFF_Q_BLOCK), BF16),
                        pltpu.VMEM((1, stacked), F32),
                        pltpu.VMEM((1, stacked), F32),
                        pltpu.VMEM((HEAD_DIM, stacked), F32)],
        compiler_params=pltpu.CompilerParams(dimension_semantics=("parallel", "arbitrary"),
                                             vmem_limit_bytes=VMEM_LIMIT),
        name="diff_attn",
    )(qdn, kdn, z, lamv, bd64, g_d)


def _odd_even_merge_sort_pairs(n):
    pairs = []

    def merge(lo, length, r):
        step = 2 * r
        if step < length:
            merge(lo, length, step)
            merge(lo + r, length, step)
            pairs.extend((i, i + r) for i in range(lo + r, lo + length - r, step))
        else:
            pairs.append((lo, lo + r))

    def sort(lo, length):
        if length > 1:
            sort(lo, length // 2)
            sort(lo + length // 2, length // 2)
            merge(lo, length, 1)

    sort(0, n)
    return pairs


_SORT16 = _odd_even_merge_sort_pairs(PEER_TOPK)


def _top16_sorted(slabs):
    n = PEER_TOPK
    assert n // 2 <= len(slabs) <= n
    a = list(slabs) + [None] * (n - len(slabs))

    def exchange(i, j):
        if a[j] is None:
            return
        if a[i] is None:
            a[i], a[j] = a[j], None
        else:
            a[i], a[j] = jnp.maximum(a[i], a[j]), jnp.minimum(a[i], a[j])

    for i, j in _SORT16:
        exchange(i, j)
    for shift in (SUBLANES // 2, SUBLANES // 4, SUBLANES // 8):
        b = [None if x is None else pltpu.roll(x, shift, axis=0) for x in a]
        for j in range(n):
            other = b[n - 1 - j]
            if a[j] is None:
                a[j] = other
            elif other is not None:
                a[j] = jnp.maximum(a[j], other)
        d = n // 2
        while d:
            for i in range(n):
                if not i & d:
                    exchange(i, i + d)
            d //= 2
    return a


def _sublane_pick(arrays):
    row = lax.broadcasted_iota(jnp.int32, arrays[0].shape, 0)
    out = arrays[0]
    for s in range(1, SUBLANES):
        out = jnp.where(row == s, arrays[s], out)
    return out


def _dup_bf16_words(x):
    bits = pltpu.bitcast(x.astype(BF16).astype(F32), jnp.uint32)
    return bits | (bits >> 16)


def _out_router_kernel(yabc_ref, yd_ref, x_ref, wo1_ref, wo2_ref, g2_ref, wq_ref, keys_ref,
                       x1_ref, xn2_ref, n1_ref, w1_ref, r2_ref, w2_ref):
    acc = (jnp.dot(yabc_ref[...], wo1_ref[...], preferred_element_type=F32)
           + jnp.dot(yd_ref[...], wo2_ref[...], preferred_element_type=F32))
    x1 = x_ref[...] + acc
    x1_ref[...] = x1
    xn2 = (x1 * lax.rsqrt(jnp.mean(x1 * x1, axis=-1, keepdims=True) + EPS) * g2_ref[...]).astype(BF16)
    xn2_ref[...] = pltpu.bitcast(xn2, jnp.uint32)
    q = jnp.dot(xn2, wq_ref[...], preferred_element_type=F32)
    tm = q.shape[0]
    for h in range(PEER_HEADS):
        scores = []
        for c in range(2):
            lo = (2 * h + c) * PEER_HALF
            qhc = q[:, lo:lo + PEER_HALF].astype(BF16)
            scores.append(lax.dot_general(keys_ref[c], qhc, NT_DIMS, preferred_element_type=F32))
        for p in range(tm // ROUTER_LANES):
            ls = slice(p * ROUTER_LANES, (p + 1) * ROUTER_LANES)
            _route_head(scores[0][:, ls], scores[1][:, ls],
                        n1_ref.at[h, :, ls], w1_ref.at[h, :, ls], r2_ref.at[h, :, ls], w2_ref.at[h, :, ls])


def _route_head(s1, s2, n1_ref, w1_ref, r2_ref, w2_ref):
    lanes = s1.shape[1]
    sl = SUBLANES
    row8 = lax.broadcasted_iota(jnp.int32, (sl, lanes), 0)
    slabs = PEER_KEYS // sl
    t1 = _top16_sorted([s1[sl * j:sl * j + sl] for j in range(slabs)])
    t2 = _top16_sorted([s2[sl * j:sl * j + sl] for j in range(slabs)])
    v1_lo, v1_hi, v2_hi = _sublane_pick(t1[:sl]), _sublane_pick(t1[sl:]), _sublane_pick(t2[sl:])
    groups = [v1_lo + t2[0], v1_hi + t2[0]]
    for b in range(1, sl):
        groups.append(jnp.where(row8 < PEER_TOPK // (b + 1), v1_lo + t2[b], NEG))
    groups.append(t1[0] + v2_hi)
    best = _top16_sorted(groups)
    tau = best[PEER_TOPK - 1]
    zsum = jnp.ones_like(tau)
    for r in range(1, PEER_TOPK):
        zsum = zsum + jnp.exp(best[r] - best[0])
    inv_z = 1.0 / zsum
    dense_b = 4
    top_rows = PEER_TOPK // (dense_b + 1)
    n1_top = []
    for a in range(top_rows):
        count = jnp.zeros_like(tau)
        for b in range(dense_b, PEER_TOPK // (a + 1)):
            count = count + jnp.where(t1[a] + t2[b] >= tau, 1.0, 0.0)
        n1_top.append(count)
    for j in range(PEER_KEYS // BF16_ROWS):
        words = slice(sl * j, sl * j + sl)
        rank2, w2 = [], []
        for half in range(2):
            ks = slice(BF16_ROWS * j + sl * half, BF16_ROWS * j + sl * half + sl)
            s1k, s2k = s1[ks], s2[ks]
            n1 = jnp.zeros((sl, lanes), F32)
            for a in reversed(range(top_rows)):
                n1 = jnp.where(s1k >= t1[a], n1_top[a], n1)
            rk = jnp.full((sl, lanes), float(PEER_TOPK), F32)
            for b in reversed(range(PEER_TOPK)):
                if b < dense_b:
                    n1 = n1 + jnp.where(s1k + t2[b] >= tau, 1.0, 0.0)
                rk = jnp.where(s2k >= t2[b], float(b), rk)
            n1_ref[ks, :] = _dup_bf16_words(n1)
            w1_ref[ks, :] = _dup_bf16_words(jnp.exp(s1k - t1[0]) * inv_z)
            rank2.append(rk)
            w2.append(jnp.exp(s2k - t2[0]))
        r2_ref[words, :] = pltpu.bitcast(jnp.concatenate(rank2, axis=0).astype(BF16), jnp.uint32)
        w2_ref[words, :] = pltpu.bitcast(jnp.concatenate(w2, axis=0).astype(BF16), jnp.uint32)


def _out_router(yabc, yd, x2d, wo1, wo2, g2, wq, keys):
    t = x2d.shape[0]
    tm = min(ROUTER_ROWS, t)

    def whole(arr):
        return pl.BlockSpec(arr.shape, lambda i: (0,) * arr.ndim)

    row_spec = pl.BlockSpec((PEER_HEADS, PEER_KEYS, tm), lambda i: (0, 0, i))
    row_shape = jax.ShapeDtypeStruct((PEER_HEADS, PEER_KEYS, t), jnp.uint32)
    tile_spec = pl.BlockSpec((PEER_HEADS, PEER_KEYS // 2, tm), lambda i: (0, 0, i))
    tile_shape = jax.ShapeDtypeStruct((PEER_HEADS, PEER_KEYS // 2, t), jnp.uint32)
    return pl.pallas_call(
        _out_router_kernel,
        grid=(t // tm,),
        in_specs=[pl.BlockSpec((tm, 3 * GROUP_WIDTH), lambda i: (i, 0)),
                  pl.BlockSpec((tm, GROUP_WIDTH), lambda i: (i, 0)),
                  pl.BlockSpec((tm, D_MODEL), lambda i: (i, 0)),
                  whole(wo1), whole(wo2), whole(g2), whole(wq), whole(keys)],
        out_specs=[pl.BlockSpec((tm, D_MODEL), lambda i: (i, 0)),
                   pl.BlockSpec((tm // 2, D_MODEL), lambda i: (i, 0)),
                   row_spec, row_spec, tile_spec, tile_spec],
        out_shape=[jax.ShapeDtypeStruct((t, D_MODEL), F32),
                   jax.ShapeDtypeStruct((t // 2, D_MODEL), jnp.uint32),
                   row_shape, row_shape, tile_shape, tile_shape],
        compiler_params=pltpu.CompilerParams(dimension_semantics=("parallel",),
                                             vmem_limit_bytes=VMEM_LIMIT),
        name="out_router",
    )(yabc, yd, x2d, wo1, wo2, g2, wq, keys)


def _gate_activations(h_ref, a_ref, n1_ref, w1_ref, r2_ref, w2_ref, first_row, rows):
    tb = h_ref.shape[1]
    zero = jnp.zeros((), BF16)
    half = PEER_KEYS // 2
    for lc in range(tb // LANES):
        ls = slice(lc * LANES, (lc + 1) * LANES)
        n1_rows = [n1_ref[hd, pl.ds(first_row, rows), ls] for hd in range(PEER_HEADS)]
        w1_rows = [w1_ref[hd, pl.ds(first_row, rows), ls] for hd in range(PEER_HEADS)]
        for r in range(rows):
            w = None
            for hd in range(PEER_HEADS):
                n1b = pltpu.bitcast(jnp.broadcast_to(n1_rows[hd][r:r + 1], (half, LANES)), BF16)
                w1b = pltpu.bitcast(jnp.broadcast_to(w1_rows[hd][r:r + 1], (half, LANES)), BF16)
                rank2 = pltpu.bitcast(r2_ref[hd, :, ls], BF16)
                w2 = pltpu.bitcast(w2_ref[hd, :, ls], BF16)
                term = jnp.where(rank2 < n1b, w2 * w1b, zero)
                w = term if w is None else w + term
            g = _gelu(h_ref[r * PEER_KEYS:(r + 1) * PEER_KEYS, ls]).astype(BF16)
            a_ref[r * half:(r + 1) * half, ls] = pltpu.bitcast(w * g, jnp.uint32)


def _peer_kernel(x_ref, xn_ref, u_ref, vt_ref, n1_ref, w1_ref, r2_ref, w2_ref, o_ref,
                 acc_ref, h_ref, a_ref, *, rows):
    j = pl.program_id(1)

    @pl.when(j == 0)
    def _():
        acc_ref[...] = jnp.zeros_like(acc_ref)

    xn = pltpu.bitcast(xn_ref[...], BF16)
    for m in range(0, u_ref.shape[0], PEER_H_ROWS):
        h_ref[m:m + PEER_H_ROWS, :] = lax.dot_general(u_ref[m:m + PEER_H_ROWS, :], xn, NT_DIMS,
                                                      preferred_element_type=F32)
    first_row = pl.multiple_of(j * rows, rows)
    _gate_activations(h_ref, a_ref, n1_ref, w1_ref, r2_ref, w2_ref, first_row, rows)
    acc_ref[...] += jnp.dot(vt_ref[...], pltpu.bitcast(a_ref[...], BF16), preferred_element_type=F32)

    @pl.when(j == pl.num_programs(1) - 1)
    def _():
        o_ref[...] = x_ref[...] + acc_ref[...].T


def _peer(x1, xn2, u_bf16, vt_bf16, n1, w1, r2, w2):
    t = x1.shape[0]
    tb = min(PEER_TOKENS, t)
    ec = PEER_EXPERT_CHUNK
    rows = ec // PEER_KEYS
    assert rows % SUBLANES == 0 and tb % LANES == 0 and t % tb == 0 and PEER_EXPERTS % ec == 0
    row_spec = pl.BlockSpec((PEER_HEADS, PEER_KEYS, tb), lambda i, j: (0, 0, i))
    tile_spec = pl.BlockSpec((PEER_HEADS, PEER_KEYS // 2, tb), lambda i, j: (0, 0, i))
    kernel = functools.partial(_peer_kernel, rows=rows)
    return pl.pallas_call(
        kernel,
        grid=(t // tb, PEER_EXPERTS // ec),
        in_specs=[pl.BlockSpec((tb, D_MODEL), lambda i, j: (i, 0)),
                  pl.BlockSpec((tb // 2, D_MODEL), lambda i, j: (i, 0)),
                  pl.BlockSpec((ec, D_MODEL), lambda i, j: (j, 0)),
                  pl.BlockSpec((D_MODEL, ec), lambda i, j: (0, j)),
                  row_spec, row_spec, tile_spec, tile_spec],
        out_specs=pl.BlockSpec((tb, D_MODEL), lambda i, j: (i, 0)),
        out_shape=jax.ShapeDtypeStruct((t, D_MODEL), F32),
        scratch_shapes=[pltpu.VMEM((D_MODEL, tb), F32),
                        pltpu.VMEM((ec, tb), F32),
                        pltpu.VMEM((ec // 2, tb), jnp.uint32)],
        compiler_params=pltpu.CompilerParams(dimension_semantics=("parallel", "arbitrary"),
                                             vmem_limit_bytes=VMEM_LIMIT),
        name="peer",
    )(x1, xn2, u_bf16, vt_bf16, n1, w1, r2, w2)


def _block_diag_ones(width, group):
    idx = np.arange(width) // group
    return jnp.asarray((idx[:, None] == idx[None, :]).astype(np.float32), BF16)


def _layer(x2d, batch, seq, layer_idx, norm1_g, w_in, w_pool, swa_q_norm, swa_k_norm, swa_sinks, conv_w,
           diff_q_norm, diff_k_norm, lam_q1, lam_k1, lam_q2, lam_k2, out_norm_g, w_out, norm2_g, w_query,
           sub_keys, peer_u, peer_v):
    lam_init = 0.8 - 0.6 * float(np.exp(-0.3 * layer_idx))
    bd64 = _block_diag_ones(GROUP_WIDTH, HEAD_DIM)
    bd32 = _block_diag_ones(GROUP_WIDTH, DIFF_QK_DIM)
    wpool_bd = jax.scipy.linalg.block_diag(*[w_pool[g] for g in range(len(POOL_WINDOWS))]).astype(BF16)

    z = _in_proj(x2d, norm1_g.reshape(1, D_MODEL), w_in.astype(BF16))
    yabc, qdn, kdn = _local_mix(
        z, batch, seq, swa_sinks.astype(F32), wpool_bd, bd64, bd32,
        jnp.tile(swa_q_norm, SWA_Q_HEADS).reshape(1, GROUP_WIDTH),
        jnp.tile(swa_k_norm, 2).reshape(1, 128),
        conv_w, out_norm_g[:3 * GROUP_WIDTH].reshape(1, 3 * GROUP_WIDTH),
        jnp.tile(diff_q_norm, 2 * DIFF_HEADS).reshape(1, GROUP_WIDTH),
        jnp.tile(diff_k_norm, 2 * DIFF_HEADS).reshape(1, GROUP_WIDTH))
    lamv = jnp.stack([lam_q1, lam_k1, lam_q2, lam_k2]).astype(F32)
    yd = _diff_attn(qdn, kdn, z, batch, seq, lamv, bd64,
                    out_norm_g[3 * GROUP_WIDTH:].reshape(1, GROUP_WIDTH), lam_init)
    w_out_bf16 = w_out.astype(BF16)
    x1, xn2, n1, w1, r2, w2 = _out_router(
        yabc, yd, x2d, w_out_bf16[:3 * GROUP_WIDTH], w_out_bf16[3 * GROUP_WIDTH:],
        norm2_g.reshape(1, D_MODEL), w_query.astype(BF16), sub_keys.astype(BF16))
    return _peer(x1, xn2, peer_u.astype(BF16), peer_v.T.astype(BF16), n1, w1, r2, w2)


def kernel(x, norm1_g, w_in, w_pool, swa_q_norm, swa_k_norm, swa_sinks, conv_w, diff_q_norm, diff_k_norm,
           lam_q1, lam_k1, lam_q2, lam_k2, out_norm_g, w_out, norm2_g, w_query, sub_keys, peer_u, peer_v):
    batch, seq, d = x.shape
    x2d = x.reshape(batch * seq, d)
    depth = norm1_g.shape[0]
    for l in range(depth):
        x2d = _layer(x2d, batch, seq, l, norm1_g[l], w_in[l], w_pool[l], swa_q_norm[l], swa_k_norm[l],
                     swa_sinks[l], conv_w[l], diff_q_norm[l], diff_k_norm[l], lam_q1[l], lam_k1[l],
                     lam_q2[l], lam_k2[l], out_norm_g[l], w_out[l], norm2_g[l], w_query[l], sub_keys[l],
                     peer_u[l], peer_v[l])
    return x2d.reshape(batch, seq, d)
```

```python
import functools

import jax
import jax.numpy as jnp
import numpy as np
from jax import lax
from jax.experimental import pallas as pl
from jax.experimental.pallas import tpu as pltpu

F32 = jnp.float32
BF16 = jnp.bfloat16

D_MODEL = 1024
EPS = 1e-6
HEAD_DIM = 64
GROUP_WIDTH = 256
POOL_WINDOWS = (2, 4, 8, 16)
SWA_Q_HEADS = 4
SWA_GQA = 2
SWA_WINDOW = 128
DIFF_HEADS = 4
DIFF_QK_DIM = 32
IN_PROJ_WIDTH = 2304
PEER_HEADS = 8
PEER_KEYS = 128
PEER_EXPERTS = PEER_KEYS * PEER_KEYS
PEER_HALF = 128
PEER_TOPK = 16
ALIBI_SLOPES = tuple(2.0 ** (-(i + 1)) for i in range(SWA_Q_HEADS + DIFF_HEADS))

LANES = 128
SUBLANES = 8
BF16_ROWS = 16

NEG = -1e30
LOG2_E = 1.4426950408889634
HALO = 16
ROW_BLOCK = 128
DIFF_Q_BLOCK = 256
IN_PROJ_ROWS = 1024
ROUTER_ROWS = 512
ROUTER_LANES = 256
PEER_TOKENS = 512
PEER_EXPERT_CHUNK = 2048
PEER_H_ROWS = 1024
VMEM_LIMIT = 56 * 1024 * 1024

NT_DIMS = (((1,), (1,)), ((), ()))


def _group_sumsq(x, bd):
    x2 = x * x
    hi = x2.astype(BF16)
    lo = (x2 - hi.astype(F32)).astype(BF16)
    return (jnp.dot(hi, bd, preferred_element_type=F32)
            + jnp.dot(lo, bd, preferred_element_type=F32))


def _group_rms_norm(x, bd, group, gain):
    return x * lax.rsqrt(_group_sumsq(x, bd) * (1.0 / group) + EPS) * gain


def _gelu(h):
    return 0.5 * h * (1.0 + lax.erf(h * np.float32(np.sqrt(0.5))))


def _in_proj_kernel(x_ref, g_ref, w_ref, z_ref):
    x = x_ref[...]
    xn = x * lax.rsqrt(jnp.mean(x * x, axis=-1, keepdims=True) + EPS) * g_ref[...]
    z = jnp.dot(xn.astype(BF16), w_ref[...], preferred_element_type=F32)
    z_ref[...] = z.astype(z_ref.dtype)


def _in_proj(x2d, g, w_bf16):
    t = x2d.shape[0]
    tm = min(IN_PROJ_ROWS, t)
    return pl.pallas_call(
        _in_proj_kernel,
        grid=(t // tm,),
        in_specs=[pl.BlockSpec((tm, D_MODEL), lambda i: (i, 0)),
                  pl.BlockSpec((1, D_MODEL), lambda i: (0, 0)),
                  pl.BlockSpec((D_MODEL, IN_PROJ_WIDTH), lambda i: (0, 0))],
        out_specs=pl.BlockSpec((tm, IN_PROJ_WIDTH), lambda i: (i, 0)),
        out_shape=jax.ShapeDtypeStruct((t, IN_PROJ_WIDTH), BF16),
        compiler_params=pltpu.CompilerParams(dimension_semantics=("parallel",),
                                             vmem_limit_bytes=VMEM_LIMIT),
        name="in_proj",
    )(x2d, g, w_bf16)


def _local_mix_kernel(sinks_ref, a_ref, a_halo_ref, bq_ref, bkv_ref, bkv_prev_ref, ch_ref, ch_halo_ref,
                      cb_ref, cc_ref, cc_halo_ref, dq_ref, dk_ref,
                      wpool_ref, bd64_ref, bd32_ref, gq_swa_ref, gk_swa_ref, conv_w_ref, g_out_ref,
                      gq_diff_ref, gk_diff_ref,
                      y_ref, qdn_ref, kdn_ref):
    n = pl.program_id(1)
    not_first = (n > 0).astype(F32)
    bd64 = bd64_ref[...]
    rows = lax.broadcasted_iota(jnp.int32, (ROW_BLOCK, GROUP_WIDTH), 0)
    cols = lax.broadcasted_iota(jnp.int32, (ROW_BLOCK, GROUP_WIDTH), 1)

    p = a_ref[...].astype(F32)
    p_ext = jnp.concatenate([a_halo_ref[...].astype(F32) * not_first, p], axis=0)
    s2 = p_ext + pltpu.roll(p_ext, 1, axis=0)
    s4 = s2 + pltpu.roll(s2, 2, axis=0)
    s8 = s4 + pltpu.roll(s4, 4, axis=0)
    s16 = s8 + pltpu.roll(s8, 8, axis=0)
    wsel = jnp.where(cols < 64, 2, jnp.where(cols < 128, 4, jnp.where(cols < 192, 8, 16)))
    ssel = jnp.where(cols < 64, s2[HALO:], jnp.where(cols < 128, s4[HALO:],
                                                      jnp.where(cols < 192, s8[HALO:], s16[HALO:])))
    cnt = jnp.minimum(n * ROW_BLOCK + rows + 1, wsel).astype(F32)
    d = ssel / cnt - p
    y_a = jnp.dot(d.astype(BF16), wpool_ref[...], preferred_element_type=F32)

    q = _group_rms_norm(bq_ref[...].astype(F32), bd64, HEAD_DIM, gq_swa_ref[...]) * (HEAD_DIM ** -0.5)
    qn = q.astype(BF16)
    bd64k = bd64[:128, :128]
    kv_cur = bkv_ref[...]
    kv_prev = bkv_prev_ref[...]
    k_cur = _group_rms_norm(kv_cur[:, :128].astype(F32), bd64k, HEAD_DIM, gk_swa_ref[...])
    k_prev = _group_rms_norm(kv_prev[:, :128].astype(F32), bd64k, HEAD_DIM, gk_swa_ref[...])
    kn = jnp.concatenate([k_prev, k_cur], axis=0).astype(BF16)
    v = jnp.concatenate([kv_prev[:, 128:], kv_cur[:, 128:]], axis=0)
    qi = lax.broadcasted_iota(jnp.int32, (ROW_BLOCK, 2 * ROW_BLOCK), 0)
    kj = lax.broadcasted_iota(jnp.int32, (ROW_BLOCK, 2 * ROW_BLOCK), 1)
    dist = qi + ROW_BLOCK - kj
    valid = (dist >= 0) & (dist < SWA_WINDOW) & ((n - 1) * ROW_BLOCK + kj >= 0)
    distf = dist.astype(F32)
    heads = []
    for h in range(SWA_Q_HEADS):
        hk = h // SWA_GQA
        sc = lax.dot_general(qn[:, h * 64:(h + 1) * 64], kn[:, hk * 64:(hk + 1) * 64], NT_DIMS,
                             preferred_element_type=F32)
        sc = jnp.where(valid, sc - ALIBI_SLOPES[h] * distf, NEG)
        sink = sinks_ref[h]
        mx = jnp.maximum(jnp.max(sc, axis=-1, keepdims=True), sink)
        e = jnp.exp(sc - mx)
        den = jnp.sum(e, axis=-1, keepdims=True) + jnp.exp(sink - mx)
        o = jnp.dot(e.astype(BF16), v[:, hk * 64:(hk + 1) * 64], preferred_element_type=F32)
        heads.append(o / den)
    y_b = jnp.concatenate(heads, axis=-1)

    zc = cc_ref[...].astype(F32) * ch_ref[...].astype(F32)
    zc_halo = cc_halo_ref[...].astype(F32) * ch_halo_ref[...].astype(F32) * not_first
    zc_ext = jnp.concatenate([zc_halo, zc], axis=0)
    cw = conv_w_ref[...]
    conv = (cw[0:1] * pltpu.roll(zc_ext, 2, axis=0)[HALO:] + cw[1:2] * pltpu.roll(zc_ext, 1, axis=0)[HALO:]
            + cw[2:3] * zc)
    y_c = cb_ref[...].astype(F32) * conv

    g_out = g_out_ref[...]
    for gi, y in enumerate((y_a, y_b, y_c)):
        lo, hi = gi * GROUP_WIDTH, (gi + 1) * GROUP_WIDTH
        y_ref[:, lo:hi] = _group_rms_norm(y, bd64, HEAD_DIM, g_out[:, lo:hi]).astype(y_ref.dtype)

    bd32 = bd32_ref[...]
    qd = (_group_rms_norm(dq_ref[...].astype(F32), bd32, DIFF_QK_DIM, gq_diff_ref[...])
          * (DIFF_QK_DIM ** -0.5 * LOG2_E))
    kd = _group_rms_norm(dk_ref[...].astype(F32), bd32, DIFF_QK_DIM, gk_diff_ref[...])
    qdn_ref[...] = qd.astype(qdn_ref.dtype)
    kdn_ref[...] = kd.astype(kdn_ref.dtype)


def _local_mix(z, batch, seq, sinks, wpool_bd, bd64, bd32, gq_swa, gk_swa, conv_w, g_out, gq_diff, gk_diff):
    t = batch * seq
    nb = seq // ROW_BLOCK
    halo_per_block = ROW_BLOCK // HALO

    def cur(col):
        return pl.BlockSpec((ROW_BLOCK, GROUP_WIDTH), lambda b, n: (b * nb + n, col))

    def prev(col):
        return pl.BlockSpec((ROW_BLOCK, GROUP_WIDTH), lambda b, n: (b * nb + jnp.maximum(n - 1, 0), col))

    def halo(col):
        return pl.BlockSpec(
            (HALO, GROUP_WIDTH),
            lambda b, n: (jnp.maximum((b * nb + n) * halo_per_block - 1, 0), col))

    def whole(arr):
        return pl.BlockSpec(arr.shape, lambda b, n: (0,) * arr.ndim)

    params = (wpool_bd, bd64, bd32, gq_swa, gk_swa, conv_w, g_out, gq_diff, gk_diff)
    in_specs = ([pl.BlockSpec(memory_space=pltpu.SMEM),
                 cur(0), halo(0), cur(1), cur(2), prev(2), cur(3), halo(3), cur(4), cur(5), halo(5),
                 cur(6), cur(7)] + [whole(a) for a in params])
    out_specs = [pl.BlockSpec((ROW_BLOCK, 3 * GROUP_WIDTH), lambda b, n: (b * nb + n, 0)),
                 pl.BlockSpec((ROW_BLOCK, GROUP_WIDTH), lambda b, n: (b * nb + n, 0)),
                 pl.BlockSpec((ROW_BLOCK, GROUP_WIDTH), lambda b, n: (b * nb + n, 0))]
    out_shape = [jax.ShapeDtypeStruct((t, 3 * GROUP_WIDTH), BF16),
                 jax.ShapeDtypeStruct((t, GROUP_WIDTH), BF16),
                 jax.ShapeDtypeStruct((t, GROUP_WIDTH), BF16)]
    return pl.pallas_call(
        _local_mix_kernel,
        grid=(batch, nb),
        in_specs=in_specs,
        out_specs=out_specs,
        out_shape=out_shape,
        compiler_params=pltpu.CompilerParams(dimension_semantics=("parallel", "parallel"),
                                             vmem_limit_bytes=VMEM_LIMIT),
        name="local_mix",
    )(sinks, *([z] * 12), *params)


def _diff_attn_kernel(q_ref, k_ref, v_ref, lamv_ref, bd64_ref, g_ref, y_ref,
                      qs_ref, vt_ref, m_ref, den_ref, acc_ref, *, one_minus_lam_init, lam_init):
    qi = pl.program_id(1)
    qb = DIFF_Q_BLOCK
    lamv = lamv_ref[...]
    lam = (jnp.exp(jnp.sum(lamv[0:1] * lamv[1:2], axis=-1, keepdims=True))
           - jnp.exp(jnp.sum(lamv[2:3] * lamv[3:4], axis=-1, keepdims=True)) + lam_init)
    n_maps = 2 * DIFF_HEADS

    @pl.when(qi == 0)
    def _():
        for c in range(vt_ref.shape[0]):
            vt_ref[c] = v_ref[c * qb:(c + 1) * qb, :].astype(F32).T.astype(BF16)

    q = q_ref[...]
    lane_group = lax.broadcasted_iota(jnp.int32, (qb, GROUP_WIDTH), 1) // DIFF_QK_DIM
    for g in range(n_maps):
        qs_ref[g * qb:(g + 1) * qb, :] = jnp.where(lane_group == g, q, jnp.zeros_like(q))
    m_ref[...] = jnp.full(m_ref.shape, NEG, F32)
    den_ref[...] = jnp.zeros(den_ref.shape, F32)
    acc_ref[...] = jnp.zeros(acc_ref.shape, F32)
    rel = (lax.broadcasted_iota(jnp.int32, (qb, qb), 1) - lax.broadcasted_iota(jnp.int32, (qb, qb), 0))
    relf = rel.astype(F32)
    slopes = [ALIBI_SLOPES[SWA_Q_HEADS + h] * LOG2_E for h in range(DIFF_HEADS)]
    bias = [slopes[h] * relf for h in range(DIFF_HEADS)]

    def kv_step(c, diagonal):
        start = pl.multiple_of(c * qb, qb)
        sc_all = lax.dot_general(k_ref[pl.ds(start, qb), :], qs_ref[...], NT_DIMS,
                                 preferred_element_type=F32)
        off = ((qi - c) * qb).astype(F32)
        for g in range(n_maps):
            cols = slice(g * qb, (g + 1) * qb)
            sc = sc_all[:, cols] - bias[g // 2]
            if diagonal:
                sc = jnp.where(rel >= 0, sc, NEG)
            shift = slopes[g // 2] * off
            m_old = m_ref[:, cols]
            m_new = jnp.maximum(m_old, jnp.max(sc, axis=0, keepdims=True) - shift)
            alpha = jnp.exp2(m_old - m_new)
            e = jnp.exp2(sc - (m_new + shift))
            den_ref[:, cols] = alpha * den_ref[:, cols] + jnp.sum(e, axis=0, keepdims=True)
            m_ref[:, cols] = m_new
            rows = slice((g // 2) * HEAD_DIM, (g // 2 + 1) * HEAD_DIM)
            pv = jnp.dot(vt_ref[c, rows, :], e.astype(BF16), preferred_element_type=F32)
            acc_ref[:, cols] = alpha * acc_ref[:, cols] + pv

    def off_diagonal_pair(p, carry):
        kv_step(2 * p, False)
        kv_step(2 * p + 1, False)
        return carry

    lax.fori_loop(0, qi // 2, off_diagonal_pair, 0)

    @pl.when(qi % 2 == 1)
    def _():
        kv_step(qi - 1, False)

    kv_step(qi, True)

    heads = []
    for h in range(DIFF_HEADS):
        c1 = slice(2 * h * qb, (2 * h + 1) * qb)
        c2 = slice((2 * h + 1) * qb, (2 * h + 2) * qb)
        heads.append(acc_ref[:, c1] / den_ref[:, c1] - lam * (acc_ref[:, c2] / den_ref[:, c2]))
    o = jnp.concatenate(heads, axis=0).T
    y = _group_rms_norm(o, bd64_ref[...], HEAD_DIM, g_ref[...]) * one_minus_lam_init
    y_ref[...] = y.astype(y_ref.dtype)


def _diff_attn(qdn, kdn, z, batch, seq, lamv, bd64, g_d, lam_init):
    t = batch * seq
    nq = seq // DIFF_Q_BLOCK
    v_col = 8
    stacked = 2 * DIFF_HEADS * DIFF_Q_BLOCK
    kernel = functools.partial(_diff_attn_kernel, one_minus_lam_init=1.0 - lam_init, lam_init=lam_init)
    return pl.pallas_call(
        kernel,
        grid=(batch, nq),
        in_specs=[pl.BlockSpec((DIFF_Q_BLOCK, GROUP_WIDTH), lambda b, i: (b * nq + i, 0)),
                  pl.BlockSpec((seq, GROUP_WIDTH), lambda b, i: (b, 0)),
                  pl.BlockSpec((seq, GROUP_WIDTH), lambda b, i: (b, v_col)),
                  pl.BlockSpec(lamv.shape, lambda b, i: (0, 0)),
                  pl.BlockSpec(bd64.shape, lambda b, i: (0, 0)),
                  pl.BlockSpec(g_d.shape, lambda b, i: (0, 0))],
        out_specs=pl.BlockSpec((DIFF_Q_BLOCK, GROUP_WIDTH), lambda b, i: (b * nq + i, 0)),
        out_shape=jax.ShapeDtypeStruct((t, GROUP_WIDTH), BF16),
        scratch_shapes=[pltpu.VMEM((stacked, GROUP_WIDTH), BF16),
                        pltpu.VMEM((seq // DIFF_Q_BLOCK, GROUP_WIDTH, DIFF_Q_BLOCK), BF16),
                        pltpu.VMEM((1, stacked), F32),
                        pltpu.VMEM((1, stacked), F32),
                        pltpu.VMEM((HEAD_DIM, stacked), F32)],
        compiler_params=pltpu.CompilerParams(dimension_semantics=("parallel", "arbitrary"),
                                             vmem_limit_bytes=VMEM_LIMIT),
        name="diff_attn",
    )(qdn, kdn, z, lamv, bd64, g_d)


def _odd_even_merge_sort_pairs(n):
    pairs = []

    def merge(lo, length, r):
        step = 2 * r
        if step < length:
            merge(lo, length, step)
            merge(lo + r, length, step)
            pairs.extend((i, i + r) for i in range(lo + r, lo + length - r, step))
        else:
            pairs.append((lo, lo + r))

    def sort(lo, length):
        if length > 1:
            sort(lo, length // 2)
            sort(lo + length // 2, length // 2)
            merge(lo, length, 1)

    sort(0, n)
    return pairs


_SORT16 = _odd_even_merge_sort_pairs(PEER_TOPK)


def _top16_sorted(slabs):
    n = PEER_TOPK
    assert n // 2 <= len(slabs) <= n
    a = list(slabs) + [None] * (n - len(slabs))

    def exchange(i, j):
        if a[j] is None:
            return
        if a[i] is None:
            a[i], a[j] = a[j], None
        else:
            a[i], a[j] = jnp.maximum(a[i], a[j]), jnp.minimum(a[i], a[j])

    for i, j in _SORT16:
        exchange(i, j)
    for shift in (SUBLANES // 2, SUBLANES // 4, SUBLANES // 8):
        b = [None if x is None else pltpu.roll(x, shift, axis=0) for x in a]
        for j in range(n):
            other = b[n - 1 - j]
            if a[j] is None:
                a[j] = other
            elif other is not None:
                a[j] = jnp.maximum(a[j], other)
        d = n // 2
        while d:
            for i in range(n):
                if not i & d:
                    exchange(i, i + d)
            d //= 2
    return a


def _sublane_pick(arrays):
    row = lax.broadcasted_iota(jnp.int32, arrays[0].shape, 0)
    out = arrays[0]
    for s in range(1, SUBLANES):
        out = jnp.where(row == s, arrays[s], out)
    return out


def _dup_bf16_words(x):
    bits = pltpu.bitcast(x.astype(BF16).astype(F32), jnp.uint32)
    return bits | (bits >> 16)


def _out_router_kernel(yabc_ref, yd_ref, x_ref, wo1_ref, wo2_ref, g2_ref, wq_ref, keys_ref,
                       x1_ref, xn2_ref, n1_ref, w1_ref, r2_ref, w2_ref):
    acc = (jnp.dot(yabc_ref[...], wo1_ref[...], preferred_element_type=F32)
           + jnp.dot(yd_ref[...], wo2_ref[...], preferred_element_type=F32))
    x1 = x_ref[...] + acc
    x1_ref[...] = x1
    xn2 = (x1 * lax.rsqrt(jnp.mean(x1 * x1, axis=-1, keepdims=True) + EPS) * g2_ref[...]).astype(BF16)
    xn2_ref[...] = pltpu.bitcast(xn2, jnp.uint32)
    q = jnp.dot(xn2, wq_ref[...], preferred_element_type=F32)
    tm = q.shape[0]
    for h in range(PEER_HEADS):
        scores = []
        for c in range(2):
            lo = (2 * h + c) * PEER_HALF
            qhc = q[:, lo:lo + PEER_HALF].astype(BF16)
            scores.append(lax.dot_general(keys_ref[c], qhc, NT_DIMS, preferred_element_type=F32))
        for p in range(tm // ROUTER_LANES):
            ls = slice(p * ROUTER_LANES, (p + 1) * ROUTER_LANES)
            _route_head(scores[0][:, ls], scores[1][:, ls],
                        n1_ref.at[h, :, ls], w1_ref.at[h, :, ls], r2_ref.at[h, :, ls], w2_ref.at[h, :, ls])


def _route_head(s1, s2, n1_ref, w1_ref, r2_ref, w2_ref):
    lanes = s1.shape[1]
    sl = SUBLANES
    row8 = lax.broadcasted_iota(jnp.int32, (sl, lanes), 0)
    slabs = PEER_KEYS // sl
    t1 = _top16_sorted([s1[sl * j:sl * j + sl] for j in range(slabs)])
    t2 = _top16_sorted([s2[sl * j:sl * j + sl] for j in range(slabs)])
    v1_lo, v1_hi, v2_hi = _sublane_pick(t1[:sl]), _sublane_pick(t1[sl:]), _sublane_pick(t2[sl:])
    groups = [v1_lo + t2[0], v1_hi + t2[0]]
    for b in range(1, sl):
        groups.append(jnp.where(row8 < PEER_TOPK // (b + 1), v1_lo + t2[b], NEG))
    groups.append(t1[0] + v2_hi)
    best = _top16_sorted(groups)
    tau = best[PEER_TOPK - 1]
    zsum = jnp.ones_like(tau)
    for r in range(1, PEER_TOPK):
        zsum = zsum + jnp.exp(best[r] - best[0])
    inv_z = 1.0 / zsum
    dense_b = 4
    top_rows = PEER_TOPK // (dense_b + 1)
    n1_top = []
    for a in range(top_rows):
        count = jnp.zeros_like(tau)
        for b in range(dense_b, PEER_TOPK // (a + 1)):
            count = count + jnp.where(t1[a] + t2[b] >= tau, 1.0, 0.0)
        n1_top.append(count)
    for j in range(PEER_KEYS // BF16_ROWS):
        words = slice(sl * j, sl * j + sl)
        rank2, w2 = [], []
        for half in range(2):
            ks = slice(BF16_ROWS * j + sl * half, BF16_ROWS * j + sl * half + sl)
            s1k, s2k = s1[ks], s2[ks]
            n1 = jnp.zeros((sl, lanes), F32)
            for a in reversed(range(top_rows)):
                n1 = jnp.where(s1k >= t1[a], n1_top[a], n1)
            rk = jnp.full((sl, lanes), float(PEER_TOPK), F32)
            for b in reversed(range(PEER_TOPK)):
                if b < dense_b:
                    n1 = n1 + jnp.where(s1k + t2[b] >= tau, 1.0, 0.0)
                rk = jnp.where(s2k >= t2[b], float(b), rk)
            n1_ref[ks, :] = _dup_bf16_words(n1)
            w1_ref[ks, :] = _dup_bf16_words(jnp.exp(s1k - t1[0]) * inv_z)
            rank2.append(rk)
            w2.append(jnp.exp(s2k - t2[0]))
        r2_ref[words, :] = pltpu.bitcast(jnp.concatenate(rank2, axis=0).astype(BF16), jnp.uint32)
        w2_ref[words, :] = pltpu.bitcast(jnp.concatenate(w2, axis=0).astype(BF16), jnp.uint32)


def _out_router(yabc, yd, x2d, wo1, wo2, g2, wq, keys):
    t = x2d.shape[0]
    tm = min(ROUTER_ROWS, t)

    def whole(arr):
        return pl.BlockSpec(arr.shape, lambda i: (0,) * arr.ndim)

    row_spec = pl.BlockSpec((PEER_HEADS, PEER_KEYS, tm), lambda i: (0, 0, i))
    row_shape = jax.ShapeDtypeStruct((PEER_HEADS, PEER_KEYS, t), jnp.uint32)
    tile_spec = pl.BlockSpec((PEER_HEADS, PEER_KEYS // 2, tm), lambda i: (0, 0, i))
    tile_shape = jax.ShapeDtypeStruct((PEER_HEADS, PEER_KEYS // 2, t), jnp.uint32)
    return pl.pallas_call(
        _out_router_kernel,
        grid=(t // tm,),
        in_specs=[pl.BlockSpec((tm, 3 * GROUP_WIDTH), lambda i: (i, 0)),
                  pl.BlockSpec((tm, GROUP_WIDTH), lambda i: (i, 0)),
                  pl.BlockSpec((tm, D_MODEL), lambda i: (i, 0)),
                  whole(wo1), whole(wo2), whole(g2), whole(wq), whole(keys)],
        out_specs=[pl.BlockSpec((tm, D_MODEL), lambda i: (i, 0)),
                   pl.BlockSpec((tm // 2, D_MODEL), lambda i: (i, 0)),
                   row_spec, row_spec, tile_spec, tile_spec],
        out_shape=[jax.ShapeDtypeStruct((t, D_MODEL), F32),
                   jax.ShapeDtypeStruct((t // 2, D_MODEL), jnp.uint32),
                   row_shape, row_shape, tile_shape, tile_shape],
        compiler_params=pltpu.CompilerParams(dimension_semantics=("parallel",),
                                             vmem_limit_bytes=VMEM_LIMIT),
        name="out_router",
    )(yabc, yd, x2d, wo1, wo2, g2, wq, keys)


def _apply_gates(a_ref, n1_ref, w1_ref, r2_ref, w2_ref, first_row, rows):
    tb = a_ref.shape[1]
    zero = jnp.zeros((), BF16)
    half = PEER_KEYS // 2
    for lc in range(tb // LANES):
        ls = slice(lc * LANES, (lc + 1) * LANES)
        n1_rows = [n1_ref[hd, pl.ds(first_row, rows), ls] for hd in range(PEER_HEADS)]
        w1_rows = [w1_ref[hd, pl.ds(first_row, rows), ls] for hd in range(PEER_HEADS)]
        for r in range(rows):
            w = None
            for hd in range(PEER_HEADS):
                n1b = pltpu.bitcast(jnp.broadcast_to(n1_rows[hd][r:r + 1], (half, LANES)), BF16)
                w1b = pltpu.bitcast(jnp.broadcast_to(w1_rows[hd][r:r + 1], (half, LANES)), BF16)
                rank2 = pltpu.bitcast(r2_ref[hd, :, ls], BF16)
                w2 = pltpu.bitcast(w2_ref[hd, :, ls], BF16)
                term = jnp.where(rank2 < n1b, w2 * w1b, zero)
                w = term if w is None else w + term
            ws = slice(r * half, (r + 1) * half)
            a_ref[ws, ls] = pltpu.bitcast(w * pltpu.bitcast(a_ref[ws, ls], BF16), jnp.uint32)


def _peer_kernel(x_ref, xn_ref, u_ref, vt_ref, n1_ref, w1_ref, r2_ref, w2_ref, o_ref,
                 acc_ref, a_ref, *, rows):
    j = pl.program_id(1)

    @pl.when(j == 0)
    def _():
        acc_ref[...] = jnp.zeros_like(acc_ref)

    xn = pltpu.bitcast(xn_ref[...], BF16)
    for m in range(0, u_ref.shape[0], PEER_H_ROWS):
        hm = lax.dot_general(u_ref[m:m + PEER_H_ROWS, :], xn, NT_DIMS, preferred_element_type=F32)
        a_ref[m // 2:(m + PEER_H_ROWS) // 2, :] = pltpu.bitcast(_gelu(hm).astype(BF16), jnp.uint32)
    first_row = pl.multiple_of(j * rows, rows)
    _apply_gates(a_ref, n1_ref, w1_ref, r2_ref, w2_ref, first_row, rows)
    acc_ref[...] += jnp.dot(vt_ref[...], pltpu.bitcast(a_ref[...], BF16), preferred_element_type=F32)

    @pl.when(j == pl.num_programs(1) - 1)
    def _():
        o_ref[...] = x_ref[...] + acc_ref[...].T


def _peer(x1, xn2, u_bf16, vt_bf16, n1, w1, r2, w2):
    t = x1.shape[0]
    tb = min(PEER_TOKENS, t)
    ec = PEER_EXPERT_CHUNK
    rows = ec // PEER_KEYS
    assert rows % SUBLANES == 0 and tb % LANES == 0 and t % tb == 0 and PEER_EXPERTS % ec == 0
    row_spec = pl.BlockSpec((PEER_HEADS, PEER_KEYS, tb), lambda i, j: (0, 0, i))
    tile_spec = pl.BlockSpec((PEER_HEADS, PEER_KEYS // 2, tb), lambda i, j: (0, 0, i))
    kernel = functools.partial(_peer_kernel, rows=rows)
    return pl.pallas_call(
        kernel,
        grid=(t // tb, PEER_EXPERTS // ec),
        in_specs=[pl.BlockSpec((tb, D_MODEL), lambda i, j: (i, 0)),
                  pl.BlockSpec((tb // 2, D_MODEL), lambda i, j: (i, 0)),
                  pl.BlockSpec((ec, D_MODEL), lambda i, j: (j, 0)),
                  pl.BlockSpec((D_MODEL, ec), lambda i, j: (0, j)),
                  row_spec, row_spec, tile_spec, tile_spec],
        out_specs=pl.BlockSpec((tb, D_MODEL), lambda i, j: (i, 0)),
        out_shape=jax.ShapeDtypeStruct((t, D_MODEL), F32),
        scratch_shapes=[pltpu.VMEM((D_MODEL, tb), F32),
                        pltpu.VMEM((ec // 2, tb), jnp.uint32)],
        compiler_params=pltpu.CompilerParams(dimension_semantics=("parallel", "arbitrary"),
                                             vmem_limit_bytes=VMEM_LIMIT),
        name="peer",
    )(x1, xn2, u_bf16, vt_bf16, n1, w1, r2, w2)


def _block_diag_ones(width, group):
    idx = np.arange(width) // group
    return jnp.asarray((idx[:, None] == idx[None, :]).astype(np.float32), BF16)


def _layer(x2d, batch, seq, layer_idx, norm1_g, w_in, w_pool, swa_q_norm, swa_k_norm, swa_sinks, conv_w,
           diff_q_norm, diff_k_norm, lam_q1, lam_k1, lam_q2, lam_k2, out_norm_g, w_out, norm2_g, w_query,
           sub_keys, peer_u, peer_v):
    lam_init = 0.8 - 0.6 * float(np.exp(-0.3 * layer_idx))
    bd64 = _block_diag_ones(GROUP_WIDTH, HEAD_DIM)
    bd32 = _block_diag_ones(GROUP_WIDTH, DIFF_QK_DIM)
    wpool_bd = jax.scipy.linalg.block_diag(*[w_pool[g] for g in range(len(POOL_WINDOWS))]).astype(BF16)

    z = _in_proj(x2d, norm1_g.reshape(1, D_MODEL), w_in.astype(BF16))
    yabc, qdn, kdn = _local_mix(
        z, batch, seq, swa_sinks.astype(F32), wpool_bd, bd64, bd32,
        jnp.tile(swa_q_norm, SWA_Q_HEADS).reshape(1, GROUP_WIDTH),
        jnp.tile(swa_k_norm, 2).reshape(1, 128),
        conv_w, out_norm_g[:3 * GROUP_WIDTH].reshape(1, 3 * GROUP_WIDTH),
        jnp.tile(diff_q_norm, 2 * DIFF_HEADS).reshape(1, GROUP_WIDTH),
        jnp.tile(diff_k_norm, 2 * DIFF_HEADS).reshape(1, GROUP_WIDTH))
    lamv = jnp.stack([lam_q1, lam_k1, lam_q2, lam_k2]).astype(F32)
    yd = _diff_attn(qdn, kdn, z, batch, seq, lamv, bd64,
                    out_norm_g[3 * GROUP_WIDTH:].reshape(1, GROUP_WIDTH), lam_init)
    w_out_bf16 = w_out.astype(BF16)
    x1, xn2, n1, w1, r2, w2 = _out_router(
        yabc, yd, x2d, w_out_bf16[:3 * GROUP_WIDTH], w_out_bf16[3 * GROUP_WIDTH:],
        norm2_g.reshape(1, D_MODEL), w_query.astype(BF16), sub_keys.astype(BF16))
    return _peer(x1, xn2, peer_u.astype(BF16), peer_v.T.astype(BF16), n1, w1, r2, w2)


def kernel(x, norm1_g, w_in, w_pool, swa_q_norm, swa_k_norm, swa_sinks, conv_w, diff_q_norm, diff_k_norm,
           lam_q1, lam_k1, lam_q2, lam_k2, out_norm_g, w_out, norm2_g, w_query, sub_keys, peer_u, peer_v):
    batch, seq, d = x.shape
    x2d = x.reshape(batch * seq, d)
    depth = norm1_g.shape[0]
    for l in range(depth):
        x2d = _layer(x2d, batch, seq, l, norm1_g[l], w_in[l], w_pool[l], swa_q_norm[l], swa_k_norm[l],
                     swa_sinks[l], conv_w[l], diff_q_norm[l], diff_k_norm[l], lam_q1[l], lam_k1[l],
                     lam_q2[l], lam_k2[l], out_norm_g[l], w_out[l], norm2_g[l], w_query[l], sub_keys[l],
                     peer_u[l], peer_v[l])
    return x2d.reshape(batch, seq, d)
```

```python
import functools

import jax
import jax.numpy as jnp
import numpy as np
from jax import lax
from jax.experimental import pallas as pl
from jax.experimental.pallas import tpu as pltpu

F32 = jnp.float32
BF16 = jnp.bfloat16

D_MODEL = 1024
EPS = 1e-6
HEAD_DIM = 64
GROUP_WIDTH = 256
POOL_WINDOWS = (2, 4, 8, 16)
SWA_Q_HEADS = 4
SWA_GQA = 2
SWA_WINDOW = 128
DIFF_HEADS = 4
DIFF_QK_DIM = 32
IN_PROJ_WIDTH = 2304
PEER_HEADS = 8
PEER_KEYS = 128
PEER_EXPERTS = PEER_KEYS * PEER_KEYS
PEER_HALF = 128
PEER_TOPK = 16
ALIBI_SLOPES = tuple(2.0 ** (-(i + 1)) for i in range(SWA_Q_HEADS + DIFF_HEADS))

LANES = 128
SUBLANES = 8
BF16_ROWS = 16

NEG = -1e30
LOG2_E = 1.4426950408889634
HALO = 16
ROW_BLOCK = 128
DIFF_Q_BLOCK = 256
IN_PROJ_ROWS = 2048
ROUTER_ROWS = 512
ROUTER_LANES = 256
PEER_TOKENS = 512
PEER_EXPERT_CHUNK = 2048
PEER_H_ROWS = 1024
VMEM_LIMIT = 56 * 1024 * 1024

NT_DIMS = (((1,), (1,)), ((), ()))


def _group_sumsq(x, bd):
    x2 = x * x
    hi = x2.astype(BF16)
    lo = (x2 - hi.astype(F32)).astype(BF16)
    return (jnp.dot(hi, bd, preferred_element_type=F32)
            + jnp.dot(lo, bd, preferred_element_type=F32))


def _group_rms_norm(x, bd, group, gain):
    return x * lax.rsqrt(_group_sumsq(x, bd) * (1.0 / group) + EPS) * gain


def _gelu(h):
    return 0.5 * h * (1.0 + lax.erf(h * np.float32(np.sqrt(0.5))))


def _in_proj_kernel(x_ref, g_ref, w_ref, z_ref):
    x = x_ref[...]
    xn = x * lax.rsqrt(jnp.mean(x * x, axis=-1, keepdims=True) + EPS) * g_ref[...]
    z = jnp.dot(xn.astype(BF16), w_ref[...], preferred_element_type=F32)
    z_ref[...] = z.astype(z_ref.dtype)


def _in_proj(x2d, g, w_bf16):
    t = x2d.shape[0]
    tm = min(IN_PROJ_ROWS, t)
    return pl.pallas_call(
        _in_proj_kernel,
        grid=(t // tm,),
        in_specs=[pl.BlockSpec((tm, D_MODEL), lambda i: (i, 0)),
                  pl.BlockSpec((1, D_MODEL), lambda i: (0, 0)),
                  pl.BlockSpec((D_MODEL, IN_PROJ_WIDTH), lambda i: (0, 0))],
        out_specs=pl.BlockSpec((tm, IN_PROJ_WIDTH), lambda i: (i, 0)),
        out_shape=jax.ShapeDtypeStruct((t, IN_PROJ_WIDTH), BF16),
        compiler_params=pltpu.CompilerParams(dimension_semantics=("parallel",),
                                             vmem_limit_bytes=VMEM_LIMIT),
        name="in_proj",
    )(x2d, g, w_bf16)


def _local_mix_kernel(sinks_ref, a_ref, a_halo_ref, bq_ref, bkv_ref, bkv_prev_ref, ch_ref, ch_halo_ref,
                      cb_ref, cc_ref, cc_halo_ref, dq_ref, dk_ref,
                      wpool_ref, bd64_ref, bd32_ref, gq_swa_ref, gk_swa_ref, conv_w_ref, g_out_ref,
                      gq_diff_ref, gk_diff_ref,
                      y_ref, qdn_ref, kdn_ref):
    n = pl.program_id(1)
    not_first = (n > 0).astype(F32)
    bd64 = bd64_ref[...]
    rows = lax.broadcasted_iota(jnp.int32, (ROW_BLOCK, GROUP_WIDTH), 0)
    cols = lax.broadcasted_iota(jnp.int32, (ROW_BLOCK, GROUP_WIDTH), 1)

    p = a_ref[...].astype(F32)
    p_ext = jnp.concatenate([a_halo_ref[...].astype(F32) * not_first, p], axis=0)
    s2 = p_ext + pltpu.roll(p_ext, 1, axis=0)
    s4 = s2 + pltpu.roll(s2, 2, axis=0)
    s8 = s4 + pltpu.roll(s4, 4, axis=0)
    s16 = s8 + pltpu.roll(s8, 8, axis=0)
    wsel = jnp.where(cols < 64, 2, jnp.where(cols < 128, 4, jnp.where(cols < 192, 8, 16)))
    ssel = jnp.where(cols < 64, s2[HALO:], jnp.where(cols < 128, s4[HALO:],
                                                      jnp.where(cols < 192, s8[HALO:], s16[HALO:])))
    cnt = jnp.minimum(n * ROW_BLOCK + rows + 1, wsel).astype(F32)
    d = ssel / cnt - p
    y_a = jnp.dot(d.astype(BF16), wpool_ref[...], preferred_element_type=F32)

    q = _group_rms_norm(bq_ref[...].astype(F32), bd64, HEAD_DIM, gq_swa_ref[...]) * (HEAD_DIM ** -0.5)
    qn = q.astype(BF16)
    bd64k = bd64[:128, :128]
    kv_cur = bkv_ref[...]
    kv_prev = bkv_prev_ref[...]
    k_cur = _group_rms_norm(kv_cur[:, :128].astype(F32), bd64k, HEAD_DIM, gk_swa_ref[...])
    k_prev = _group_rms_norm(kv_prev[:, :128].astype(F32), bd64k, HEAD_DIM, gk_swa_ref[...])
    kn = jnp.concatenate([k_prev, k_cur], axis=0).astype(BF16)
    v = jnp.concatenate([kv_prev[:, 128:], kv_cur[:, 128:]], axis=0)
    qi = lax.broadcasted_iota(jnp.int32, (ROW_BLOCK, 2 * ROW_BLOCK), 0)
    kj = lax.broadcasted_iota(jnp.int32, (ROW_BLOCK, 2 * ROW_BLOCK), 1)
    dist = qi + ROW_BLOCK - kj
    valid = (dist >= 0) & (dist < SWA_WINDOW) & ((n - 1) * ROW_BLOCK + kj >= 0)
    distf = dist.astype(F32)
    heads = []
    for h in range(SWA_Q_HEADS):
        hk = h // SWA_GQA
        sc = lax.dot_general(qn[:, h * 64:(h + 1) * 64], kn[:, hk * 64:(hk + 1) * 64], NT_DIMS,
                             preferred_element_type=F32)
        sc = jnp.where(valid, sc - ALIBI_SLOPES[h] * distf, NEG)
        sink = sinks_ref[h]
        mx = jnp.maximum(jnp.max(sc, axis=-1, keepdims=True), sink)
        e = jnp.exp(sc - mx)
        den = jnp.sum(e, axis=-1, keepdims=True) + jnp.exp(sink - mx)
        o = jnp.dot(e.astype(BF16), v[:, hk * 64:(hk + 1) * 64], preferred_element_type=F32)
        heads.append(o / den)
    y_b = jnp.concatenate(heads, axis=-1)

    zc = cc_ref[...].astype(F32) * ch_ref[...].astype(F32)
    zc_halo = cc_halo_ref[...].astype(F32) * ch_halo_ref[...].astype(F32) * not_first
    zc_ext = jnp.concatenate([zc_halo, zc], axis=0)
    cw = conv_w_ref[...]
    conv = (cw[0:1] * pltpu.roll(zc_ext, 2, axis=0)[HALO:] + cw[1:2] * pltpu.roll(zc_ext, 1, axis=0)[HALO:]
            + cw[2:3] * zc)
    y_c = cb_ref[...].astype(F32) * conv

    g_out = g_out_ref[...]
    for gi, y in enumerate((y_a, y_b, y_c)):
        lo, hi = gi * GROUP_WIDTH, (gi + 1) * GROUP_WIDTH
        y_ref[:, lo:hi] = _group_rms_norm(y, bd64, HEAD_DIM, g_out[:, lo:hi]).astype(y_ref.dtype)

    bd32 = bd32_ref[...]
    qd = (_group_rms_norm(dq_ref[...].astype(F32), bd32, DIFF_QK_DIM, gq_diff_ref[...])
          * (DIFF_QK_DIM ** -0.5 * LOG2_E))
    kd = _group_rms_norm(dk_ref[...].astype(F32), bd32, DIFF_QK_DIM, gk_diff_ref[...])
    qdn_ref[...] = qd.astype(qdn_ref.dtype)
    kdn_ref[...] = kd.astype(kdn_ref.dtype)


def _local_mix(z, batch, seq, sinks, wpool_bd, bd64, bd32, gq_swa, gk_swa, conv_w, g_out, gq_diff, gk_diff):
    t = batch * seq
    nb = seq // ROW_BLOCK
    halo_per_block = ROW_BLOCK // HALO

    def cur(col):
        return pl.BlockSpec((ROW_BLOCK, GROUP_WIDTH), lambda b, n: (b * nb + n, col))

    def prev(col):
        return pl.BlockSpec((ROW_BLOCK, GROUP_WIDTH), lambda b, n: (b * nb + jnp.maximum(n - 1, 0), col))

    def halo(col):
        return pl.BlockSpec(
            (HALO, GROUP_WIDTH),
            lambda b, n: (jnp.maximum((b * nb + n) * halo_per_block - 1, 0), col))

    def whole(arr):
        return pl.BlockSpec(arr.shape, lambda b, n: (0,) * arr.ndim)

    params = (wpool_bd, bd64, bd32, gq_swa, gk_swa, conv_w, g_out, gq_diff, gk_diff)
    in_specs = ([pl.BlockSpec(memory_space=pltpu.SMEM),
                 cur(0), halo(0), cur(1), cur(2), prev(2), cur(3), halo(3), cur(4), cur(5), halo(5),
                 cur(6), cur(7)] + [whole(a) for a in params])
    out_specs = [pl.BlockSpec((ROW_BLOCK, 3 * GROUP_WIDTH), lambda b, n: (b * nb + n, 0)),
                 pl.BlockSpec((ROW_BLOCK, GROUP_WIDTH), lambda b, n: (b * nb + n, 0)),
                 pl.BlockSpec((ROW_BLOCK, GROUP_WIDTH), lambda b, n: (b * nb + n, 0))]
    out_shape = [jax.ShapeDtypeStruct((t, 3 * GROUP_WIDTH), BF16),
                 jax.ShapeDtypeStruct((t, GROUP_WIDTH), BF16),
                 jax.ShapeDtypeStruct((t, GROUP_WIDTH), BF16)]
    return pl.pallas_call(
        _local_mix_kernel,
        grid=(batch, nb),
        in_specs=in_specs,
        out_specs=out_specs,
        out_shape=out_shape,
        compiler_params=pltpu.CompilerParams(dimension_semantics=("parallel", "parallel"),
                                             vmem_limit_bytes=VMEM_LIMIT),
        name="local_mix",
    )(sinks, *([z] * 12), *params)


def _diff_attn_kernel(q_ref, k_ref, v_ref, lamv_ref, bd64_ref, g_ref, y_ref,
                      qs_ref, vt_ref, m_ref, den_ref, acc_ref, *, one_minus_lam_init, lam_init):
    qi = pl.program_id(1)
    qb = DIFF_Q_BLOCK
    lamv = lamv_ref[...]
    lam = (jnp.exp(jnp.sum(lamv[0:1] * lamv[1:2], axis=-1, keepdims=True))
           - jnp.exp(jnp.sum(lamv[2:3] * lamv[3:4], axis=-1, keepdims=True)) + lam_init)
    n_maps = 2 * DIFF_HEADS

    @pl.when(qi == 0)
    def _():
        for c in range(vt_ref.shape[0]):
            vt_ref[c] = v_ref[c * qb:(c + 1) * qb, :].astype(F32).T.astype(BF16)

    q = q_ref[...]
    lane_group = lax.broadcasted_iota(jnp.int32, (qb, GROUP_WIDTH), 1) // DIFF_QK_DIM
    for g in range(n_maps):
        qs_ref[g * qb:(g + 1) * qb, :] = jnp.where(lane_group == g, q, jnp.zeros_like(q))
    m_ref[...] = jnp.full(m_ref.shape, NEG, F32)
    den_ref[...] = jnp.zeros(den_ref.shape, F32)
    acc_ref[...] = jnp.zeros(acc_ref.shape, F32)
    rel = (lax.broadcasted_iota(jnp.int32, (qb, qb), 1) - lax.broadcasted_iota(jnp.int32, (qb, qb), 0))
    relf = rel.astype(F32)
    slopes = [ALIBI_SLOPES[SWA_Q_HEADS + h] * LOG2_E for h in range(DIFF_HEADS)]
    bias = [slopes[h] * relf for h in range(DIFF_HEADS)]

    def kv_step(c, diagonal):
        start = pl.multiple_of(c * qb, qb)
        sc_all = lax.dot_general(k_ref[pl.ds(start, qb), :], qs_ref[...], NT_DIMS,
                                 preferred_element_type=F32)
        off = ((qi - c) * qb).astype(F32)
        for g in range(n_maps):
            cols = slice(g * qb, (g + 1) * qb)
            sc = sc_all[:, cols] - bias[g // 2]
            if diagonal:
                sc = jnp.where(rel >= 0, sc, NEG)
            shift = slopes[g // 2] * off
            m_old = m_ref[:, cols]
            m_new = jnp.maximum(m_old, jnp.max(sc, axis=0, keepdims=True) - shift)
            alpha = jnp.exp2(m_old - m_new)
            e = jnp.exp2(sc - (m_new + shift))
            den_ref[:, cols] = alpha * den_ref[:, cols] + jnp.sum(e, axis=0, keepdims=True)
            m_ref[:, cols] = m_new
            rows = slice((g // 2) * HEAD_DIM, (g // 2 + 1) * HEAD_DIM)
            pv = jnp.dot(vt_ref[c, rows, :], e.astype(BF16), preferred_element_type=F32)
            acc_ref[:, cols] = alpha * acc_ref[:, cols] + pv

    def off_diagonal_pair(p, carry):
        kv_step(2 * p, False)
        kv_step(2 * p + 1, False)
        return carry

    lax.fori_loop(0, qi // 2, off_diagonal_pair, 0)

    @pl.when(qi % 2 == 1)
    def _():
        kv_step(qi - 1, False)

    kv_step(qi, True)

    heads = []
    for h in range(DIFF_HEADS):
        c1 = slice(2 * h * qb, (2 * h + 1) * qb)
        c2 = slice((2 * h + 1) * qb, (2 * h + 2) * qb)
        heads.append(acc_ref[:, c1] / den_ref[:, c1] - lam * (acc_ref[:, c2] / den_ref[:, c2]))
    o = jnp.concatenate(heads, axis=0).T
    y = _group_rms_norm(o, bd64_ref[...], HEAD_DIM, g_ref[...]) * one_minus_lam_init
    y_ref[...] = y.astype(y_ref.dtype)


def _diff_attn(qdn, kdn, z, batch, seq, lamv, bd64, g_d, lam_init):
    t = batch * seq
    nq = seq // DIFF_Q_BLOCK
    v_col = 8
    stacked = 2 * DIFF_HEADS * DIFF_Q_BLOCK
    kernel = functools.partial(_diff_attn_kernel, one_minus_lam_init=1.0 - lam_init, lam_init=lam_init)
    return pl.pallas_call(
        kernel,
        grid=(batch, nq),
        in_specs=[pl.BlockSpec((DIFF_Q_BLOCK, GROUP_WIDTH), lambda b, i: (b * nq + i, 0)),
                  pl.BlockSpec((seq, GROUP_WIDTH), lambda b, i: (b, 0)),
                  pl.BlockSpec((seq, GROUP_WIDTH), lambda b, i: (b, v_col)),
                  pl.BlockSpec(lamv.shape, lambda b, i: (0, 0)),
                  pl.BlockSpec(bd64.shape, lambda b, i: (0, 0)),
                  pl.BlockSpec(g_d.shape, lambda b, i: (0, 0))],
        out_specs=pl.BlockSpec((DIFF_Q_BLOCK, GROUP_WIDTH), lambda b, i: (b * nq + i, 0)),
        out_shape=jax.ShapeDtypeStruct((t, GROUP_WIDTH), BF16),
        scratch_shapes=[pltpu.VMEM((stacked, GROUP_WIDTH), BF16),
                        pltpu.VMEM((seq // DIFF_Q_BLOCK, GROUP_WIDTH, DIFF_Q_BLOCK), BF16),
                        pltpu.VMEM((1, stacked), F32),
                        pltpu.VMEM((1, stacked), F32),
                        pltpu.VMEM((HEAD_DIM, stacked), F32)],
        compiler_params=pltpu.CompilerParams(dimension_semantics=("parallel", "arbitrary"),
                                             vmem_limit_bytes=VMEM_LIMIT),
        name="diff_attn",
    )(qdn, kdn, z, lamv, bd64, g_d)


def _odd_even_merge_sort_pairs(n):
    pairs = []

    def merge(lo, length, r):
        step = 2 * r
        if step < length:
            merge(lo, length, step)
            merge(lo + r, length, step)
            pairs.extend((i, i + r) for i in range(lo + r, lo + length - r, step))
        else:
            pairs.append((lo, lo + r))

    def sort(lo, length):
        if length > 1:
            sort(lo, length // 2)
            sort(lo + length // 2, length // 2)
            merge(lo, length, 1)

    sort(0, n)
    return pairs


_SORT16 = _odd_even_merge_sort_pairs(PEER_TOPK)


def _top16_sorted(slabs):
    n = PEER_TOPK
    assert n // 2 <= len(slabs) <= n
    a = list(slabs) + [None] * (n - len(slabs))

    def exchange(i, j):
        if a[j] is None:
            return
        if a[i] is None:
            a[i], a[j] = a[j], None
        else:
            a[i], a[j] = jnp.maximum(a[i], a[j]), jnp.minimum(a[i], a[j])

    for i, j in _SORT16:
        exchange(i, j)
    for shift in (SUBLANES // 2, SUBLANES // 4, SUBLANES // 8):
        b = [None if x is None else pltpu.roll(x, shift, axis=0) for x in a]
        for j in range(n):
            other = b[n - 1 - j]
            if a[j] is None:
                a[j] = other
            elif other is not None:
                a[j] = jnp.maximum(a[j], other)
        d = n // 2
        while d:
            for i in range(n):
                if not i & d:
                    exchange(i, i + d)
            d //= 2
    return a


def _sublane_pick(arrays):
    row = lax.broadcasted_iota(jnp.int32, arrays[0].shape, 0)
    out = arrays[0]
    for s in range(1, SUBLANES):
        out = jnp.where(row == s, arrays[s], out)
    return out


def _dup_bf16_words(x):
    bits = pltpu.bitcast(x.astype(BF16).astype(F32), jnp.uint32)
    return bits | (bits >> 16)


def _out_router_kernel(yabc_ref, yd_ref, x_ref, wo1_ref, wo2_ref, g2_ref, wq_ref, keys_ref,
                       x1_ref, xn2_ref, n1_ref, w1_ref, r2_ref, w2_ref):
    acc = (jnp.dot(yabc_ref[...], wo1_ref[...], preferred_element_type=F32)
           + jnp.dot(yd_ref[...], wo2_ref[...], preferred_element_type=F32))
    x1 = x_ref[...] + acc
    x1_ref[...] = x1
    xn2 = (x1 * lax.rsqrt(jnp.mean(x1 * x1, axis=-1, keepdims=True) + EPS) * g2_ref[...]).astype(BF16)
    xn2_ref[...] = pltpu.bitcast(xn2, jnp.uint32)
    q = jnp.dot(xn2, wq_ref[...], preferred_element_type=F32)
    tm = q.shape[0]
    for h in range(PEER_HEADS):
        scores = []
        for c in range(2):
            lo = (2 * h + c) * PEER_HALF
            qhc = q[:, lo:lo + PEER_HALF].astype(BF16)
            scores.append(lax.dot_general(keys_ref[c], qhc, NT_DIMS, preferred_element_type=F32))
        for p in range(tm // ROUTER_LANES):
            ls = slice(p * ROUTER_LANES, (p + 1) * ROUTER_LANES)
            _route_head(scores[0][:, ls], scores[1][:, ls],
                        n1_ref.at[h, :, ls], w1_ref.at[h, :, ls], r2_ref.at[h, :, ls], w2_ref.at[h, :, ls])


def _route_head(s1, s2, n1_ref, w1_ref, r2_ref, w2_ref):
    lanes = s1.shape[1]
    sl = SUBLANES
    row8 = lax.broadcasted_iota(jnp.int32, (sl, lanes), 0)
    slabs = PEER_KEYS // sl
    t1 = _top16_sorted([s1[sl * j:sl * j + sl] for j in range(slabs)])
    t2 = _top16_sorted([s2[sl * j:sl * j + sl] for j in range(slabs)])
    v1_lo, v1_hi, v2_hi = _sublane_pick(t1[:sl]), _sublane_pick(t1[sl:]), _sublane_pick(t2[sl:])
    groups = [v1_lo + t2[0], v1_hi + t2[0]]
    for b in range(1, sl):
        groups.append(jnp.where(row8 < PEER_TOPK // (b + 1), v1_lo + t2[b], NEG))
    groups.append(t1[0] + v2_hi)
    best = _top16_sorted(groups)
    tau = best[PEER_TOPK - 1]
    zsum = jnp.ones_like(tau)
    for r in range(1, PEER_TOPK):
        zsum = zsum + jnp.exp(best[r] - best[0])
    inv_z = 1.0 / zsum
    dense_b = 4
    top_rows = PEER_TOPK // (dense_b + 1)
    n1_top = []
    for a in range(top_rows):
        count = jnp.zeros_like(tau)
        for b in range(dense_b, PEER_TOPK // (a + 1)):
            count = count + jnp.where(t1[a] + t2[b] >= tau, 1.0, 0.0)
        n1_top.append(count)
    for j in range(PEER_KEYS // BF16_ROWS):
        words = slice(sl * j, sl * j + sl)
        rank2, w2 = [], []
        for half in range(2):
            ks = slice(BF16_ROWS * j + sl * half, BF16_ROWS * j + sl * half + sl)
            s1k, s2k = s1[ks], s2[ks]
            n1 = jnp.zeros((sl, lanes), F32)
            for a in reversed(range(top_rows)):
                n1 = jnp.where(s1k >= t1[a], n1_top[a], n1)
            rk = jnp.full((sl, lanes), float(PEER_TOPK), F32)
            for b in reversed(range(PEER_TOPK)):
                if b < dense_b:
                    n1 = n1 + jnp.where(s1k + t2[b] >= tau, 1.0, 0.0)
                rk = jnp.where(s2k >= t2[b], float(b), rk)
            n1_ref[ks, :] = _dup_bf16_words(n1)
            w1_ref[ks, :] = _dup_bf16_words(jnp.exp(s1k - t1[0]) * inv_z)
            rank2.append(rk)
            w2.append(jnp.exp(s2k - t2[0]))
        r2_ref[words, :] = pltpu.bitcast(jnp.concatenate(rank2, axis=0).astype(BF16), jnp.uint32)
        w2_ref[words, :] = pltpu.bitcast(jnp.concatenate(w2, axis=0).astype(BF16), jnp.uint32)


def _out_router(yabc, yd, x2d, wo1, wo2, g2, wq, keys):
    t = x2d.shape[0]
    tm = min(ROUTER_ROWS, t)

    def whole(arr):
        return pl.BlockSpec(arr.shape, lambda i: (0,) * arr.ndim)

    row_spec = pl.BlockSpec((PEER_HEADS, PEER_KEYS, tm), lambda i: (0, 0, i))
    row_shape = jax.ShapeDtypeStruct((PEER_HEADS, PEER_KEYS, t), jnp.uint32)
    tile_spec = pl.BlockSpec((PEER_HEADS, PEER_KEYS // 2, tm), lambda i: (0, 0, i))
    tile_shape = jax.ShapeDtypeStruct((PEER_HEADS, PEER_KEYS // 2, t), jnp.uint32)
    return pl.pallas_call(
        _out_router_kernel,
        grid=(t // tm,),
        in_specs=[pl.BlockSpec((tm, 3 * GROUP_WIDTH), lambda i: (i, 0)),
                  pl.BlockSpec((tm, GROUP_WIDTH), lambda i: (i, 0)),
                  pl.BlockSpec((tm, D_MODEL), lambda i: (i, 0)),
                  whole(wo1), whole(wo2), whole(g2), whole(wq), whole(keys)],
        out_specs=[pl.BlockSpec((tm, D_MODEL), lambda i: (i, 0)),
                   pl.BlockSpec((tm // 2, D_MODEL), lambda i: (i, 0)),
                   row_spec, row_spec, tile_spec, tile_spec],
        out_shape=[jax.ShapeDtypeStruct((t, D_MODEL), F32),
                   jax.ShapeDtypeStruct((t // 2, D_MODEL), jnp.uint32),
                   row_shape, row_shape, tile_shape, tile_shape],
        compiler_params=pltpu.CompilerParams(dimension_semantics=("parallel",),
                                             vmem_limit_bytes=VMEM_LIMIT),
        name="out_router",
    )(yabc, yd, x2d, wo1, wo2, g2, wq, keys)


def _gate_activations(h_ref, a_ref, n1_ref, w1_ref, r2_ref, w2_ref, first_row, rows):
    tb = h_ref.shape[1]
    zero = jnp.zeros((), BF16)
    half = PEER_KEYS // 2
    for lc in range(tb // LANES):
        ls = slice(lc * LANES, (lc + 1) * LANES)
        n1_rows = [n1_ref[hd, pl.ds(first_row, rows), ls] for hd in range(PEER_HEADS)]
        w1_rows = [w1_ref[hd, pl.ds(first_row, rows), ls] for hd in range(PEER_HEADS)]
        for r in range(rows):
            w = None
            for hd in range(PEER_HEADS):
                n1b = pltpu.bitcast(jnp.broadcast_to(n1_rows[hd][r:r + 1], (half, LANES)), BF16)
                w1b = pltpu.bitcast(jnp.broadcast_to(w1_rows[hd][r:r + 1], (half, LANES)), BF16)
                rank2 = pltpu.bitcast(r2_ref[hd, :, ls], BF16)
                w2 = pltpu.bitcast(w2_ref[hd, :, ls], BF16)
                term = jnp.where(rank2 < n1b, w2 * w1b, zero)
                w = term if w is None else w + term
            g = _gelu(h_ref[r * PEER_KEYS:(r + 1) * PEER_KEYS, ls]).astype(BF16)
            a_ref[r * half:(r + 1) * half, ls] = pltpu.bitcast(w * g, jnp.uint32)


def _peer_kernel(x_ref, xn_ref, u_ref, vt_ref, n1_ref, w1_ref, r2_ref, w2_ref, o_ref,
                 acc_ref, h_ref, a_ref, *, rows):
    j = pl.program_id(1)

    @pl.when(j == 0)
    def _():
        acc_ref[...] = jnp.zeros_like(acc_ref)

    xn = pltpu.bitcast(xn_ref[...], BF16)
    for m in range(0, u_ref.shape[0], PEER_H_ROWS):
        h_ref[m:m + PEER_H_ROWS, :] = lax.dot_general(u_ref[m:m + PEER_H_ROWS, :], xn, NT_DIMS,
                                                      preferred_element_type=F32)
    first_row = pl.multiple_of(j * rows, rows)
    _gate_activations(h_ref, a_ref, n1_ref, w1_ref, r2_ref, w2_ref, first_row, rows)
    acc_ref[...] += jnp.dot(vt_ref[...], pltpu.bitcast(a_ref[...], BF16), preferred_element_type=F32)

    @pl.when(j == pl.num_programs(1) - 1)
    def _():
        o_ref[...] = x_ref[...] + acc_ref[...].T


def _peer(x1, xn2, u_bf16, vt_bf16, n1, w1, r2, w2):
    t = x1.shape[0]
    tb = min(PEER_TOKENS, t)
    ec = PEER_EXPERT_CHUNK
    rows = ec // PEER_KEYS
    assert rows % SUBLANES == 0 and tb % LANES == 0 and t % tb == 0 and PEER_EXPERTS % ec == 0
    row_spec = pl.BlockSpec((PEER_HEADS, PEER_KEYS, tb), lambda i, j: (0, 0, i))
    tile_spec = pl.BlockSpec((PEER_HEADS, PEER_KEYS // 2, tb), lambda i, j: (0, 0, i))
    kernel = functools.partial(_peer_kernel, rows=rows)
    return pl.pallas_call(
        kernel,
        grid=(t // tb, PEER_EXPERTS // ec),
        in_specs=[pl.BlockSpec((tb, D_MODEL), lambda i, j: (i, 0)),
                  pl.BlockSpec((tb // 2, D_MODEL), lambda i, j: (i, 0)),
                  pl.BlockSpec((ec, D_MODEL), lambda i, j: (j, 0)),
                  pl.BlockSpec((D_MODEL, ec), lambda i, j: (0, j)),
                  row_spec, row_spec, tile_spec, tile_spec],
        out_specs=pl.BlockSpec((tb, D_MODEL), lambda i, j: (i, 0)),
        out_shape=jax.ShapeDtypeStruct((t, D_MODEL), F32),
        scratch_shapes=[pltpu.VMEM((D_MODEL, tb), F32),
                        pltpu.VMEM((ec, tb), F32),
                        pltpu.VMEM((ec // 2, tb), jnp.uint32)],
        compiler_params=pltpu.CompilerParams(dimension_semantics=("parallel", "arbitrary"),
                                             vmem_limit_bytes=VMEM_LIMIT),
        name="peer",
    )(x1, xn2, u_bf16, vt_bf16, n1, w1, r2, w2)


def _block_diag_ones(width, group):
    idx = np.arange(width) // group
    return jnp.asarray((idx[:, None] == idx[None, :]).astype(np.float32), BF16)


def _layer(x2d, batch, seq, layer_idx, norm1_g, w_in, w_pool, swa_q_norm, swa_k_norm, swa_sinks, conv_w,
           diff_q_norm, diff_k_norm, lam_q1, lam_k1, lam_q2, lam_k2, out_norm_g, w_out, norm2_g, w_query,
           sub_keys, peer_u, peer_v):
    lam_init = 0.8 - 0.6 * float(np.exp(-0.3 * layer_idx))
    bd64 = _block_diag_ones(GROUP_WIDTH, HEAD_DIM)
    bd32 = _block_diag_ones(GROUP_WIDTH, DIFF_QK_DIM)
    wpool_bd = jax.scipy.linalg.block_diag(*[w_pool[g] for g in range(len(POOL_WINDOWS))]).astype(BF16)

    z = _in_proj(x2d, norm1_g.reshape(1, D_MODEL), w_in.astype(BF16))
    yabc, qdn, kdn = _local_mix(
        z, batch, seq, swa_sinks.astype(F32), wpool_bd, bd64, bd32,
        jnp.tile(swa_q_norm, SWA_Q_HEADS).reshape(1, GROUP_WIDTH),
        jnp.tile(swa_k_norm, 2).reshape(1, 128),
        conv_w, out_norm_g[:3 * GROUP_WIDTH].reshape(1, 3 * GROUP_WIDTH),
        jnp.tile(diff_q_norm, 2 * DIFF_HEADS).reshape(1, GROUP_WIDTH),
        jnp.tile(diff_k_norm, 2 * DIFF_HEADS).reshape(1, GROUP_WIDTH))
    lamv = jnp.stack([lam_q1, lam_k1, lam_q2, lam_k2]).astype(F32)
    yd = _diff_attn(qdn, kdn, z, batch, seq, lamv, bd64,
                    out_norm_g[3 * GROUP_WIDTH:].reshape(1, GROUP_WIDTH), lam_init)
    w_out_bf16 = w_out.astype(BF16)
    x1, xn2, n1, w1, r2, w2 = _out_router(
        yabc, yd, x2d, w_out_bf16[:3 * GROUP_WIDTH], w_out_bf16[3 * GROUP_WIDTH:],
        norm2_g.reshape(1, D_MODEL), w_query.astype(BF16), sub_keys.astype(BF16))
    return _peer(x1, xn2, peer_u.astype(BF16), peer_v.astype(BF16).T, n1, w1, r2, w2)


def kernel(x, norm1_g, w_in, w_pool, swa_q_norm, swa_k_norm, swa_sinks, conv_w, diff_q_norm, diff_k_norm,
           lam_q1, lam_k1, lam_q2, lam_k2, out_norm_g, w_out, norm2_g, w_query, sub_keys, peer_u, peer_v):
    batch, seq, d = x.shape
    x2d = x.reshape(batch * seq, d)
    depth = norm1_g.shape[0]
    for l in range(depth):
        x2d = _layer(x2d, batch, seq, l, norm1_g[l], w_in[l], w_pool[l], swa_q_norm[l], swa_k_norm[l],
                     swa_sinks[l], conv_w[l], diff_q_norm[l], diff_k_norm[l], lam_q1[l], lam_k1[l],
                     lam_q2[l], lam_k2[l], out_norm_g[l], w_out[l], norm2_g[l], w_query[l], sub_keys[l],
                     peer_u[l], peer_v[l])
    return x2d.reshape(batch, seq, d)
```

```python
import functools

import jax
import jax.numpy as jnp
import numpy as np
from jax import lax
from jax.experimental import pallas as pl
from jax.experimental.pallas import tpu as pltpu

F32 = jnp.float32
BF16 = jnp.bfloat16

D_MODEL = 1024
EPS = 1e-6
HEAD_DIM = 64
GROUP_WIDTH = 256
POOL_WINDOWS = (2, 4, 8, 16)
SWA_Q_HEADS = 4
SWA_GQA = 2
SWA_WINDOW = 128
DIFF_HEADS = 4
DIFF_QK_DIM = 32
IN_PROJ_WIDTH = 2304
PEER_HEADS = 8
PEER_KEYS = 128
PEER_EXPERTS = PEER_KEYS * PEER_KEYS
PEER_HALF = 128
PEER_TOPK = 16
ALIBI_SLOPES = tuple(2.0 ** (-(i + 1)) for i in range(SWA_Q_HEADS + DIFF_HEADS))

LANES = 128
SUBLANES = 8
BF16_ROWS = 16

NEG = -1e30
LOG2_E = 1.4426950408889634
HALO = 16
ROW_BLOCK = 128
DIFF_Q_BLOCK = 256
IN_PROJ_ROWS = 2048
ROUTER_ROWS = 512
ROUTER_LANES = 128
PEER_TOKENS = 512
PEER_EXPERT_CHUNK = 2048
PEER_H_ROWS = 1024
VMEM_LIMIT = 56 * 1024 * 1024

NT_DIMS = (((1,), (1,)), ((), ()))


def _group_sumsq(x, bd):
    x2 = x * x
    hi = x2.astype(BF16)
    lo = (x2 - hi.astype(F32)).astype(BF16)
    return (jnp.dot(hi, bd, preferred_element_type=F32)
            + jnp.dot(lo, bd, preferred_element_type=F32))


def _group_rms_norm(x, bd, group, gain):
    return x * lax.rsqrt(_group_sumsq(x, bd) * (1.0 / group) + EPS) * gain


def _gelu(h):
    return 0.5 * h * (1.0 + lax.erf(h * np.float32(np.sqrt(0.5))))


def _in_proj_kernel(x_ref, g_ref, w_ref, z_ref):
    x = x_ref[...]
    xn = x * lax.rsqrt(jnp.mean(x * x, axis=-1, keepdims=True) + EPS) * g_ref[...]
    z = jnp.dot(xn.astype(BF16), w_ref[...], preferred_element_type=F32)
    z_ref[...] = z.astype(z_ref.dtype)


def _in_proj(x2d, g, w_bf16):
    t = x2d.shape[0]
    tm = min(IN_PROJ_ROWS, t)
    return pl.pallas_call(
        _in_proj_kernel,
        grid=(t // tm,),
        in_specs=[pl.BlockSpec((tm, D_MODEL), lambda i: (i, 0)),
                  pl.BlockSpec((1, D_MODEL), lambda i: (0, 0)),
                  pl.BlockSpec((D_MODEL, IN_PROJ_WIDTH), lambda i: (0, 0))],
        out_specs=pl.BlockSpec((tm, IN_PROJ_WIDTH), lambda i: (i, 0)),
        out_shape=jax.ShapeDtypeStruct((t, IN_PROJ_WIDTH), BF16),
        compiler_params=pltpu.CompilerParams(dimension_semantics=("parallel",),
                                             vmem_limit_bytes=VMEM_LIMIT),
        name="in_proj",
    )(x2d, g, w_bf16)


def _local_mix_kernel(sinks_ref, a_ref, a_halo_ref, bq_ref, bkv_ref, bkv_prev_ref, ch_ref, ch_halo_ref,
                      cb_ref, cc_ref, cc_halo_ref, dq_ref, dk_ref,
                      wpool_ref, bd64_ref, bd32_ref, gq_swa_ref, gk_swa_ref, conv_w_ref, g_out_ref,
                      gq_diff_ref, gk_diff_ref,
                      y_ref, qdn_ref, kdn_ref):
    n = pl.program_id(1)
    not_first = (n > 0).astype(F32)
    bd64 = bd64_ref[...]
    rows = lax.broadcasted_iota(jnp.int32, (ROW_BLOCK, GROUP_WIDTH), 0)
    cols = lax.broadcasted_iota(jnp.int32, (ROW_BLOCK, GROUP_WIDTH), 1)

    p = a_ref[...].astype(F32)
    p_ext = jnp.concatenate([a_halo_ref[...].astype(F32) * not_first, p], axis=0)
    s2 = p_ext + pltpu.roll(p_ext, 1, axis=0)
    s4 = s2 + pltpu.roll(s2, 2, axis=0)
    s8 = s4 + pltpu.roll(s4, 4, axis=0)
    s16 = s8 + pltpu.roll(s8, 8, axis=0)
    wsel = jnp.where(cols < 64, 2, jnp.where(cols < 128, 4, jnp.where(cols < 192, 8, 16)))
    ssel = jnp.where(cols < 64, s2[HALO:], jnp.where(cols < 128, s4[HALO:],
                                                      jnp.where(cols < 192, s8[HALO:], s16[HALO:])))
    cnt = jnp.minimum(n * ROW_BLOCK + rows + 1, wsel).astype(F32)
    d = ssel / cnt - p
    y_a = jnp.dot(d.astype(BF16), wpool_ref[...], preferred_element_type=F32)

    q = _group_rms_norm(bq_ref[...].astype(F32), bd64, HEAD_DIM, gq_swa_ref[...]) * (HEAD_DIM ** -0.5)
    qn = q.astype(BF16)
    bd64k = bd64[:128, :128]
    kv_cur = bkv_ref[...]
    kv_prev = bkv_prev_ref[...]
    k_cur = _group_rms_norm(kv_cur[:, :128].astype(F32), bd64k, HEAD_DIM, gk_swa_ref[...])
    k_prev = _group_rms_norm(kv_prev[:, :128].astype(F32), bd64k, HEAD_DIM, gk_swa_ref[...])
    kn = jnp.concatenate([k_prev, k_cur], axis=0).astype(BF16)
    v = jnp.concatenate([kv_prev[:, 128:], kv_cur[:, 128:]], axis=0)
    qi = lax.broadcasted_iota(jnp.int32, (ROW_BLOCK, 2 * ROW_BLOCK), 0)
    kj = lax.broadcasted_iota(jnp.int32, (ROW_BLOCK, 2 * ROW_BLOCK), 1)
    dist = qi + ROW_BLOCK - kj
    valid = (dist >= 0) & (dist < SWA_WINDOW) & ((n - 1) * ROW_BLOCK + kj >= 0)
    distf = dist.astype(F32)
    heads = []
    for h in range(SWA_Q_HEADS):
        hk = h // SWA_GQA
        sc = lax.dot_general(qn[:, h * 64:(h + 1) * 64], kn[:, hk * 64:(hk + 1) * 64], NT_DIMS,
                             preferred_element_type=F32)
        sc = jnp.where(valid, sc - ALIBI_SLOPES[h] * distf, NEG)
        sink = sinks_ref[h]
        mx = jnp.maximum(jnp.max(sc, axis=-1, keepdims=True), sink)
        e = jnp.exp(sc - mx)
        den = jnp.sum(e, axis=-1, keepdims=True) + jnp.exp(sink - mx)
        o = jnp.dot(e.astype(BF16), v[:, hk * 64:(hk + 1) * 64], preferred_element_type=F32)
        heads.append(o / den)
    y_b = jnp.concatenate(heads, axis=-1)

    zc = cc_ref[...].astype(F32) * ch_ref[...].astype(F32)
    zc_halo = cc_halo_ref[...].astype(F32) * ch_halo_ref[...].astype(F32) * not_first
    zc_ext = jnp.concatenate([zc_halo, zc], axis=0)
    cw = conv_w_ref[...]
    conv = (cw[0:1] * pltpu.roll(zc_ext, 2, axis=0)[HALO:] + cw[1:2] * pltpu.roll(zc_ext, 1, axis=0)[HALO:]
            + cw[2:3] * zc)
    y_c = cb_ref[...].astype(F32) * conv

    g_out = g_out_ref[...]
    for gi, y in enumerate((y_a, y_b, y_c)):
        lo, hi = gi * GROUP_WIDTH, (gi + 1) * GROUP_WIDTH
        y_ref[:, lo:hi] = _group_rms_norm(y, bd64, HEAD_DIM, g_out[:, lo:hi]).astype(y_ref.dtype)

    bd32 = bd32_ref[...]
    qd = (_group_rms_norm(dq_ref[...].astype(F32), bd32, DIFF_QK_DIM, gq_diff_ref[...])
          * (DIFF_QK_DIM ** -0.5 * LOG2_E))
    kd = _group_rms_norm(dk_ref[...].astype(F32), bd32, DIFF_QK_DIM, gk_diff_ref[...])
    qdn_ref[...] = qd.astype(qdn_ref.dtype)
    kdn_ref[...] = kd.astype(kdn_ref.dtype)


def _local_mix(z, batch, seq, sinks, wpool_bd, bd64, bd32, gq_swa, gk_swa, conv_w, g_out, gq_diff, gk_diff):
    t = batch * seq
    nb = seq // ROW_BLOCK
    halo_per_block = ROW_BLOCK // HALO

    def cur(col):
        return pl.BlockSpec((ROW_BLOCK, GROUP_WIDTH), lambda b, n: (b * nb + n, col))

    def prev(col):
        return pl.BlockSpec((ROW_BLOCK, GROUP_WIDTH), lambda b, n: (b * nb + jnp.maximum(n - 1, 0), col))

    def halo(col):
        return pl.BlockSpec(
            (HALO, GROUP_WIDTH),
            lambda b, n: (jnp.maximum((b * nb + n) * halo_per_block - 1, 0), col))

    def whole(arr):
        return pl.BlockSpec(arr.shape, lambda b, n: (0,) * arr.ndim)

    params = (wpool_bd, bd64, bd32, gq_swa, gk_swa, conv_w, g_out, gq_diff, gk_diff)
    in_specs = ([pl.BlockSpec(memory_space=pltpu.SMEM),
                 cur(0), halo(0), cur(1), cur(2), prev(2), cur(3), halo(3), cur(4), cur(5), halo(5),
                 cur(6), cur(7)] + [whole(a) for a in params])
    out_specs = [pl.BlockSpec((ROW_BLOCK, 3 * GROUP_WIDTH), lambda b, n: (b * nb + n, 0)),
                 pl.BlockSpec((ROW_BLOCK, GROUP_WIDTH), lambda b, n: (b * nb + n, 0)),
                 pl.BlockSpec((ROW_BLOCK, GROUP_WIDTH), lambda b, n: (b * nb + n, 0))]
    out_shape = [jax.ShapeDtypeStruct((t, 3 * GROUP_WIDTH), BF16),
                 jax.ShapeDtypeStruct((t, GROUP_WIDTH), BF16),
                 jax.ShapeDtypeStruct((t, GROUP_WIDTH), BF16)]
    return pl.pallas_call(
        _local_mix_kernel,
        grid=(batch, nb),
        in_specs=in_specs,
        out_specs=out_specs,
        out_shape=out_shape,
        compiler_params=pltpu.CompilerParams(dimension_semantics=("parallel", "parallel"),
                                             vmem_limit_bytes=VMEM_LIMIT),
        name="local_mix",
    )(sinks, *([z] * 12), *params)


def _diff_attn_kernel(q_ref, k_ref, v_ref, lamv_ref, bd64_ref, g_ref, y_ref,
                      qs_ref, vt_ref, m_ref, den_ref, acc_ref, *, one_minus_lam_init, lam_init):
    qi = pl.program_id(1)
    qb = DIFF_Q_BLOCK
    lamv = lamv_ref[...]
    lam = (jnp.exp(jnp.sum(lamv[0:1] * lamv[1:2], axis=-1, keepdims=True))
           - jnp.exp(jnp.sum(lamv[2:3] * lamv[3:4], axis=-1, keepdims=True)) + lam_init)
    n_maps = 2 * DIFF_HEADS

    @pl.when(qi == 0)
    def _():
        for c in range(vt_ref.shape[0]):
            vt_ref[c] = v_ref[c * qb:(c + 1) * qb, :].astype(F32).T.astype(BF16)

    q = q_ref[...]
    lane_group = lax.broadcasted_iota(jnp.int32, (qb, GROUP_WIDTH), 1) // DIFF_QK_DIM
    for g in range(n_maps):
        qs_ref[g * qb:(g + 1) * qb, :] = jnp.where(lane_group == g, q, jnp.zeros_like(q))
    m_ref[...] = jnp.full(m_ref.shape, NEG, F32)
    den_ref[...] = jnp.zeros(den_ref.shape, F32)
    acc_ref[...] = jnp.zeros(acc_ref.shape, F32)
    rel = (lax.broadcasted_iota(jnp.int32, (qb, qb), 1) - lax.broadcasted_iota(jnp.int32, (qb, qb), 0))
    relf = rel.astype(F32)
    slopes = [ALIBI_SLOPES[SWA_Q_HEADS + h] * LOG2_E for h in range(DIFF_HEADS)]
    bias = [slopes[h] * relf for h in range(DIFF_HEADS)]

    def kv_step(c, diagonal):
        start = pl.multiple_of(c * qb, qb)
        sc_all = lax.dot_general(k_ref[pl.ds(start, qb), :], qs_ref[...], NT_DIMS,
                                 preferred_element_type=F32)
        off = ((qi - c) * qb).astype(F32)
        for g in range(n_maps):
            cols = slice(g * qb, (g + 1) * qb)
            sc = sc_all[:, cols] - bias[g // 2]
            if diagonal:
                sc = jnp.where(rel >= 0, sc, NEG)
            shift = slopes[g // 2] * off
            m_old = m_ref[:, cols]
            m_new = jnp.maximum(m_old, jnp.max(sc, axis=0, keepdims=True) - shift)
            alpha = jnp.exp2(m_old - m_new)
            e = jnp.exp2(sc - (m_new + shift))
            den_ref[:, cols] = alpha * den_ref[:, cols] + jnp.sum(e, axis=0, keepdims=True)
            m_ref[:, cols] = m_new
            rows = slice((g // 2) * HEAD_DIM, (g // 2 + 1) * HEAD_DIM)
            pv = jnp.dot(vt_ref[c, rows, :], e.astype(BF16), preferred_element_type=F32)
            acc_ref[:, cols] = alpha * acc_ref[:, cols] + pv

    def off_diagonal_pair(p, carry):
        kv_step(2 * p, False)
        kv_step(2 * p + 1, False)
        return carry

    lax.fori_loop(0, qi // 2, off_diagonal_pair, 0)

    @pl.when(qi % 2 == 1)
    def _():
        kv_step(qi - 1, False)

    kv_step(qi, True)

    heads = []
    for h in range(DIFF_HEADS):
        c1 = slice(2 * h * qb, (2 * h + 1) * qb)
        c2 = slice((2 * h + 1) * qb, (2 * h + 2) * qb)
        heads.append(acc_ref[:, c1] / den_ref[:, c1] - lam * (acc_ref[:, c2] / den_ref[:, c2]))
    o = jnp.concatenate(heads, axis=0).T
    y = _group_rms_norm(o, bd64_ref[...], HEAD_DIM, g_ref[...]) * one_minus_lam_init
    y_ref[...] = y.astype(y_ref.dtype)


def _diff_attn(qdn, kdn, z, batch, seq, lamv, bd64, g_d, lam_init):
    t = batch * seq
    nq = seq // DIFF_Q_BLOCK
    v_col = 8
    stacked = 2 * DIFF_HEADS * DIFF_Q_BLOCK
    kernel = functools.partial(_diff_attn_kernel, one_minus_lam_init=1.0 - lam_init, lam_init=lam_init)
    return pl.pallas_call(
        kernel,
        grid=(batch, nq),
        in_specs=[pl.BlockSpec((DIFF_Q_BLOCK, GROUP_WIDTH), lambda b, i: (b * nq + i, 0)),
                  pl.BlockSpec((seq, GROUP_WIDTH), lambda b, i: (b, 0)),
                  pl.BlockSpec((seq, GROUP_WIDTH), lambda b, i: (b, v_col)),
                  pl.BlockSpec(lamv.shape, lambda b, i: (0, 0)),
                  pl.BlockSpec(bd64.shape, lambda b, i: (0, 0)),
                  pl.BlockSpec(g_d.shape, lambda b, i: (0, 0))],
        out_specs=pl.BlockSpec((DIFF_Q_BLOCK, GROUP_WIDTH), lambda b, i: (b * nq + i, 0)),
        out_shape=jax.ShapeDtypeStruct((t, GROUP_WIDTH), BF16),
        scratch_shapes=[pltpu.VMEM((stacked, GROUP_WIDTH), BF16),
                        pltpu.VMEM((seq // DIFF_Q_BLOCK, GROUP_WIDTH, DIFF_Q_BLOCK), BF16),
                        pltpu.VMEM((1, stacked), F32),
                        pltpu.VMEM((1, stacked), F32),
                        pltpu.VMEM((HEAD_DIM, stacked), F32)],
        compiler_params=pltpu.CompilerParams(dimension_semantics=("parallel", "arbitrary"),
                                             vmem_limit_bytes=VMEM_LIMIT),
        name="diff_attn",
    )(qdn, kdn, z, lamv, bd64, g_d)


def _odd_even_merge_sort_pairs(n):
    pairs = []

    def merge(lo, length, r):
        step = 2 * r
        if step < length:
            merge(lo, length, step)
            merge(lo + r, length, step)
            pairs.extend((i, i + r) for i in range(lo + r, lo + length - r, step))
        else:
            pairs.append((lo, lo + r))

    def sort(lo, length):
        if length > 1:
            sort(lo, length // 2)
            sort(lo + length // 2, length // 2)
            merge(lo, length, 1)

    sort(0, n)
    return pairs


_SORT16 = _odd_even_merge_sort_pairs(PEER_TOPK)


def _top16_sorted(slabs):
    n = PEER_TOPK
    assert n // 2 <= len(slabs) <= n
    a = list(slabs) + [None] * (n - len(slabs))

    def exchange(i, j):
        if a[j] is None:
            return
        if a[i] is None:
            a[i], a[j] = a[j], None
        else:
            a[i], a[j] = jnp.maximum(a[i], a[j]), jnp.minimum(a[i], a[j])

    for i, j in _SORT16:
        exchange(i, j)
    for shift in (SUBLANES // 2, SUBLANES // 4, SUBLANES // 8):
        b = [None if x is None else pltpu.roll(x, shift, axis=0) for x in a]
        for j in range(n):
            other = b[n - 1 - j]
            if a[j] is None:
                a[j] = other
            elif other is not None:
                a[j] = jnp.maximum(a[j], other)
        d = n // 2
        while d:
            for i in range(n):
                if not i & d:
                    exchange(i, i + d)
            d //= 2
    return a


def _sublane_pick(arrays):
    row = lax.broadcasted_iota(jnp.int32, arrays[0].shape, 0)
    out = arrays[0]
    for s in range(1, SUBLANES):
        out = jnp.where(row == s, arrays[s], out)
    return out


def _dup_bf16_words(x):
    bits = pltpu.bitcast(x.astype(BF16).astype(F32), jnp.uint32)
    return bits | (bits >> 16)


def _out_router_kernel(yabc_ref, yd_ref, x_ref, wo1_ref, wo2_ref, g2_ref, wq_ref, keys_ref,
                       x1_ref, xn2_ref, n1_ref, w1_ref, r2_ref, w2_ref):
    acc = (jnp.dot(yabc_ref[...], wo1_ref[...], preferred_element_type=F32)
           + jnp.dot(yd_ref[...], wo2_ref[...], preferred_element_type=F32))
    x1 = x_ref[...] + acc
    x1_ref[...] = x1
    xn2 = (x1 * lax.rsqrt(jnp.mean(x1 * x1, axis=-1, keepdims=True) + EPS) * g2_ref[...]).astype(BF16)
    xn2_ref[...] = pltpu.bitcast(xn2, jnp.uint32)
    q = jnp.dot(xn2, wq_ref[...], preferred_element_type=F32)
    tm = q.shape[0]
    for h in range(PEER_HEADS):
        scores = []
        for c in range(2):
            lo = (2 * h + c) * PEER_HALF
            qhc = q[:, lo:lo + PEER_HALF].astype(BF16)
            scores.append(lax.dot_general(keys_ref[c], qhc, NT_DIMS, preferred_element_type=F32))
        for p in range(tm // ROUTER_LANES):
            ls = slice(p * ROUTER_LANES, (p + 1) * ROUTER_LANES)
            _route_head(scores[0][:, ls], scores[1][:, ls],
                        n1_ref.at[h, :, ls], w1_ref.at[h, :, ls], r2_ref.at[h, :, ls], w2_ref.at[h, :, ls])


def _route_head(s1, s2, n1_ref, w1_ref, r2_ref, w2_ref):
    lanes = s1.shape[1]
    sl = SUBLANES
    row8 = lax.broadcasted_iota(jnp.int32, (sl, lanes), 0)
    slabs = PEER_KEYS // sl
    t1 = _top16_sorted([s1[sl * j:sl * j + sl] for j in range(slabs)])
    t2 = _top16_sorted([s2[sl * j:sl * j + sl] for j in range(slabs)])
    v1_lo, v1_hi, v2_hi = _sublane_pick(t1[:sl]), _sublane_pick(t1[sl:]), _sublane_pick(t2[sl:])
    groups = [v1_lo + t2[0], v1_hi + t2[0]]
    for b in range(1, sl):
        groups.append(jnp.where(row8 < PEER_TOPK // (b + 1), v1_lo + t2[b], NEG))
    groups.append(t1[0] + v2_hi)
    best = _top16_sorted(groups)
    tau = best[PEER_TOPK - 1]
    zsum = jnp.ones_like(tau)
    for r in range(1, PEER_TOPK):
        zsum = zsum + jnp.exp(best[r] - best[0])
    inv_z = 1.0 / zsum
    dense_b = 4
    top_rows = PEER_TOPK // (dense_b + 1)
    n1_top = []
    for a in range(top_rows):
        count = jnp.zeros_like(tau)
        for b in range(dense_b, PEER_TOPK // (a + 1)):
            count = count + jnp.where(t1[a] + t2[b] >= tau, 1.0, 0.0)
        n1_top.append(count)
    for j in range(PEER_KEYS // BF16_ROWS):
        words = slice(sl * j, sl * j + sl)
        rank2, w2 = [], []
        for half in range(2):
            ks = slice(BF16_ROWS * j + sl * half, BF16_ROWS * j + sl * half + sl)
            s1k, s2k = s1[ks], s2[ks]
            n1 = jnp.zeros((sl, lanes), F32)
            for a in reversed(range(top_rows)):
                n1 = jnp.where(s1k >= t1[a], n1_top[a], n1)
            rk = jnp.full((sl, lanes), float(PEER_TOPK), F32)
            for b in reversed(range(PEER_TOPK)):
                if b < dense_b:
                    n1 = n1 + jnp.where(s1k + t2[b] >= tau, 1.0, 0.0)
                rk = jnp.where(s2k >= t2[b], float(b), rk)
            n1_ref[ks, :] = _dup_bf16_words(n1)
            w1_ref[ks, :] = _dup_bf16_words(jnp.exp(s1k - t1[0]) * inv_z)
            rank2.append(rk)
            w2.append(jnp.exp(s2k - t2[0]))
        r2_ref[words, :] = pltpu.bitcast(jnp.concatenate(rank2, axis=0).astype(BF16), jnp.uint32)
        w2_ref[words, :] = pltpu.bitcast(jnp.concatenate(w2, axis=0).astype(BF16), jnp.uint32)


def _out_router(yabc, yd, x2d, wo1, wo2, g2, wq, keys):
    t = x2d.shape[0]
    tm = min(ROUTER_ROWS, t)

    def whole(arr):
        return pl.BlockSpec(arr.shape, lambda i: (0,) * arr.ndim)

    row_spec = pl.BlockSpec((PEER_HEADS, PEER_KEYS, tm), lambda i: (0, 0, i))
    row_shape = jax.ShapeDtypeStruct((PEER_HEADS, PEER_KEYS, t), jnp.uint32)
    tile_spec = pl.BlockSpec((PEER_HEADS, PEER_KEYS // 2, tm), lambda i: (0, 0, i))
    tile_shape = jax.ShapeDtypeStruct((PEER_HEADS, PEER_KEYS // 2, t), jnp.uint32)
    return pl.pallas_call(
        _out_router_kernel,
        grid=(t // tm,),
        in_specs=[pl.BlockSpec((tm, 3 * GROUP_WIDTH), lambda i: (i, 0)),
                  pl.BlockSpec((tm, GROUP_WIDTH), lambda i: (i, 0)),
                  pl.BlockSpec((tm, D_MODEL), lambda i: (i, 0)),
                  whole(wo1), whole(wo2), whole(g2), whole(wq), whole(keys)],
        out_specs=[pl.BlockSpec((tm, D_MODEL), lambda i: (i, 0)),
                   pl.BlockSpec((tm // 2, D_MODEL), lambda i: (i, 0)),
                   row_spec, row_spec, tile_spec, tile_spec],
        out_shape=[jax.ShapeDtypeStruct((t, D_MODEL), F32),
                   jax.ShapeDtypeStruct((t // 2, D_MODEL), jnp.uint32),
                   row_shape, row_shape, tile_shape, tile_shape],
        compiler_params=pltpu.CompilerParams(dimension_semantics=("parallel",),
                                             vmem_limit_bytes=VMEM_LIMIT),
        name="out_router",
    )(yabc, yd, x2d, wo1, wo2, g2, wq, keys)


def _gate_activations(h_ref, a_ref, n1_ref, w1_ref, r2_ref, w2_ref, first_row, rows):
    tb = h_ref.shape[1]
    zero = jnp.zeros((), BF16)
    half = PEER_KEYS // 2
    for lc in range(tb // LANES):
        ls = slice(lc * LANES, (lc + 1) * LANES)
        n1_rows = [n1_ref[hd, pl.ds(first_row, rows), ls] for hd in range(PEER_HEADS)]
        w1_rows = [w1_ref[hd, pl.ds(first_row, rows), ls] for hd in range(PEER_HEADS)]
        for r in range(rows):
            w = None
            for hd in range(PEER_HEADS):
                n1b = pltpu.bitcast(jnp.broadcast_to(n1_rows[hd][r:r + 1], (half, LANES)), BF16)
                w1b = pltpu.bitcast(jnp.broadcast_to(w1_rows[hd][r:r + 1], (half, LANES)), BF16)
                rank2 = pltpu.bitcast(r2_ref[hd, :, ls], BF16)
                w2 = pltpu.bitcast(w2_ref[hd, :, ls], BF16)
                term = jnp.where(rank2 < n1b, w2 * w1b, zero)
                w = term if w is None else w + term
            g = _gelu(h_ref[r * PEER_KEYS:(r + 1) * PEER_KEYS, ls]).astype(BF16)
            a_ref[r * half:(r + 1) * half, ls] = pltpu.bitcast(w * g, jnp.uint32)


def _peer_kernel(x_ref, xn_ref, u_ref, vt_ref, n1_ref, w1_ref, r2_ref, w2_ref, o_ref,
                 acc_ref, h_ref, a_ref, *, rows):
    j = pl.program_id(1)

    @pl.when(j == 0)
    def _():
        acc_ref[...] = jnp.zeros_like(acc_ref)

    xn = pltpu.bitcast(xn_ref[...], BF16)
    for m in range(0, u_ref.shape[0], PEER_H_ROWS):
        h_ref[m:m + PEER_H_ROWS, :] = lax.dot_general(u_ref[m:m + PEER_H_ROWS, :], xn, NT_DIMS,
                                                      preferred_element_type=F32)
    first_row = pl.multiple_of(j * rows, rows)
    _gate_activations(h_ref, a_ref, n1_ref, w1_ref, r2_ref, w2_ref, first_row, rows)
    acc_ref[...] += jnp.dot(vt_ref[...], pltpu.bitcast(a_ref[...], BF16), preferred_element_type=F32)

    @pl.when(j == pl.num_programs(1) - 1)
    def _():
        o_ref[...] = x_ref[...] + acc_ref[...].T


def _peer(x1, xn2, u_bf16, vt_bf16, n1, w1, r2, w2):
    t = x1.shape[0]
    tb = min(PEER_TOKENS, t)
    ec = PEER_EXPERT_CHUNK
    rows = ec // PEER_KEYS
    assert rows % SUBLANES == 0 and tb % LANES == 0 and t % tb == 0 and PEER_EXPERTS % ec == 0
    row_spec = pl.BlockSpec((PEER_HEADS, PEER_KEYS, tb), lambda i, j: (0, 0, i))
    tile_spec = pl.BlockSpec((PEER_HEADS, PEER_KEYS // 2, tb), lambda i, j: (0, 0, i))
    kernel = functools.partial(_peer_kernel, rows=rows)
    return pl.pallas_call(
        kernel,
        grid=(t // tb, PEER_EXPERTS // ec),
        in_specs=[pl.BlockSpec((tb, D_MODEL), lambda i, j: (i, 0)),
                  pl.BlockSpec((tb // 2, D_MODEL), lambda i, j: (i, 0)),
                  pl.BlockSpec((ec, D_MODEL), lambda i, j: (j, 0)),
                  pl.BlockSpec((D_MODEL, ec), lambda i, j: (0, j)),
                  row_spec, row_spec, tile_spec, tile_spec],
        out_specs=pl.BlockSpec((tb, D_MODEL), lambda i, j: (i, 0)),
        out_shape=jax.ShapeDtypeStruct((t, D_MODEL), F32),
        scratch_shapes=[pltpu.VMEM((D_MODEL, tb), F32),
                        pltpu.VMEM((ec, tb), F32),
                        pltpu.VMEM((ec // 2, tb), jnp.uint32)],
        compiler_params=pltpu.CompilerParams(dimension_semantics=("parallel", "arbitrary"),
                                             vmem_limit_bytes=VMEM_LIMIT),
        name="peer",
    )(x1, xn2, u_bf16, vt_bf16, n1, w1, r2, w2)


def _block_diag_ones(width, group):
    idx = np.arange(width) // group
    return jnp.asarray((idx[:, None] == idx[None, :]).astype(np.float32), BF16)


def _layer(x2d, batch, seq, layer_idx, norm1_g, w_in, w_pool, swa_q_norm, swa_k_norm, swa_sinks, conv_w,
           diff_q_norm, diff_k_norm, lam_q1, lam_k1, lam_q2, lam_k2, out_norm_g, w_out, norm2_g, w_query,
           sub_keys, peer_u, peer_v):
    lam_init = 0.8 - 0.6 * float(np.exp(-0.3 * layer_idx))
    bd64 = _block_diag_ones(GROUP_WIDTH, HEAD_DIM)
    bd32 = _block_diag_ones(GROUP_WIDTH, DIFF_QK_DIM)
    wpool_bd = jax.scipy.linalg.block_diag(*[w_pool[g] for g in range(len(POOL_WINDOWS))]).astype(BF16)

    z = _in_proj(x2d, norm1_g.reshape(1, D_MODEL), w_in.astype(BF16))
    yabc, qdn, kdn = _local_mix(
        z, batch, seq, swa_sinks.astype(F32), wpool_bd, bd64, bd32,
        jnp.tile(swa_q_norm, SWA_Q_HEADS).reshape(1, GROUP_WIDTH),
        jnp.tile(swa_k_norm, 2).reshape(1, 128),
        conv_w, out_norm_g[:3 * GROUP_WIDTH].reshape(1, 3 * GROUP_WIDTH),
        jnp.tile(diff_q_norm, 2 * DIFF_HEADS).reshape(1, GROUP_WIDTH),
        jnp.tile(diff_k_norm, 2 * DIFF_HEADS).reshape(1, GROUP_WIDTH))
    lamv = jnp.stack([lam_q1, lam_k1, lam_q2, lam_k2]).astype(F32)
    yd = _diff_attn(qdn, kdn, z, batch, seq, lamv, bd64,
                    out_norm_g[3 * GROUP_WIDTH:].reshape(1, GROUP_WIDTH), lam_init)
    w_out_bf16 = w_out.astype(BF16)
    x1, xn2, n1, w1, r2, w2 = _out_router(
        yabc, yd, x2d, w_out_bf16[:3 * GROUP_WIDTH], w_out_bf16[3 * GROUP_WIDTH:],
        norm2_g.reshape(1, D_MODEL), w_query.astype(BF16), sub_keys.astype(BF16))
    return _peer(x1, xn2, peer_u.astype(BF16), peer_v.astype(BF16).T, n1, w1, r2, w2)


def kernel(x, norm1_g, w_in, w_pool, swa_q_norm, swa_k_norm, swa_sinks, conv_w, diff_q_norm, diff_k_norm,
           lam_q1, lam_k1, lam_q2, lam_k2, out_norm_g, w_out, norm2_g, w_query, sub_keys, peer_u, peer_v):
    batch, seq, d = x.shape
    x2d = x.reshape(batch * seq, d)
    depth = norm1_g.shape[0]
    for l in range(depth):
        x2d = _layer(x2d, batch, seq, l, norm1_g[l], w_in[l], w_pool[l], swa_q_norm[l], swa_k_norm[l],
                     swa_sinks[l], conv_w[l], diff_q_norm[l], diff_k_norm[l], lam_q1[l], lam_k1[l],
                     lam_q2[l], lam_k2[l], out_norm_g[l], w_out[l], norm2_g[l], w_query[l], sub_keys[l],
                     peer_u[l], peer_v[l])
    return x2d.reshape(batch, seq, d)
```

```python
import functools

import jax
import jax.numpy as jnp
import numpy as np
from jax import lax
from jax.experimental import pallas as pl
from jax.experimental.pallas import tpu as pltpu

F32 = jnp.float32
BF16 = jnp.bfloat16

D_MODEL = 1024
EPS = 1e-6
HEAD_DIM = 64
GROUP_WIDTH = 256
POOL_WINDOWS = (2, 4, 8, 16)
SWA_Q_HEADS = 4
SWA_GQA = 2
SWA_WINDOW = 128
DIFF_HEADS = 4
DIFF_QK_DIM = 32
IN_PROJ_WIDTH = 2304
PEER_HEADS = 8
PEER_KEYS = 128
PEER_EXPERTS = PEER_KEYS * PEER_KEYS
PEER_HALF = 128
PEER_TOPK = 16
ALIBI_SLOPES = tuple(2.0 ** (-(i + 1)) for i in range(SWA_Q_HEADS + DIFF_HEADS))

LANES = 128
SUBLANES = 8
BF16_ROWS = 16

NEG = -1e30
LOG2_E = 1.4426950408889634
HALO = 16
ROW_BLOCK = 128
DIFF_Q_BLOCK = 256
IN_PROJ_ROWS = 2048
ROUTER_ROWS = 512
ROUTER_LANES = 128
PEER_TOKENS = 512
PEER_EXPERT_CHUNK = 2048
PEER_H_ROWS = 1024
VMEM_LIMIT = 56 * 1024 * 1024

NT_DIMS = (((1,), (1,)), ((), ()))


def _group_sumsq(x, bd):
    x2 = x * x
    hi = x2.astype(BF16)
    lo = (x2 - hi.astype(F32)).astype(BF16)
    return (jnp.dot(hi, bd, preferred_element_type=F32)
            + jnp.dot(lo, bd, preferred_element_type=F32))


def _group_rms_norm(x, bd, group, gain):
    return x * lax.rsqrt(_group_sumsq(x, bd) * (1.0 / group) + EPS) * gain


def _gelu(h):
    return 0.5 * h * (1.0 + lax.erf(h * np.float32(np.sqrt(0.5))))


def _in_proj_kernel(x_ref, g_ref, w_ref, z_ref):
    x = x_ref[...]
    xn = x * lax.rsqrt(jnp.mean(x * x, axis=-1, keepdims=True) + EPS) * g_ref[...]
    z = jnp.dot(xn.astype(BF16), w_ref[...], preferred_element_type=F32)
    z_ref[...] = z.astype(z_ref.dtype)


def _in_proj(x2d, g, w_bf16):
    t = x2d.shape[0]
    tm = min(IN_PROJ_ROWS, t)
    return pl.pallas_call(
        _in_proj_kernel,
        grid=(t // tm,),
        in_specs=[pl.BlockSpec((tm, D_MODEL), lambda i: (i, 0)),
                  pl.BlockSpec((1, D_MODEL), lambda i: (0, 0)),
                  pl.BlockSpec((D_MODEL, IN_PROJ_WIDTH), lambda i: (0, 0))],
        out_specs=pl.BlockSpec((tm, IN_PROJ_WIDTH), lambda i: (i, 0)),
        out_shape=jax.ShapeDtypeStruct((t, IN_PROJ_WIDTH), BF16),
        compiler_params=pltpu.CompilerParams(dimension_semantics=("parallel",),
                                             vmem_limit_bytes=VMEM_LIMIT),
        name="in_proj",
    )(x2d, g, w_bf16)


def _local_mix_kernel(sinks_ref, a_ref, a_halo_ref, bq_ref, bkv_ref, bkv_prev_ref, ch_ref, ch_halo_ref,
                      cb_ref, cc_ref, cc_halo_ref, dq_ref, dk_ref,
                      wpool_ref, bd64_ref, bd32_ref, gq_swa_ref, gk_swa_ref, conv_w_ref, g_out_ref,
                      gq_diff_ref, gk_diff_ref,
                      y_ref, qdn_ref, kdn_ref):
    n = pl.program_id(1)
    not_first = (n > 0).astype(F32)
    bd64 = bd64_ref[...]
    rows = lax.broadcasted_iota(jnp.int32, (ROW_BLOCK, GROUP_WIDTH), 0)
    cols = lax.broadcasted_iota(jnp.int32, (ROW_BLOCK, GROUP_WIDTH), 1)

    p = a_ref[...].astype(F32)
    p_ext = jnp.concatenate([a_halo_ref[...].astype(F32) * not_first, p], axis=0)
    s2 = p_ext + pltpu.roll(p_ext, 1, axis=0)
    s4 = s2 + pltpu.roll(s2, 2, axis=0)
    s8 = s4 + pltpu.roll(s4, 4, axis=0)
    s16 = s8 + pltpu.roll(s8, 8, axis=0)
    wsel = jnp.where(cols < 64, 2, jnp.where(cols < 128, 4, jnp.where(cols < 192, 8, 16)))
    ssel = jnp.where(cols < 64, s2[HALO:], jnp.where(cols < 128, s4[HALO:],
                                                      jnp.where(cols < 192, s8[HALO:], s16[HALO:])))
    cnt = jnp.minimum(n * ROW_BLOCK + rows + 1, wsel).astype(F32)
    d = ssel / cnt - p
    y_a = jnp.dot(d.astype(BF16), wpool_ref[...], preferred_element_type=F32)

    q = _group_rms_norm(bq_ref[...].astype(F32), bd64, HEAD_DIM, gq_swa_ref[...]) * (HEAD_DIM ** -0.5)
    qn = q.astype(BF16)
    bd64k = bd64[:128, :128]
    kv_cur = bkv_ref[...]
    kv_prev = bkv_prev_ref[...]
    k_cur = _group_rms_norm(kv_cur[:, :128].astype(F32), bd64k, HEAD_DIM, gk_swa_ref[...])
    k_prev = _group_rms_norm(kv_prev[:, :128].astype(F32), bd64k, HEAD_DIM, gk_swa_ref[...])
    kn = jnp.concatenate([k_prev, k_cur], axis=0).astype(BF16)
    v = jnp.concatenate([kv_prev[:, 128:], kv_cur[:, 128:]], axis=0)
    qi = lax.broadcasted_iota(jnp.int32, (ROW_BLOCK, 2 * ROW_BLOCK), 0)
    kj = lax.broadcasted_iota(jnp.int32, (ROW_BLOCK, 2 * ROW_BLOCK), 1)
    dist = qi + ROW_BLOCK - kj
    valid = (dist >= 0) & (dist < SWA_WINDOW) & ((n - 1) * ROW_BLOCK + kj >= 0)
    distf = dist.astype(F32)
    heads = []
    for h in range(SWA_Q_HEADS):
        hk = h // SWA_GQA
        sc = lax.dot_general(qn[:, h * 64:(h + 1) * 64], kn[:, hk * 64:(hk + 1) * 64], NT_DIMS,
                             preferred_element_type=F32)
        sc = jnp.where(valid, sc - ALIBI_SLOPES[h] * distf, NEG)
        sink = sinks_ref[h]
        mx = jnp.maximum(jnp.max(sc, axis=-1, keepdims=True), sink)
        e = jnp.exp(sc - mx)
        den = jnp.sum(e, axis=-1, keepdims=True) + jnp.exp(sink - mx)
        o = jnp.dot(e.astype(BF16), v[:, hk * 64:(hk + 1) * 64], preferred_element_type=F32)
        heads.append(o / den)
    y_b = jnp.concatenate(heads, axis=-1)

    zc = cc_ref[...].astype(F32) * ch_ref[...].astype(F32)
    zc_halo = cc_halo_ref[...].astype(F32) * ch_halo_ref[...].astype(F32) * not_first
    zc_ext = jnp.concatenate([zc_halo, zc], axis=0)
    cw = conv_w_ref[...]
    conv = (cw[0:1] * pltpu.roll(zc_ext, 2, axis=0)[HALO:] + cw[1:2] * pltpu.roll(zc_ext, 1, axis=0)[HALO:]
            + cw[2:3] * zc)
    y_c = cb_ref[...].astype(F32) * conv

    g_out = g_out_ref[...]
    for gi, y in enumerate((y_a, y_b, y_c)):
        lo, hi = gi * GROUP_WIDTH, (gi + 1) * GROUP_WIDTH
        y_ref[:, lo:hi] = _group_rms_norm(y, bd64, HEAD_DIM, g_out[:, lo:hi]).astype(y_ref.dtype)

    bd32 = bd32_ref[...]
    qd = (_group_rms_norm(dq_ref[...].astype(F32), bd32, DIFF_QK_DIM, gq_diff_ref[...])
          * (DIFF_QK_DIM ** -0.5 * LOG2_E))
    kd = _group_rms_norm(dk_ref[...].astype(F32), bd32, DIFF_QK_DIM, gk_diff_ref[...])
    qdn_ref[...] = qd.astype(qdn_ref.dtype)
    kdn_ref[...] = kd.astype(kdn_ref.dtype)


def _local_mix(z, batch, seq, sinks, wpool_bd, bd64, bd32, gq_swa, gk_swa, conv_w, g_out, gq_diff, gk_diff):
    t = batch * seq
    nb = seq // ROW_BLOCK
    halo_per_block = ROW_BLOCK // HALO

    def cur(col):
        return pl.BlockSpec((ROW_BLOCK, GROUP_WIDTH), lambda b, n: (b * nb + n, col))

    def prev(col):
        return pl.BlockSpec((ROW_BLOCK, GROUP_WIDTH), lambda b, n: (b * nb + jnp.maximum(n - 1, 0), col))

    def halo(col):
        return pl.BlockSpec(
            (HALO, GROUP_WIDTH),
            lambda b, n: (jnp.maximum((b * nb + n) * halo_per_block - 1, 0), col))

    def whole(arr):
        return pl.BlockSpec(arr.shape, lambda b, n: (0,) * arr.ndim)

    params = (wpool_bd, bd64, bd32, gq_swa, gk_swa, conv_w, g_out, gq_diff, gk_diff)
    in_specs = ([pl.BlockSpec(memory_space=pltpu.SMEM),
                 cur(0), halo(0), cur(1), cur(2), prev(2), cur(3), halo(3), cur(4), cur(5), halo(5),
                 cur(6), cur(7)] + [whole(a) for a in params])
    out_specs = [pl.BlockSpec((ROW_BLOCK, 3 * GROUP_WIDTH), lambda b, n: (b * nb + n, 0)),
                 pl.BlockSpec((ROW_BLOCK, GROUP_WIDTH), lambda b, n: (b * nb + n, 0)),
                 pl.BlockSpec((ROW_BLOCK, GROUP_WIDTH), lambda b, n: (b * nb + n, 0))]
    out_shape = [jax.ShapeDtypeStruct((t, 3 * GROUP_WIDTH), BF16),
                 jax.ShapeDtypeStruct((t, GROUP_WIDTH), BF16),
                 jax.ShapeDtypeStruct((t, GROUP_WIDTH), BF16)]
    return pl.pallas_call(
        _local_mix_kernel,
        grid=(batch, nb),
        in_specs=in_specs,
        out_specs=out_specs,
        out_shape=out_shape,
        compiler_params=pltpu.CompilerParams(dimension_semantics=("parallel", "parallel"),
                                             vmem_limit_bytes=VMEM_LIMIT),
        name="local_mix",
    )(sinks, *([z] * 12), *params)


def _diff_attn_kernel(q_ref, k_ref, v_ref, lamv_ref, bd64_ref, g_ref, y_ref,
                      qs_ref, vt_ref, m_ref, den_ref, acc_ref, *, one_minus_lam_init, lam_init):
    qi = pl.program_id(1)
    qb = DIFF_Q_BLOCK
    lamv = lamv_ref[...]
    lam = (jnp.exp(jnp.sum(lamv[0:1] * lamv[1:2], axis=-1, keepdims=True))
           - jnp.exp(jnp.sum(lamv[2:3] * lamv[3:4], axis=-1, keepdims=True)) + lam_init)
    n_maps = 2 * DIFF_HEADS

    @pl.when(qi == 0)
    def _():
        for c in range(vt_ref.shape[0]):
            vt_ref[c] = v_ref[c * qb:(c + 1) * qb, :].astype(F32).T.astype(BF16)

    q = q_ref[...]
    lane_group = lax.broadcasted_iota(jnp.int32, (qb, GROUP_WIDTH), 1) // DIFF_QK_DIM
    for g in range(n_maps):
        qs_ref[g * qb:(g + 1) * qb, :] = jnp.where(lane_group == g, q, jnp.zeros_like(q))
    rel = (lax.broadcasted_iota(jnp.int32, (qb, qb), 1) - lax.broadcasted_iota(jnp.int32, (qb, qb), 0))
    relf = rel.astype(F32)
    slopes = [ALIBI_SLOPES[SWA_Q_HEADS + h] * LOG2_E for h in range(DIFF_HEADS)]
    bias = [slopes[h] * relf for h in range(DIFF_HEADS)]

    def kv_step(c, diagonal):
        start = pl.multiple_of(c * qb, qb)
        sc_all = lax.dot_general(k_ref[pl.ds(start, qb), :], qs_ref[...], NT_DIMS,
                                 preferred_element_type=F32)
        off = ((qi - c) * qb).astype(F32)
        for g in range(n_maps):
            cols = slice(g * qb, (g + 1) * qb)
            sc = sc_all[:, cols] - bias[g // 2]
            if diagonal:
                sc = jnp.where(rel >= 0, sc, NEG)
            shift = slopes[g // 2] * off
            rows = slice((g // 2) * HEAD_DIM, (g // 2 + 1) * HEAD_DIM)
            if diagonal:
                m_new = jnp.max(sc, axis=0, keepdims=True) - shift
                e = jnp.exp2(sc - (m_new + shift))
                den_ref[:, cols] = jnp.sum(e, axis=0, keepdims=True)
                acc_ref[:, cols] = jnp.dot(vt_ref[c, rows, :], e.astype(BF16), preferred_element_type=F32)
            else:
                m_old = m_ref[:, cols]
                m_new = jnp.maximum(m_old, jnp.max(sc, axis=0, keepdims=True) - shift)
                alpha = jnp.exp2(m_old - m_new)
                e = jnp.exp2(sc - (m_new + shift))
                den_ref[:, cols] = alpha * den_ref[:, cols] + jnp.sum(e, axis=0, keepdims=True)
                pv = jnp.dot(vt_ref[c, rows, :], e.astype(BF16), preferred_element_type=F32)
                acc_ref[:, cols] = alpha * acc_ref[:, cols] + pv
            m_ref[:, cols] = m_new

    def off_diagonal_pair(p, carry):
        kv_step(2 * p, False)
        kv_step(2 * p + 1, False)
        return carry

    kv_step(qi, True)
    lax.fori_loop(0, qi // 2, off_diagonal_pair, 0)

    @pl.when(qi % 2 == 1)
    def _():
        kv_step(qi - 1, False)

    heads = []
    for h in range(DIFF_HEADS):
        c1 = slice(2 * h * qb, (2 * h + 1) * qb)
        c2 = slice((2 * h + 1) * qb, (2 * h + 2) * qb)
        heads.append(acc_ref[:, c1] / den_ref[:, c1] - lam * (acc_ref[:, c2] / den_ref[:, c2]))
    o = jnp.concatenate(heads, axis=0).T
    y = _group_rms_norm(o, bd64_ref[...], HEAD_DIM, g_ref[...]) * one_minus_lam_init
    y_ref[...] = y.astype(y_ref.dtype)


def _diff_attn(qdn, kdn, z, batch, seq, lamv, bd64, g_d, lam_init):
    t = batch * seq
    nq = seq // DIFF_Q_BLOCK
    v_col = 8
    stacked = 2 * DIFF_HEADS * DIFF_Q_BLOCK
    kernel = functools.partial(_diff_attn_kernel, one_minus_lam_init=1.0 - lam_init, lam_init=lam_init)
    return pl.pallas_call(
        kernel,
        grid=(batch, nq),
        in_specs=[pl.BlockSpec((DIFF_Q_BLOCK, GROUP_WIDTH), lambda b, i: (b * nq + i, 0)),
                  pl.BlockSpec((seq, GROUP_WIDTH), lambda b, i: (b, 0)),
                  pl.BlockSpec((seq, GROUP_WIDTH), lambda b, i: (b, v_col)),
                  pl.BlockSpec(lamv.shape, lambda b, i: (0, 0)),
                  pl.BlockSpec(bd64.shape, lambda b, i: (0, 0)),
                  pl.BlockSpec(g_d.shape, lambda b, i: (0, 0))],
        out_specs=pl.BlockSpec((DIFF_Q_BLOCK, GROUP_WIDTH), lambda b, i: (b * nq + i, 0)),
        out_shape=jax.ShapeDtypeStruct((t, GROUP_WIDTH), BF16),
        scratch_shapes=[pltpu.VMEM((stacked, GROUP_WIDTH), BF16),
                        pltpu.VMEM((seq // DIFF_Q_BLOCK, GROUP_WIDTH, DIFF_Q_BLOCK), BF16),
                        pltpu.VMEM((1, stacked), F32),
                        pltpu.VMEM((1, stacked), F32),
                        pltpu.VMEM((HEAD_DIM, stacked), F32)],
        compiler_params=pltpu.CompilerParams(dimension_semantics=("parallel", "arbitrary"),
                                             vmem_limit_bytes=VMEM_LIMIT),
        name="diff_attn",
    )(qdn, kdn, z, lamv, bd64, g_d)


def _odd_even_merge_sort_pairs(n):
    pairs = []

    def merge(lo, length, r):
        step = 2 * r
        if step < length:
            merge(lo, length, step)
            merge(lo + r, length, step)
            pairs.extend((i, i + r) for i in range(lo + r, lo + length - r, step))
        else:
            pairs.append((lo, lo + r))

    def sort(lo, length):
        if length > 1:
            sort(lo, length // 2)
            sort(lo + length // 2, length // 2)
            merge(lo, length, 1)

    sort(0, n)
    return pairs


_SORT16 = _odd_even_merge_sort_pairs(PEER_TOPK)


def _top16_sorted(slabs):
    n = PEER_TOPK
    assert n // 2 <= len(slabs) <= n
    a = list(slabs) + [None] * (n - len(slabs))

    def exchange(i, j):
        if a[j] is None:
            return
        if a[i] is None:
            a[i], a[j] = a[j], None
        else:
            a[i], a[j] = jnp.maximum(a[i], a[j]), jnp.minimum(a[i], a[j])

    for i, j in _SORT16:
        exchange(i, j)
    for shift in (SUBLANES // 2, SUBLANES // 4, SUBLANES // 8):
        b = [None if x is None else pltpu.roll(x, shift, axis=0) for x in a]
        for j in range(n):
            other = b[n - 1 - j]
            if a[j] is None:
                a[j] = other
            elif other is not None:
                a[j] = jnp.maximum(a[j], other)
        d = n // 2
        while d:
            for i in range(n):
                if not i & d:
                    exchange(i, i + d)
            d //= 2
    return a


def _sublane_pick(arrays):
    row = lax.broadcasted_iota(jnp.int32, arrays[0].shape, 0)
    out = arrays[0]
    for s in range(1, SUBLANES):
        out = jnp.where(row == s, arrays[s], out)
    return out


def _dup_bf16_words(x):
    bits = pltpu.bitcast(x.astype(BF16).astype(F32), jnp.uint32)
    return bits | (bits >> 16)


def _out_router_kernel(yabc_ref, yd_ref, x_ref, wo1_ref, wo2_ref, g2_ref, wq_ref, keys_ref,
                       x1_ref, xn2_ref, n1_ref, w1_ref, r2_ref, w2_ref):
    acc = (jnp.dot(yabc_ref[...], wo1_ref[...], preferred_element_type=F32)
           + jnp.dot(yd_ref[...], wo2_ref[...], preferred_element_type=F32))
    x1 = x_ref[...] + acc
    x1_ref[...] = x1
    xn2 = (x1 * lax.rsqrt(jnp.mean(x1 * x1, axis=-1, keepdims=True) + EPS) * g2_ref[...]).astype(BF16)
    xn2_ref[...] = pltpu.bitcast(xn2, jnp.uint32)
    q = jnp.dot(xn2, wq_ref[...], preferred_element_type=F32)
    tm = q.shape[0]
    for h in range(PEER_HEADS):
        scores = []
        for c in range(2):
            lo = (2 * h + c) * PEER_HALF
            qhc = q[:, lo:lo + PEER_HALF].astype(BF16)
            scores.append(lax.dot_general(keys_ref[c], qhc, NT_DIMS, preferred_element_type=F32))
        for p in range(tm // ROUTER_LANES):
            ls = slice(p * ROUTER_LANES, (p + 1) * ROUTER_LANES)
            _route_head(scores[0][:, ls], scores[1][:, ls],
                        n1_ref.at[h, :, ls], w1_ref.at[h, :, ls], r2_ref.at[h, :, ls], w2_ref.at[h, :, ls])


def _route_head(s1, s2, n1_ref, w1_ref, r2_ref, w2_ref):
    lanes = s1.shape[1]
    sl = SUBLANES
    row8 = lax.broadcasted_iota(jnp.int32, (sl, lanes), 0)
    slabs = PEER_KEYS // sl
    t1 = _top16_sorted([s1[sl * j:sl * j + sl] for j in range(slabs)])
    t2 = _top16_sorted([s2[sl * j:sl * j + sl] for j in range(slabs)])
    v1_lo, v1_hi, v2_hi = _sublane_pick(t1[:sl]), _sublane_pick(t1[sl:]), _sublane_pick(t2[sl:])
    groups = [v1_lo + t2[0], v1_hi + t2[0]]
    for b in range(1, sl):
        groups.append(jnp.where(row8 < PEER_TOPK // (b + 1), v1_lo + t2[b], NEG))
    groups.append(t1[0] + v2_hi)
    best = _top16_sorted(groups)
    tau = best[PEER_TOPK - 1]
    zsum = jnp.ones_like(tau)
    for r in range(1, PEER_TOPK):
        zsum = zsum + jnp.exp(best[r] - best[0])
    inv_z = 1.0 / zsum
    dense_b = 4
    top_rows = PEER_TOPK // (dense_b + 1)
    n1_top = []
    for a in range(top_rows):
        count = jnp.zeros_like(tau)
        for b in range(dense_b, PEER_TOPK // (a + 1)):
            count = count + jnp.where(t1[a] + t2[b] >= tau, 1.0, 0.0)
        n1_top.append(count)
    for j in range(PEER_KEYS // BF16_ROWS):
        words = slice(sl * j, sl * j + sl)
        rank2, w2 = [], []
        for half in range(2):
            ks = slice(BF16_ROWS * j + sl * half, BF16_ROWS * j + sl * half + sl)
            s1k, s2k = s1[ks], s2[ks]
            n1 = jnp.zeros((sl, lanes), F32)
            for a in reversed(range(top_rows)):
                n1 = jnp.where(s1k >= t1[a], n1_top[a], n1)
            rk = jnp.full((sl, lanes), float(PEER_TOPK), F32)
            for b in reversed(range(PEER_TOPK)):
                if b < dense_b:
                    n1 = n1 + jnp.where(s1k + t2[b] >= tau, 1.0, 0.0)
                rk = jnp.where(s2k >= t2[b], float(b), rk)
            n1_ref[ks, :] = _dup_bf16_words(n1)
            w1_ref[ks, :] = _dup_bf16_words(jnp.exp(s1k - t1[0]) * inv_z)
            rank2.append(rk)
            w2.append(jnp.exp(s2k - t2[0]))
        r2_ref[words, :] = pltpu.bitcast(jnp.concatenate(rank2, axis=0).astype(BF16), jnp.uint32)
        w2_ref[words, :] = pltpu.bitcast(jnp.concatenate(w2, axis=0).astype(BF16), jnp.uint32)


def _out_router(yabc, yd, x2d, wo1, wo2, g2, wq, keys):
    t = x2d.shape[0]
    tm = min(ROUTER_ROWS, t)

    def whole(arr):
        return pl.BlockSpec(arr.shape, lambda i: (0,) * arr.ndim)

    row_spec = pl.BlockSpec((PEER_HEADS, PEER_KEYS, tm), lambda i: (0, 0, i))
    row_shape = jax.ShapeDtypeStruct((PEER_HEADS, PEER_KEYS, t), jnp.uint32)
    tile_spec = pl.BlockSpec((PEER_HEADS, PEER_KEYS // 2, tm), lambda i: (0, 0, i))
    tile_shape = jax.ShapeDtypeStruct((PEER_HEADS, PEER_KEYS // 2, t), jnp.uint32)
    return pl.pallas_call(
        _out_router_kernel,
        grid=(t // tm,),
        in_specs=[pl.BlockSpec((tm, 3 * GROUP_WIDTH), lambda i: (i, 0)),
                  pl.BlockSpec((tm, GROUP_WIDTH), lambda i: (i, 0)),
                  pl.BlockSpec((tm, D_MODEL), lambda i: (i, 0)),
                  whole(wo1), whole(wo2), whole(g2), whole(wq), whole(keys)],
        out_specs=[pl.BlockSpec((tm, D_MODEL), lambda i: (i, 0)),
                   pl.BlockSpec((tm // 2, D_MODEL), lambda i: (i, 0)),
                   row_spec, row_spec, tile_spec, tile_spec],
        out_shape=[jax.ShapeDtypeStruct((t, D_MODEL), F32),
                   jax.ShapeDtypeStruct((t // 2, D_MODEL), jnp.uint32),
                   row_shape, row_shape, tile_shape, tile_shape],
        compiler_params=pltpu.CompilerParams(dimension_semantics=("parallel",),
                                             vmem_limit_bytes=VMEM_LIMIT),
        name="out_router",
    )(yabc, yd, x2d, wo1, wo2, g2, wq, keys)


def _gate_activations(h_ref, a_ref, n1_ref, w1_ref, r2_ref, w2_ref, first_row, rows):
    tb = h_ref.shape[1]
    zero = jnp.zeros((), BF16)
    half = PEER_KEYS // 2
    for lc in range(tb // LANES):
        ls = slice(lc * LANES, (lc + 1) * LANES)
        n1_rows = [n1_ref[hd, pl.ds(first_row, rows), ls] for hd in range(PEER_HEADS)]
        w1_rows = [w1_ref[hd, pl.ds(first_row, rows), ls] for hd in range(PEER_HEADS)]
        for r in range(rows):
            w = None
            for hd in range(PEER_HEADS):
                n1b = pltpu.bitcast(jnp.broadcast_to(n1_rows[hd][r:r + 1], (half, LANES)), BF16)
                w1b = pltpu.bitcast(jnp.broadcast_to(w1_rows[hd][r:r + 1], (half, LANES)), BF16)
                rank2 = pltpu.bitcast(r2_ref[hd, :, ls], BF16)
                w2 = pltpu.bitcast(w2_ref[hd, :, ls], BF16)
                term = jnp.where(rank2 < n1b, w2 * w1b, zero)
                w = term if w is None else w + term
            g = _gelu(h_ref[r * PEER_KEYS:(r + 1) * PEER_KEYS, ls]).astype(BF16)
            a_ref[r * half:(r + 1) * half, ls] = pltpu.bitcast(w * g, jnp.uint32)


def _peer_kernel(x_ref, xn_ref, u_ref, vt_ref, n1_ref, w1_ref, r2_ref, w2_ref, o_ref,
                 acc_ref, h_ref, a_ref, *, rows):
    j = pl.program_id(1)

    @pl.when(j == 0)
    def _():
        acc_ref[...] = jnp.zeros_like(acc_ref)

    xn = pltpu.bitcast(xn_ref[...], BF16)
    for m in range(0, u_ref.shape[0], PEER_H_ROWS):
        h_ref[m:m + PEER_H_ROWS, :] = lax.dot_general(u_ref[m:m + PEER_H_ROWS, :], xn, NT_DIMS,
                                                      preferred_element_type=F32)
    first_row = pl.multiple_of(j * rows, rows)
    _gate_activations(h_ref, a_ref, n1_ref, w1_ref, r2_ref, w2_ref, first_row, rows)
    acc_ref[...] += jnp.dot(vt_ref[...], pltpu.bitcast(a_ref[...], BF16), preferred_element_type=F32)

    @pl.when(j == pl.num_programs(1) - 1)
    def _():
        o_ref[...] = x_ref[...] + acc_ref[...].T


def _peer(x1, xn2, u_bf16, vt_bf16, n1, w1, r2, w2):
    t = x1.shape[0]
    tb = min(PEER_TOKENS, t)
    ec = PEER_EXPERT_CHUNK
    rows = ec // PEER_KEYS
    assert rows % SUBLANES == 0 and tb % LANES == 0 and t % tb == 0 and PEER_EXPERTS % ec == 0
    row_spec = pl.BlockSpec((PEER_HEADS, PEER_KEYS, tb), lambda i, j: (0, 0, i))
    tile_spec = pl.BlockSpec((PEER_HEADS, PEER_KEYS // 2, tb), lambda i, j: (0, 0, i))
    kernel = functools.partial(_peer_kernel, rows=rows)
    return pl.pallas_call(
        kernel,
        grid=(t // tb, PEER_EXPERTS // ec),
        in_specs=[pl.BlockSpec((tb, D_MODEL), lambda i, j: (i, 0)),
                  pl.BlockSpec((tb // 2, D_MODEL), lambda i, j: (i, 0)),
                  pl.BlockSpec((ec, D_MODEL), lambda i, j: (j, 0)),
                  pl.BlockSpec((D_MODEL, ec), lambda i, j: (0, j)),
                  row_spec, row_spec, tile_spec, tile_spec],
        out_specs=pl.BlockSpec((tb, D_MODEL), lambda i, j: (i, 0)),
        out_shape=jax.ShapeDtypeStruct((t, D_MODEL), F32),
        scratch_shapes=[pltpu.VMEM((D_MODEL, tb), F32),
                        pltpu.VMEM((ec, tb), F32),
                        pltpu.VMEM((ec // 2, tb), jnp.uint32)],
        compiler_params=pltpu.CompilerParams(dimension_semantics=("parallel", "arbitrary"),
                                             vmem_limit_bytes=VMEM_LIMIT),
        name="peer",
    )(x1, xn2, u_bf16, vt_bf16, n1, w1, r2, w2)


def _block_diag_ones(width, group):
    idx = np.arange(width) // group
    return jnp.asarray((idx[:, None] == idx[None, :]).astype(np.float32), BF16)


def _layer(x2d, batch, seq, layer_idx, norm1_g, w_in, w_pool, swa_q_norm, swa_k_norm, swa_sinks, conv_w,
           diff_q_norm, diff_k_norm, lam_q1, lam_k1, lam_q2, lam_k2, out_norm_g, w_out, norm2_g, w_query,
           sub_keys, peer_u, peer_v):
    lam_init = 0.8 - 0.6 * float(np.exp(-0.3 * layer_idx))
    bd64 = _block_diag_ones(GROUP_WIDTH, HEAD_DIM)
    bd32 = _block_diag_ones(GROUP_WIDTH, DIFF_QK_DIM)
    wpool_bd = jax.scipy.linalg.block_diag(*[w_pool[g] for g in range(len(POOL_WINDOWS))]).astype(BF16)

    z = _in_proj(x2d, norm1_g.reshape(1, D_MODEL), w_in.astype(BF16))
    yabc, qdn, kdn = _local_mix(
        z, batch, seq, swa_sinks.astype(F32), wpool_bd, bd64, bd32,
        jnp.tile(swa_q_norm, SWA_Q_HEADS).reshape(1, GROUP_WIDTH),
        jnp.tile(swa_k_norm, 2).reshape(1, 128),
        conv_w, out_norm_g[:3 * GROUP_WIDTH].reshape(1, 3 * GROUP_WIDTH),
        jnp.tile(diff_q_norm, 2 * DIFF_HEADS).reshape(1, GROUP_WIDTH),
        jnp.tile(diff_k_norm, 2 * DIFF_HEADS).reshape(1, GROUP_WIDTH))
    lamv = jnp.stack([lam_q1, lam_k1, lam_q2, lam_k2]).astype(F32)
    yd = _diff_attn(qdn, kdn, z, batch, seq, lamv, bd64,
                    out_norm_g[3 * GROUP_WIDTH:].reshape(1, GROUP_WIDTH), lam_init)
    w_out_bf16 = w_out.astype(BF16)
    x1, xn2, n1, w1, r2, w2 = _out_router(
        yabc, yd, x2d, w_out_bf16[:3 * GROUP_WIDTH], w_out_bf16[3 * GROUP_WIDTH:],
        norm2_g.reshape(1, D_MODEL), w_query.astype(BF16), sub_keys.astype(BF16))
    return _peer(x1, xn2, peer_u.astype(BF16), peer_v.astype(BF16).T, n1, w1, r2, w2)


def kernel(x, norm1_g, w_in, w_pool, swa_q_norm, swa_k_norm, swa_sinks, conv_w, diff_q_norm, diff_k_norm,
           lam_q1, lam_k1, lam_q2, lam_k2, out_norm_g, w_out, norm2_g, w_query, sub_keys, peer_u, peer_v):
    batch, seq, d = x.shape
    x2d = x.reshape(batch * seq, d)
    depth = norm1_g.shape[0]
    for l in range(depth):
        x2d = _layer(x2d, batch, seq, l, norm1_g[l], w_in[l], w_pool[l], swa_q_norm[l], swa_k_norm[l],
                     swa_sinks[l], conv_w[l], diff_q_norm[l], diff_k_norm[l], lam_q1[l], lam_k1[l],
                     lam_q2[l], lam_k2[l], out_norm_g[l], w_out[l], norm2_g[l], w_query[l], sub_keys[l],
                     peer_u[l], peer_v[l])
    return x2d.reshape(batch, seq, d)
```
